```python
import math
import jax, jax.numpy as jnp
from jax import lax
import numpy as np

D_MODEL = 2048
BATCH = 16
SEQ = 256
DEPTH = 4
DEC_BATCH = 2
DEC_SEQ = 4096
PAST_LEN = 256

GRID_W = 64
NA_HEADS = 8
NA_DIM = 128
NA_WIN_R = 8
NA_WIN_C = 16
SG_GROUPS = 4
SG_DIM = 128
SG_CHUNK = 128
DF_HEADS = 4
DF_QK = 64
DF_V = 128
ROPE_THETA = 10000.0
NA_WIDTH = NA_HEADS * NA_DIM
SG_WIDTH = SG_GROUPS * SG_DIM
DF_WIDTH = DF_HEADS * DF_V
MIX_WIDTH = NA_WIDTH + SG_WIDTH + DF_WIDTH
IN_WIDTH = 3 * NA_WIDTH + 2 * SG_WIDTH + 4 * DF_HEADS * DF_QK + DF_WIDTH
PK_HEADS = 8
PK_QDIM = 256
PK_NKEYS = 128
PK_TOPK = 16
PK_EXPERTS = PK_NKEYS * PK_NKEYS
PK_TOKEN_BLOCK = 128
ATTN_BLOCK = 128
DN_ALPHA = (2 * DEPTH) ** 0.25
DN_BETA = (8 * DEPTH) ** -0.25
LN_EPS = 1e-5
RMS_EPS = 1e-6

kernel_name = 'hybrid_natten_sgmlp_diffattn_peer_denoise_step'


def _layernorm(x, g, b):
    xf = x.astype(jnp.float32)
    mu = jnp.mean(xf, -1, keepdims=True)
    var = jnp.mean(jnp.square(xf - mu), -1, keepdims=True)
    return ((xf - mu) * lax.rsqrt(var + LN_EPS) * g.astype(jnp.float32) + b.astype(jnp.float32)).astype(x.dtype)


def _rmsnorm(x, g):
    xf = x.astype(jnp.float32)
    return (xf * lax.rsqrt(jnp.mean(xf * xf, -1, keepdims=True) + RMS_EPS) * g.astype(jnp.float32)).astype(x.dtype)


def _heads(a, n):
    B, T, _ = a.shape
    return a.reshape(B, T, n, -1).transpose(0, 2, 1, 3)


def _merge(o):
    B, H, T, d = o.shape
    return o.transpose(0, 2, 1, 3).reshape(B, T, H * d)


def _project(h, w_in_l):
    widths = [NA_WIDTH] * 3 + [SG_WIDTH] * 2 + [DF_HEADS * DF_QK] * 4 + [DF_WIDTH]
    cuts = np.cumsum(widths)[:-1].tolist()
    return jnp.split(h @ w_in_l, cuts, axis=-1)


def _axial_rope(n_tokens, dim, dtype):
    t = jnp.arange(n_tokens)
    row = (t // GRID_W).astype(jnp.float32)
    col = (t % GRID_W).astype(jnp.float32)
    n_freq = dim // 4
    inv = 1.0 / (ROPE_THETA ** (jnp.arange(n_freq, dtype=jnp.float32) / n_freq))
    ang = jnp.concatenate([row[:, None] * inv, col[:, None] * inv], -1)
    return jnp.cos(ang).astype(dtype), jnp.sin(ang).astype(dtype)


def _apply_rope(x, cos, sin):
    x1, x2 = jnp.split(x, 2, -1)
    return jnp.concatenate([x1 * cos - x2 * sin, x1 * sin + x2 * cos], -1)


def _sweep_queries(fn, qs):
    B, H, T, _ = qs[0].shape
    nb = T // ATTN_BLOCK
    blocks = tuple(q.reshape(B, H, nb, ATTN_BLOCK, q.shape[-1]).transpose(2, 0, 1, 3, 4) for q in qs)
    out = lax.map(lambda blk: fn(*blk), blocks)
    return out.transpose(1, 2, 0, 3, 4).reshape(B, H, T, out.shape[-1])


def _softmax_attn(q, k, v):
    scale = q.shape[-1] ** -0.5
    def blk(qb):
        s = jnp.einsum('bhqd,bhkd->bhqk', qb, k).astype(jnp.float32) * scale
        p = jax.nn.softmax(s, -1).astype(v.dtype)
        return jnp.einsum('bhqk,bhkd->bhqd', p, v)
    return _sweep_queries(blk, (q,))


def _diff_lambda(lp, lam_init):
    lf = lp.astype(jnp.float32)
    return jnp.exp(jnp.sum(lf[0] * lf[1])) - jnp.exp(jnp.sum(lf[2] * lf[3])) + lam_init


def _diff_attn(q1, q2, k1, k2, v, lam):
    scale = DF_QK ** -0.5
    def blk(a, b):
        p1 = jax.nn.softmax(jnp.einsum('bhqd,bhkd->bhqk', a, k1).astype(jnp.float32) * scale, -1)
        p2 = jax.nn.softmax(jnp.einsum('bhqd,bhkd->bhqk', b, k2).astype(jnp.float32) * scale, -1)
        return jnp.einsum('bhqk,bhkd->bhqd', (p1 - lam * p2).astype(v.dtype), v)
    return _sweep_queries(blk, (q1, q2))


def _neighbourhood_attn(q, k, v, k_ctx, v_ctx, rpb):
    B, H, T, d = q.shape
    rows = T // GRID_W
    wr = min(NA_WIN_R, rows)
    wc = NA_WIN_C
    scale = d ** -0.5
    qg = q.reshape(B, H, rows, GRID_W, d)
    kg = k.reshape(B, H, rows, GRID_W, d)
    vg = v.reshape(B, H, rows, GRID_W, d)
    cols = jnp.arange(GRID_W)
    col_start = jnp.clip(cols - wc // 2, 0, GRID_W - wc)
    col_idx = col_start[:, None] + jnp.arange(wc)[None, :]
    dc = col_idx - cols[:, None] + (NA_WIN_C - 1)

    def row_step(r):
        r0 = jnp.clip(r - wr // 2, 0, rows - wr)
        kb = lax.dynamic_slice_in_dim(kg, r0, wr, axis=2)
        vb = lax.dynamic_slice_in_dim(vg, r0, wr, axis=2)
        kw = jnp.take(kb, col_idx, axis=3)
        vw = jnp.take(vb, col_idx, axis=3)
        qr = lax.dynamic_index_in_dim(qg, r, axis=2, keepdims=False)
        dr = r0 + jnp.arange(wr) - r + (NA_WIN_R - 1)
        bias = rpb[:, dr[:, None, None], dc[None, :, :]].transpose(0, 2, 1, 3)
        s_loc = jnp.einsum('bhqd,bhrqjd->bhqrj', qr, kw).astype(jnp.float32) * scale + bias[None].astype(jnp.float32)
        s_ctx = jnp.einsum('bhqd,bhkd->bhqk', qr, k_ctx).astype(jnp.float32) * scale
        s = jnp.concatenate([s_loc.reshape(B, H, GRID_W, wr * wc), s_ctx], -1)
        p = jax.nn.softmax(s, -1).astype(v.dtype)
        p_loc = p[..., :wr * wc].reshape(B, H, GRID_W, wr, wc)
        p_ctx = p[..., wr * wc:]
        return jnp.einsum('bhqrj,bhrqjd->bhqd', p_loc, vw) + jnp.einsum('bhqk,bhkd->bhqd', p_ctx, v_ctx)

    out = lax.map(row_step, jnp.arange(rows))
    return out.transpose(1, 2, 0, 3, 4).reshape(B, H, T, d)


def _spatial_gate(u, v, ln_g, ln_b, w_s, b_s):
    B, T, _ = u.shape
    n = T // SG_CHUNK
    vn = _layernorm(v, ln_g, ln_b).reshape(B, n, SG_CHUNK, SG_GROUPS, SG_DIM)
    mixed = jnp.einsum('gpq,bnqgc->bnpgc', w_s, vn) + b_s.T[None, None, :, :, None]
    return u * mixed.reshape(B, T, SG_WIDTH)


def _mix_out(oa, ob, oc, subln_g, lam_init, w_out_l):
    oc = _rmsnorm(oc, subln_g) * (1.0 - lam_init)
    return jnp.concatenate([_merge(oa), ob, _merge(oc)], -1) @ w_out_l


def _mix_context(h, w_in_l, sg_ln_g_l, sg_ln_b_l, sg_w_l, sg_b_l, lam, subln_g, lam_init, w_out_l):
    qa, ka, va, u, v, q1, q2, k1, k2, vc = _project(h, w_in_l)
    qa, ka, va = (_heads(t, NA_HEADS) for t in (qa, ka, va))
    q1, q2, k1, k2, vc = (_heads(t, DF_HEADS) for t in (q1, q2, k1, k2, vc))
    oa = _softmax_attn(qa, ka, va)
    ob = _spatial_gate(u, v, sg_ln_g_l, sg_ln_b_l, sg_w_l, sg_b_l)
    oc = _diff_attn(q1, q2, k1, k2, vc, lam)
    return _mix_out(oa, ob, oc, subln_g, lam_init, w_out_l), (ka, va, k1, k2, vc)


def _mix_latent(h, caches, rpb, w_in_l, sg_ln_g_l, sg_ln_b_l, sg_w_l, sg_b_l, lam, subln_g, lam_init, w_out_l):
    ka_c, va_c, k1_c, k2_c, v_c = caches
    qa, ka, va, u, v, q1, q2, k1, k2, vc = _project(h, w_in_l)
    qa, ka, va = (_heads(t, NA_HEADS) for t in (qa, ka, va))
    cos, sin = _axial_rope(h.shape[1], DF_QK, h.dtype)
    q1, q2, k1, k2 = (_apply_rope(_heads(t, DF_HEADS), cos, sin) for t in (q1, q2, k1, k2))
    vc = _heads(vc, DF_HEADS)
    oa = _neighbourhood_attn(qa, ka, va, ka_c, va_c, rpb)
    ob = _spatial_gate(u, v, sg_ln_g_l, sg_ln_b_l, sg_w_l, sg_b_l)
    oc = _diff_attn(q1, q2, jnp.concatenate([k1, k1_c], 2), jnp.concatenate([k2, k2_c], 2),
                    jnp.concatenate([vc, v_c], 2), lam)
    return _mix_out(oa, ob, oc, subln_g, lam_init, w_out_l), ()


def _peer(x, w_q, sub_keys, u_tab, v_tab):
    B, T, D = x.shape
    n = B * T
    xt = x.reshape(n, D)
    q = (xt @ w_q).reshape(n, PK_HEADS, 2, PK_QDIM // 2)
    s = jnp.einsum('nhad,hakd->nhak', q, sub_keys).astype(jnp.float32)
    sv, si = lax.top_k(s, PK_TOPK)
    cand = sv[:, :, 0, :, None] + sv[:, :, 1, None, :]
    cand_idx = si[:, :, 0, :, None] * PK_NKEYS + si[:, :, 1, None, :]
    best, pos = lax.top_k(cand.reshape(n, PK_HEADS, PK_TOPK * PK_TOPK), PK_TOPK)
    idx = jnp.take_along_axis(cand_idx.reshape(n, PK_HEADS, PK_TOPK * PK_TOPK), pos, -1)
    g = jax.nn.softmax(best, -1).astype(x.dtype)
    nb = n // PK_TOKEN_BLOCK
    hk = PK_HEADS * PK_TOPK

    def blk(args):
        xb, ib, gb = args
        ue = jnp.take(u_tab, ib, axis=0)
        act = jax.nn.gelu(jnp.einsum('nkd,nd->nk', ue, xb), approximate=False)
        ve = jnp.take(v_tab, ib, axis=0)
        return jnp.einsum('nk,nkd->nd', gb * act, ve)

    out = lax.map(blk, (xt.reshape(nb, PK_TOKEN_BLOCK, D), idx.reshape(nb, PK_TOKEN_BLOCK, hk),
                        g.reshape(nb, PK_TOKEN_BLOCK, hk)))
    return out.reshape(B, T, D)


def _trunk_layer(x, cond, mix_fn, w_mod_l, b_mod_l, ln_g_l, ln_b_l, peer_w):
    mod = jax.nn.silu(cond) @ w_mod_l + b_mod_l
    sh1, sc1, g1, sh2, sc2, g2 = (m[..., None, :] for m in jnp.split(mod, 6, -1))
    y, ctx_tensors = mix_fn(x * (1 + sc1) + sh1)
    x = _layernorm(DN_ALPHA * x + g1 * y, ln_g_l[0], ln_b_l[0])
    y = _peer(x * (1 + sc2) + sh2, *peer_w)
    x = _layernorm(DN_ALPHA * x + g2 * y, ln_g_l[1], ln_b_l[1])
    return x, ctx_tensors


def setup_inputs(seed: int = 0) -> dict:
    key = jax.random.key(seed)
    ks = jax.random.split(key, 32)
    def nrm(k, shape, s=1.0):
        return jax.random.normal(k, shape, jnp.float32) * s
    L = PAST_LEN
    return {
        'x_prompt': nrm(ks[0], (BATCH, SEQ, D_MODEL)),
        'x_sample': nrm(ks[1], (DEC_BATCH, DEC_SEQ, D_MODEL)),
        'cache_na_k': nrm(ks[2], (DEC_BATCH, DEPTH, NA_HEADS, L, NA_DIM)),
        'cache_na_v': nrm(ks[3], (DEC_BATCH, DEPTH, NA_HEADS, L, NA_DIM)),
        'cache_df_k1': nrm(ks[4], (DEC_BATCH, DEPTH, DF_HEADS, L, DF_QK)),
        'cache_df_k2': nrm(ks[5], (DEC_BATCH, DEPTH, DF_HEADS, L, DF_QK)),
        'cache_df_v': nrm(ks[6], (DEC_BATCH, DEPTH, DF_HEADS, L, DF_V)),
        'c': nrm(ks[7], (DEC_BATCH, D_MODEL)),
        'c_ctx': nrm(ks[8], (D_MODEL,)),
        'w_mod': nrm(ks[9], (DEPTH, D_MODEL, 6 * D_MODEL), 0.5 * D_MODEL ** -0.5),
        'b_mod': nrm(ks[10], (DEPTH, 6 * D_MODEL), 0.02),
        'w_in': nrm(ks[11], (DEPTH, D_MODEL, IN_WIDTH), D_MODEL ** -0.5),
        'na_rpb': nrm(ks[12], (DEPTH, NA_HEADS, 2 * NA_WIN_R - 1, 2 * NA_WIN_C - 1), 0.05),
        'sg_ln_g': 1.0 + nrm(ks[13], (DEPTH, SG_WIDTH), 0.02),
        'sg_ln_b': nrm(ks[14], (DEPTH, SG_WIDTH), 0.02),
        'sg_w': nrm(ks[15], (DEPTH, SG_GROUPS, SG_CHUNK, SG_CHUNK), SG_CHUNK ** -0.5),
        'sg_b': 1.0 + nrm(ks[16], (DEPTH, SG_GROUPS, SG_CHUNK), 0.02),
        'df_lambda': nrm(ks[17], (DEPTH, 4, DF_QK), 0.1),
        'df_subln_g': 1.0 + nrm(ks[18], (DEPTH, DF_V), 0.02),
        'w_out': nrm(ks[19], (DEPTH, MIX_WIDTH, D_MODEL), DN_BETA * MIX_WIDTH ** -0.5),
        'pk_wq': nrm(ks[20], (DEPTH, D_MODEL, PK_HEADS * PK_QDIM), D_MODEL ** -0.5),
        'pk_keys': nrm(ks[21], (DEPTH, PK_HEADS, 2, PK_NKEYS, PK_QDIM // 2), (PK_QDIM // 2) ** -0.5),
        'pk_u': nrm(ks[22], (DEPTH, PK_EXPERTS, D_MODEL), D_MODEL ** -0.5),
        'pk_v': nrm(ks[23], (DEPTH, PK_EXPERTS, D_MODEL), DN_BETA * PK_HEADS ** -0.5),
        'ln_g': 1.0 + nrm(ks[24], (DEPTH, 2, D_MODEL), 0.02),
        'ln_b': nrm(ks[25], (DEPTH, 2, D_MODEL), 0.02),
    }


def reference(x_prompt, x_sample, cache_na_k, cache_na_v, cache_df_k1, cache_df_k2, cache_df_v, c, c_ctx,
              w_mod, b_mod, w_in, na_rpb, sg_ln_g, sg_ln_b, sg_w, sg_b, df_lambda, df_subln_g, w_out,
              pk_wq, pk_keys, pk_u, pk_v, ln_g, ln_b):
    xp, xs = x_prompt, x_sample
    st_na_k, st_na_v, st_df_k1, st_df_k2, st_df_v = [], [], [], [], []
    for l in range(DEPTH):
        lam_init = 0.8 - 0.6 * math.exp(-0.3 * l)
        lam = _diff_lambda(df_lambda[l], lam_init)
        shared = (w_in[l], sg_ln_g[l], sg_ln_b[l], sg_w[l], sg_b[l], lam, df_subln_g[l], lam_init, w_out[l])
        peer_w = (pk_wq[l], pk_keys[l], pk_u[l], pk_v[l])
        xp, (ka, va, k1, k2, vc) = _trunk_layer(xp, c_ctx, lambda h: _mix_context(h, *shared),
                                                w_mod[l], b_mod[l], ln_g[l], ln_b[l], peer_w)
        st_na_k.append(ka)
        st_na_v.append(va)
        st_df_k1.append(k1)
        st_df_k2.append(k2)
        st_df_v.append(vc)
        caches = (cache_na_k[:, l], cache_na_v[:, l], cache_df_k1[:, l], cache_df_k2[:, l], cache_df_v[:, l])
        xs, _ = _trunk_layer(xs, c, lambda h: _mix_latent(h, caches, na_rpb[l], *shared),
                             w_mod[l], b_mod[l], ln_g[l], ln_b[l], peer_w)
    y_prompt = xp
    y_sample = xs
    na_k = jnp.stack(st_na_k, 1)
    na_v = jnp.stack(st_na_v, 1)
    df_k1 = jnp.stack(st_df_k1, 1)
    df_k2 = jnp.stack(st_df_k2, 1)
    df_v = jnp.stack(st_df_v, 1)
    return (y_prompt, y_sample, na_k, na_v, df_k1, df_k2, df_v)
```

```python
import functools
import math

import numpy as np
import jax
import jax.numpy as jnp
from jax import lax
from jax.experimental import pallas as pl
from jax.experimental.pallas import tpu as pltpu

F32 = jnp.float32
BF16 = jnp.bfloat16

D_MODEL = 2048
DEPTH = 4
GRID_W = 64
NA_HEADS = 8
NA_DIM = 128
NA_WIN_R = 8
NA_WIN_C = 16
SG_GROUPS = 4
SG_DIM = 128
SG_CHUNK = 128
DF_HEADS = 4
DF_QK = 64
DF_V = 128
ROPE_THETA = 10000.0
NA_WIDTH = NA_HEADS * NA_DIM
SG_WIDTH = SG_GROUPS * SG_DIM
DF_WIDTH = DF_HEADS * DF_V
MIX_WIDTH = NA_WIDTH + SG_WIDTH + DF_WIDTH
IN_WIDTH = 3 * NA_WIDTH + 2 * SG_WIDTH + 4 * DF_HEADS * DF_QK + DF_WIDTH
PK_HEADS = 8
PK_QDIM = 256
PK_NKEYS = 128
PK_TOPK = 16
PK_EXPERTS = PK_NKEYS * PK_NKEYS
DN_ALPHA = (2 * DEPTH) ** 0.25
LN_EPS = 1e-5
RMS_EPS = 1e-6

OFF_QA = 0
OFF_KA = NA_WIDTH
OFF_VA = 2 * NA_WIDTH
OFF_U = 3 * NA_WIDTH
OFF_V = OFF_U + SG_WIDTH
OFF_DQ = OFF_V + SG_WIDTH
OFF_DK = OFF_DQ + 2 * DF_HEADS * DF_QK
OFF_DV = OFF_DK + 2 * DF_HEADS * DF_QK

LANES = 128
MIB = 1024 * 1024

TM_IN = 512
TN_IN = IN_WIDTH // 4
TM_OUT = 512
TM_SCORE = 256
TM_PEER = 512
TE_PEER = 512
RJ_PEER = 16
TQ_DF = 256
TN_MOD = 1024
NEG_BIG = -1e30


def _cp(sem, vmem_mib):
    return pltpu.CompilerParams(dimension_semantics=sem, vmem_limit_bytes=vmem_mib * MIB)


def _dot(a, b):
    return jnp.dot(a, b, preferred_element_type=F32)


def _dot_nt(a, b):
    return lax.dot_general(a, b, (((1,), (1,)), ((), ())), preferred_element_type=F32)


def _layernorm(z, g, b):
    mu = jnp.mean(z, -1, keepdims=True)
    d = z - mu
    var = jnp.mean(d * d, -1, keepdims=True)
    return d * lax.rsqrt(var + LN_EPS) * g + b


def _softmax_rows(s):
    m = jnp.max(s, -1, keepdims=True)
    e = jnp.exp(s - m)
    return e * (1.0 / jnp.sum(e, -1, keepdims=True))


def _diff_lambda(lam_ref, lam_init):
    lf = lam_ref[...]
    a = jnp.sum(lf[0:1, :] * lf[1:2, :], axis=1, keepdims=True)
    b = jnp.sum(lf[2:3, :] * lf[3:4, :], axis=1, keepdims=True)
    return jnp.exp(a) - jnp.exp(b) + lam_init


def _map1_mask():
    lane = lax.broadcasted_iota(jnp.int32, (1, LANES), 1)
    return (lane // (DF_QK // 2)) % 2 == 0


def _subln(o, g, lam_init):
    return o * lax.rsqrt(jnp.mean(o * o, -1, keepdims=True) + RMS_EPS) * g * (1.0 - lam_init)


def _mod_kernel(c_ref, w_ref, b_ref, o_ref):
    c = c_ref[...]
    a = (c * jax.nn.sigmoid(c)).astype(BF16)
    o_ref[0] = _dot(a, w_ref[0].astype(BF16)) + b_ref[0]


def _modulation(cond8, w_mod, b_mod):
    depth, d, n6 = w_mod.shape
    return pl.pallas_call(
        _mod_kernel,
        grid=(depth, n6 // TN_MOD),
        in_specs=[pl.BlockSpec((8, d), lambda l, j: (0, 0)),
                  pl.BlockSpec((1, d, TN_MOD), lambda l, j: (l, 0, j)),
                  pl.BlockSpec((1, 1, TN_MOD), lambda l, j: (l, 0, j))],
        out_specs=pl.BlockSpec((1, 8, TN_MOD), lambda l, j: (l, 0, j)),
        out_shape=jax.ShapeDtypeStruct((depth, 8, n6), F32),
        compiler_params=_cp(("parallel", "parallel"), 32),
        name="modulation",
    )(cond8, w_mod, b_mod.reshape(depth, 1, n6))


def _in_proj_kernel(x_ref, mod_ref, w_ref, o_ref, h_ref):
    @pl.when(pl.program_id(1) == 0)
    def _():
        sh = mod_ref[0, 0:1, :]
        sc = mod_ref[0, 1:2, :]
        h_ref[...] = (x_ref[...] * (1 + sc) + sh).astype(BF16)

    o_ref[...] = _dot(h_ref[...], w_ref[...])


def _in_proj(x, mod, w, tiles_per_mod):
    n, d = x.shape
    nw = w.shape[1]
    return pl.pallas_call(
        _in_proj_kernel,
        grid=(n // TM_IN, nw // TN_IN),
        in_specs=[pl.BlockSpec((TM_IN, d), lambda i, j: (i, 0)),
                  pl.BlockSpec((1, 6, d), lambda i, j: (i // tiles_per_mod, 0, 0)),
                  pl.BlockSpec((d, TN_IN), lambda i, j: (0, j))],
        out_specs=pl.BlockSpec((TM_IN, TN_IN), lambda i, j: (i, j)),
        out_shape=jax.ShapeDtypeStruct((n, nw), F32),
        scratch_shapes=[pltpu.VMEM((TM_IN, d), BF16)],
        compiler_params=_cp(("parallel", "arbitrary"), 40),
        name="in_proj",
    )(x, mod, w)


def _ctx_attn_kernel(p_ref, lam_ref, g_ref, oa_ref, oc_ref, *, lam_init):
    scale = NA_DIM ** -0.5
    for h in range(NA_HEADS):
        sl = slice(h * NA_DIM, (h + 1) * NA_DIM)
        q = p_ref[0, :, OFF_QA + h * NA_DIM:OFF_QA + (h + 1) * NA_DIM].astype(BF16)
        k = p_ref[0, :, OFF_KA + h * NA_DIM:OFF_KA + (h + 1) * NA_DIM].astype(BF16)
        v = p_ref[0, :, OFF_VA + h * NA_DIM:OFF_VA + (h + 1) * NA_DIM].astype(BF16)
        p = _softmax_rows(_dot_nt(q, k) * scale)
        oa_ref[0, :, sl] = _dot(p.astype(BF16), v).astype(BF16)
    lam = _diff_lambda(lam_ref, lam_init)
    m1 = _map1_mask()
    dscale = DF_QK ** -0.5
    for h in range(DF_HEADS):
        sl = slice(h * DF_V, (h + 1) * DF_V)
        q = p_ref[0, :, OFF_DQ + h * LANES:OFF_DQ + (h + 1) * LANES]
        k = p_ref[0, :, OFF_DK + h * LANES:OFF_DK + (h + 1) * LANES].astype(BF16)
        v = p_ref[0, :, OFF_DV + h * DF_V:OFF_DV + (h + 1) * DF_V].astype(BF16)
        qa = jnp.where(m1, q, 0.0).astype(BF16)
        qb = jnp.where(m1, 0.0, q).astype(BF16)
        p = _softmax_rows(_dot_nt(qa, k) * dscale) - lam * _softmax_rows(_dot_nt(qb, k) * dscale)
        o = _dot(p.astype(BF16), v)
        oc_ref[0, :, sl] = _subln(o, g_ref[...], lam_init).astype(BF16)


def _ctx_attn(proj3, lam_p, subln_g, lam_init):
    b, t, nw = proj3.shape
    return pl.pallas_call(
        functools.partial(_ctx_attn_kernel, lam_init=lam_init),
        grid=(b,),
        in_specs=[pl.BlockSpec((1, t, nw), lambda i: (i, 0, 0)),
                  pl.BlockSpec((4, DF_QK), lambda i: (0, 0)),
                  pl.BlockSpec((1, DF_V), lambda i: (0, 0))],
        out_specs=[pl.BlockSpec((1, t, NA_WIDTH), lambda i: (i, 0, 0)),
                   pl.BlockSpec((1, t, DF_WIDTH), lambda i: (i, 0, 0))],
        out_shape=[jax.ShapeDtypeStruct((b, t, NA_WIDTH), BF16),
                   jax.ShapeDtypeStruct((b, t, DF_WIDTH), BF16)],
        compiler_params=_cp(("parallel",), 32),
        name="ctx_attn",
    )(proj3, lam_p, subln_g.reshape(1, DF_V))


def _lat_na_kernel(q_ref, k_ref, v_ref, kc_ref, vc_ref, bias_ref, o_ref, kb_ref, vb_ref, *, rows):
    kb_ref[...] = k_ref[0].astype(BF16)
    vb_ref[...] = v_ref[0].astype(BF16)
    kc = kc_ref[0, 0, 0].astype(BF16)
    vc = vc_ref[0, 0, 0].astype(BF16)
    scale = NA_DIM ** -0.5
    band = NA_WIN_R * GRID_W

    def body(r, carry):
        r0 = jnp.clip(r - NA_WIN_R // 2, 0, rows - NA_WIN_R)
        q = q_ref[0, pl.ds(pl.multiple_of(r * GRID_W, GRID_W), GRID_W), :].astype(BF16)
        k0 = pl.multiple_of(r0 * GRID_W, GRID_W)
        kw = kb_ref[pl.ds(k0, band), :]
        vw = vb_ref[pl.ds(k0, band), :]
        s_loc = _dot_nt(q, kw) * scale + bias_ref[0, r - r0]
        s_ctx = _dot_nt(q, kc) * scale
        m = jnp.maximum(jnp.max(s_loc, -1, keepdims=True), jnp.max(s_ctx, -1, keepdims=True))
        e_loc = jnp.exp(s_loc - m)
        e_ctx = jnp.exp(s_ctx - m)
        inv = 1.0 / (jnp.sum(e_loc, -1, keepdims=True) + jnp.sum(e_ctx, -1, keepdims=True))
        o = _dot((e_loc * inv).astype(BF16), vw) + _dot((e_ctx * inv).astype(BF16), vc)
        o_ref[0, pl.ds(pl.multiple_of(r * GRID_W, GRID_W), GRID_W), :] = o.astype(BF16)
        return carry

    lax.fori_loop(0, rows, body, 0)


def _lat_na(proj3, cache_k, cache_v, bias, layer):
    b, t, _ = proj3.shape
    past = cache_k.shape[3]
    rows = t // GRID_W
    qb, kb, vb = OFF_QA // NA_DIM, OFF_KA // NA_DIM, OFF_VA // NA_DIM
    return pl.pallas_call(
        functools.partial(_lat_na_kernel, rows=rows),
        grid=(b, NA_HEADS),
        in_specs=[pl.BlockSpec((1, t, NA_DIM), lambda i, h: (i, 0, qb + h)),
                  pl.BlockSpec((1, t, NA_DIM), lambda i, h: (i, 0, kb + h)),
                  pl.BlockSpec((1, t, NA_DIM), lambda i, h: (i, 0, vb + h)),
                  pl.BlockSpec((1, 1, 1, past, NA_DIM), lambda i, h: (i, layer, h, 0, 0)),
                  pl.BlockSpec((1, 1, 1, past, NA_DIM), lambda i, h: (i, layer, h, 0, 0)),
                  pl.BlockSpec((1, NA_WIN_R, GRID_W, NA_WIN_R * GRID_W), lambda i, h: (h, 0, 0, 0))],
        out_specs=pl.BlockSpec((1, t, NA_DIM), lambda i, h: (i, 0, h)),
        out_shape=jax.ShapeDtypeStruct((b, t, NA_WIDTH), BF16),
        scratch_shapes=[pltpu.VMEM((t, NA_DIM), BF16), pltpu.VMEM((t, NA_DIM), BF16)],
        compiler_params=_cp(("parallel", "parallel"), 40),
        name="lat_na",
    )(proj3, proj3, proj3, cache_k, cache_v, bias)


def _lat_df_kernel(q_ref, k_ref, v_ref, kc_ref, vc_ref, cos_ref, sin_ref, lam_ref, g_ref, o_ref,
                   kall_ref, vall_ref, *, t, lam_init):
    qi = pl.program_id(2)

    @pl.when(qi == 0)
    def _():
        k = k_ref[0]
        kall_ref[0:t, :] = (k * cos_ref[...] + pltpu.roll(k, LANES // 2, 1) * sin_ref[...]).astype(BF16)
        kall_ref[t:, :] = kc_ref[0, 0, 0].astype(BF16)
        vall_ref[0:t, :] = v_ref[0].astype(BF16)
        vall_ref[t:, :] = vc_ref[0, 0, 0].astype(BF16)

    t0 = pl.multiple_of(qi * TQ_DF, TQ_DF)
    q = q_ref[0]
    q = q * cos_ref[pl.ds(t0, TQ_DF), :] + pltpu.roll(q, LANES // 2, 1) * sin_ref[pl.ds(t0, TQ_DF), :]
    m1 = _map1_mask()
    qa = jnp.where(m1, q, 0.0).astype(BF16)
    qb = jnp.where(m1, 0.0, q).astype(BF16)
    dscale = DF_QK ** -0.5
    lam = _diff_lambda(lam_ref, lam_init)
    kall = kall_ref[...]
    p = _softmax_rows(_dot_nt(qa, kall) * dscale) - lam * _softmax_rows(_dot_nt(qb, kall) * dscale)
    o = _dot(p.astype(BF16), vall_ref[...])
    o_ref[0] = _subln(o, g_ref[...], lam_init).astype(BF16)


def _lat_df(proj3, cache_kp, cache_v, cos_t, sin_t, lam_p, subln_g, lam_init, layer):
    b, t, _ = proj3.shape
    past = cache_v.shape[3]
    qb, kb, vb = OFF_DQ // LANES, OFF_DK // LANES, OFF_DV // LANES
    return pl.pallas_call(
        functools.partial(_lat_df_kernel, t=t, lam_init=lam_init),
        grid=(b, DF_HEADS, t // TQ_DF),
        in_specs=[pl.BlockSpec((1, TQ_DF, LANES), lambda i, h, j: (i, j, qb + h)),
                  pl.BlockSpec((1, t, LANES), lambda i, h, j: (i, 0, kb + h)),
                  pl.BlockSpec((1, t, LANES), lambda i, h, j: (i, 0, vb + h)),
                  pl.BlockSpec((1, 1, 1, past, LANES), lambda i, h, j: (i, layer, h, 0, 0)),
                  pl.BlockSpec((1, 1, 1, past, DF_V), lambda i, h, j: (i, layer, h, 0, 0)),
                  pl.BlockSpec((t, LANES), lambda i, h, j: (0, 0)),
                  pl.BlockSpec((t, LANES), lambda i, h, j: (0, 0)),
                  pl.BlockSpec((4, DF_QK), lambda i, h, j: (0, 0)),
                  pl.BlockSpec((1, DF_V), lambda i, h, j: (0, 0))],
        out_specs=pl.BlockSpec((1, TQ_DF, DF_V), lambda i, h, j: (i, j, h)),
        out_shape=jax.ShapeDtypeStruct((b, t, DF_WIDTH), BF16),
        scratch_shapes=[pltpu.VMEM((t + past, LANES), BF16), pltpu.VMEM((t + past, DF_V), BF16)],
        compiler_params=_cp(("parallel", "parallel", "arbitrary"), 48),
        name="lat_df",
    )(proj3, proj3, proj3, cache_kp, cache_v, cos_t, sin_t, lam_p, subln_g.reshape(1, DF_V))


def _sg_kernel(u_ref, v_ref, g_ref, b_ref, ws_ref, bs_ref, o_ref, *, nch):
    for c in range(nch):
        rs = slice(c * SG_CHUNK, (c + 1) * SG_CHUNK)
        vn = _layernorm(v_ref[0, rs, :], g_ref[...], b_ref[...]).astype(BF16)
        for g in range(SG_GROUPS):
            cs = slice(g * SG_DIM, (g + 1) * SG_DIM)
            mixed = _dot(ws_ref[g], vn[:, cs]) + bs_ref[g]
            o_ref[0, rs, cs] = (u_ref[0, rs, cs] * mixed).astype(BF16)


def _spatial_gate(proj3, ln_g, ln_b, ws, bs):
    b, t, _ = proj3.shape
    nch = min(4, t // SG_CHUNK)
    tt = nch * SG_CHUNK
    ub, vb = OFF_U // SG_WIDTH, OFF_V // SG_WIDTH
    return pl.pallas_call(
        functools.partial(_sg_kernel, nch=nch),
        grid=(b, t // tt),
        in_specs=[pl.BlockSpec((1, tt, SG_WIDTH), lambda i, j: (i, j, ub)),
                  pl.BlockSpec((1, tt, SG_WIDTH), lambda i, j: (i, j, vb)),
                  pl.BlockSpec((1, SG_WIDTH), lambda i, j: (0, 0)),
                  pl.BlockSpec((1, SG_WIDTH), lambda i, j: (0, 0)),
                  pl.BlockSpec((SG_GROUPS, SG_CHUNK, SG_CHUNK), lambda i, j: (0, 0, 0)),
                  pl.BlockSpec((SG_GROUPS, SG_CHUNK, 1), lambda i, j: (0, 0, 0))],
        out_specs=pl.BlockSpec((1, tt, SG_WIDTH), lambda i, j: (i, j, 0)),
        out_shape=jax.ShapeDtypeStruct((b, t, SG_WIDTH), BF16),
        compiler_params=_cp(("parallel", "parallel"), 16),
        name="spatial_gate",
    )(proj3, proj3, ln_g.reshape(1, SG_WIDTH), ln_b.reshape(1, SG_WIDTH), ws, bs)


def _out_proj_kernel(oa_ref, ob_ref, oc_ref, x_ref, mod_ref, w_ref, lg_ref, lb_ref, x1_ref, h2t_ref):
    y = (_dot(oa_ref[...], w_ref[0:NA_WIDTH, :])
         + _dot(ob_ref[...], w_ref[NA_WIDTH:NA_WIDTH + SG_WIDTH, :])
         + _dot(oc_ref[...], w_ref[NA_WIDTH + SG_WIDTH:, :]))
    g1 = mod_ref[0, 2:3, :]
    sh2 = mod_ref[0, 3:4, :]
    sc2 = mod_ref[0, 4:5, :]
    x1 = _layernorm(DN_ALPHA * x_ref[...] + g1 * y, lg_ref[...], lb_ref[...])
    x1_ref[...] = x1
    h2t_ref[...] = (x1 * (1 + sc2) + sh2).T.astype(BF16)


def _out_proj(oa, ob, oc, x, mod, w, ln_g, ln_b, tiles_per_mod):
    n, d = x.shape
    return pl.pallas_call(
        _out_proj_kernel,
        grid=(n // TM_OUT,),
        in_specs=[pl.BlockSpec((TM_OUT, NA_WIDTH), lambda i: (i, 0)),
                  pl.BlockSpec((TM_OUT, SG_WIDTH), lambda i: (i, 0)),
                  pl.BlockSpec((TM_OUT, DF_WIDTH), lambda i: (i, 0)),
                  pl.BlockSpec((TM_OUT, d), lambda i: (i, 0)),
                  pl.BlockSpec((1, 6, d), lambda i: (i // tiles_per_mod, 0, 0)),
                  pl.BlockSpec((MIX_WIDTH, d), lambda i: (0, 0)),
                  pl.BlockSpec((1, d), lambda i: (0, 0)),
                  pl.BlockSpec((1, d), lambda i: (0, 0))],
        out_specs=[pl.BlockSpec((TM_OUT, d), lambda i: (i, 0)),
                   pl.BlockSpec((d, TM_OUT), lambda i: (0, i))],
        out_shape=[jax.ShapeDtypeStruct((n, d), F32),
                   jax.ShapeDtypeStruct((d, n), BF16)],
        compiler_params=_cp(("parallel",), 48),
        name="out_proj",
    )(oa, ob, oc, x, mod, w, ln_g.reshape(1, d), ln_b.reshape(1, d))


_PK_PAIRS = [(a, b) for a in range(PK_TOPK) for b in range(PK_TOPK) if (a + 1) * (b + 1) <= PK_TOPK]


def _peer_score_kernel(h2t_ref, wqt_ref, keys_ref, s2_out, e2_out, tau_out, q1_out,
                       qt_ref, s_ref, tt_ref, thr_ref, z_ref):
    tm = h2t_ref.shape[1]
    qt_ref[...] = _dot(wqt_ref[...], h2t_ref[...]).astype(BF16)
    iota_k = lax.broadcasted_iota(jnp.int32, (PK_NKEYS, tm), 0)
    neg_inf = -jnp.inf

    for c in range(2 * PK_HEADS):
        s = _dot(keys_ref[c], qt_ref[c * PK_NKEYS:(c + 1) * PK_NKEYS, :])
        s_ref[c] = s
        head, half = c // 2, c % 2

        def extract(k, s, head=head, half=half):
            m = jnp.max(s, axis=0, keepdims=True)
            first = jnp.min(jnp.where(s == m, iota_k, PK_NKEYS), axis=0, keepdims=True)
            tt_ref[half, k, head:head + 1, :] = m
            return jnp.where(iota_k == first, neg_inf, s)

        lax.fori_loop(0, PK_TOPK, extract, s)

    def per_lane_chunk(ch, carry):
        l0 = pl.multiple_of(ch * LANES, LANES)
        t1 = [tt_ref[0, a, :, pl.ds(l0, LANES)] for a in range(PK_TOPK)]
        t2 = [tt_ref[1, b, :, pl.ds(l0, LANES)] for b in range(PK_TOPK)]
        cands = [t1[a] + t2[b] for a, b in _PK_PAIRS]
        cur = list(cands)
        for _ in range(PK_TOPK - 1):
            m = functools.reduce(jnp.maximum, cur)
            found = jnp.zeros(m.shape, jnp.bool_)
            nxt = []
            for cnd in cur:
                is_m = cnd == m
                nxt.append(jnp.where(jnp.logical_and(is_m, jnp.logical_not(found)), neg_inf, cnd))
                found = jnp.logical_or(found, is_m)
            cur = nxt
        thr = functools.reduce(jnp.maximum, cur)
        e1 = [jnp.exp(t1[a] - t1[0]) for a in range(PK_TOPK)]
        e2 = [jnp.exp(t2[b] - t2[0]) for b in range(PK_TOPK)]
        z = jnp.zeros(thr.shape, F32)
        for (a, b), cnd in zip(_PK_PAIRS, cands):
            z = z + jnp.where(cnd >= thr, e1[a] * e2[b], 0.0)
        thr_ref[:, pl.ds(l0, LANES)] = thr
        z_ref[:, pl.ds(l0, LANES)] = z
        return carry

    lax.fori_loop(0, tm // LANES, per_lane_chunk, 0)

    for h in range(PK_HEADS):
        s1 = s_ref[2 * h]
        s2 = s_ref[2 * h + 1]
        thr = thr_ref[h:h + 1, :]
        tau = jnp.full(s1.shape, jnp.inf, F32)
        for b in range(PK_TOPK):
            t2b = tt_ref[1, b, h:h + 1, :]
            tau = jnp.minimum(tau, jnp.where(s1 + t2b >= thr, t2b, jnp.inf))
        tau_out[h] = tau
        q1_out[h] = jnp.exp(s1 - tt_ref[0, 0, h:h + 1, :]) * (1.0 / z_ref[h:h + 1, :])
        e2_out[h] = jnp.exp(s2 - tt_ref[1, 0, h:h + 1, :])
        s2_out[h] = s2


def _peer_score(h2t, wqt, keys):
    d, n = h2t.shape
    spec = pl.BlockSpec((PK_HEADS, PK_NKEYS, TM_SCORE), lambda i: (0, 0, i))
    shape = jax.ShapeDtypeStruct((PK_HEADS, PK_NKEYS, n), F32)
    return pl.pallas_call(
        _peer_score_kernel,
        grid=(n // TM_SCORE,),
        in_specs=[pl.BlockSpec((d, TM_SCORE), lambda i: (0, i)),
                  pl.BlockSpec(wqt.shape, lambda i: (0, 0)),
                  pl.BlockSpec(keys.shape, lambda i: (0, 0, 0))],
        out_specs=[spec, spec, spec, spec],
        out_shape=[shape, shape, shape, shape],
        scratch_shapes=[pltpu.VMEM((PK_HEADS * PK_QDIM, TM_SCORE), BF16),
                        pltpu.VMEM((2 * PK_HEADS, PK_NKEYS, TM_SCORE), F32),
                        pltpu.VMEM((2, PK_TOPK, PK_HEADS, TM_SCORE), F32),
                        pltpu.VMEM((PK_HEADS, TM_SCORE), F32),
                        pltpu.VMEM((PK_HEADS, TM_SCORE), F32)],
        compiler_params=_cp(("parallel",), 48),
        name="peer_score",
    )(h2t, wqt, keys)


def _peer_dense_kernel(h2t_ref, u_ref, vt_ref, s2_ref, e2_ref, tau_ref, q1_ref, o_ref,
                       acc_ref, a_ref, w_ref):
    e = pl.program_id(1)
    tm = h2t_ref.shape[1]
    ni = TE_PEER // PK_NKEYS
    nj = PK_NKEYS // RJ_PEER
    sqrt_half = math.sqrt(0.5)

    @pl.when(e == 0)
    def _():
        acc_ref[...] = jnp.zeros_like(acc_ref)

    a_ref[...] = _dot(u_ref[...], h2t_ref[...])

    def chunk(t, carry):
        ii = t // nj
        jc = t % nj
        i = e * ni + ii
        j0 = pl.multiple_of(jc * RJ_PEER, RJ_PEER)
        row0 = pl.multiple_of(ii * PK_NKEYS + jc * RJ_PEER, RJ_PEER)
        g = jnp.zeros((RJ_PEER, tm), F32)
        for h in range(PK_HEADS):
            tau = tau_ref[h, pl.ds(i, 1), :]
            q1 = q1_ref[h, pl.ds(i, 1), :]
            s2 = s2_ref[h, pl.ds(j0, RJ_PEER), :]
            e2 = e2_ref[h, pl.ds(j0, RJ_PEER), :]
            g = g + jnp.where(s2 >= tau, e2, 0.0) * q1
        a = a_ref[pl.ds(row0, RJ_PEER), :]
        act = 0.5 * a * (1.0 + lax.erf(a * sqrt_half))
        w_ref[pl.ds(row0, RJ_PEER), :] = (g * act).astype(BF16)
        return carry

    lax.fori_loop(0, ni * nj, chunk, 0)
    acc_ref[...] += _dot(vt_ref[...], w_ref[...])

    @pl.when(e == pl.num_programs(1) - 1)
    def _():
        o_ref[...] = acc_ref[...]


def _peer_dense(h2t, u_tab, vt_tab, s2, e2, tau, q1):
    d, n = h2t.shape
    ne = u_tab.shape[0]
    gspec = pl.BlockSpec((PK_HEADS, PK_NKEYS, TM_PEER), lambda i, e: (0, 0, i))
    return pl.pallas_call(
        _peer_dense_kernel,
        grid=(n // TM_PEER, ne // TE_PEER),
        in_specs=[pl.BlockSpec((d, TM_PEER), lambda i, e: (0, i)),
                  pl.BlockSpec((TE_PEER, d), lambda i, e: (e, 0)),
                  pl.BlockSpec((d, TE_PEER), lambda i, e: (0, e)),
                  gspec, gspec, gspec, gspec],
        out_specs=pl.BlockSpec((d, TM_PEER), lambda i, e: (0, i)),
        out_shape=jax.ShapeDtypeStruct((d, n), F32),
        scratch_shapes=[pltpu.VMEM((d, TM_PEER), F32),
                        pltpu.VMEM((TE_PEER, TM_PEER), F32),
                        pltpu.VMEM((TE_PEER, TM_PEER), BF16)],
        compiler_params=_cp(("parallel", "arbitrary"), 56),
        name="peer_dense",
    )(h2t, u_tab, vt_tab, s2, e2, tau, q1)


def _peer_ln_kernel(yt_ref, x1_ref, mod_ref, lg_ref, lb_ref, o_ref):
    g2 = mod_ref[0, 5:6, :]
    o_ref[...] = _layernorm(DN_ALPHA * x1_ref[...] + g2 * yt_ref[...].T, lg_ref[...], lb_ref[...])


def _peer_ln(yt, x1, mod, ln_g, ln_b, tiles_per_mod):
    n, d = x1.shape
    return pl.pallas_call(
        _peer_ln_kernel,
        grid=(n // TM_OUT,),
        in_specs=[pl.BlockSpec((d, TM_OUT), lambda i: (0, i)),
                  pl.BlockSpec((TM_OUT, d), lambda i: (i, 0)),
                  pl.BlockSpec((1, 6, d), lambda i: (i // tiles_per_mod, 0, 0)),
                  pl.BlockSpec((1, d), lambda i: (0, 0)),
                  pl.BlockSpec((1, d), lambda i: (0, 0))],
        out_specs=pl.BlockSpec((TM_OUT, d), lambda i: (i, 0)),
        out_shape=jax.ShapeDtypeStruct((n, d), F32),
        compiler_params=_cp(("parallel",), 40),
        name="peer_ln",
    )(yt, x1, mod, ln_g.reshape(1, d), ln_b.reshape(1, d))


def _in_proj_column_order():
    cols = list(range(OFF_DQ))
    half = DF_QK // 2
    nq = DF_HEADS * DF_QK
    for base in (OFF_DQ, OFF_DQ + 2 * nq):
        for h in range(DF_HEADS):
            for part in range(2):
                for which in range(2):
                    start = base + which * nq + h * DF_QK + part * half
                    cols.extend(range(start, start + half))
    cols.extend(range(OFF_DV, IN_WIDTH))
    return np.asarray(cols, np.int32)


def _pair_lanes(a, b):
    half = DF_QK // 2
    return jnp.concatenate([a[..., :half], b[..., :half], a[..., half:], b[..., half:]], -1)


def _unpair_lanes(kk):
    half = DF_QK // 2
    a = jnp.concatenate([kk[..., 0:half], kk[..., 2 * half:3 * half]], -1)
    b = jnp.concatenate([kk[..., half:2 * half], kk[..., 3 * half:]], -1)
    return a, b


def _rope_tables(t):
    tok = jnp.arange(t)
    row = (tok // GRID_W).astype(F32)
    col = (tok % GRID_W).astype(F32)
    n_freq = DF_QK // 4
    inv = 1.0 / (ROPE_THETA ** (jnp.arange(n_freq, dtype=F32) / n_freq))
    ang = jnp.concatenate([row[:, None] * inv, col[:, None] * inv], -1)
    cos, sin = jnp.cos(ang), jnp.sin(ang)
    return jnp.concatenate([cos] * 4, -1), jnp.concatenate([-sin, -sin, sin, sin], -1)


def _na_bias_table(rpb):
    cols = np.arange(GRID_W)
    start = np.clip(cols - NA_WIN_C // 2, 0, GRID_W - NA_WIN_C)
    inwin = (cols[None, :] >= start[:, None]) & (cols[None, :] < start[:, None] + NA_WIN_C)
    dc = np.clip(cols[None, :] - cols[:, None] + NA_WIN_C - 1, 0, 2 * NA_WIN_C - 2)
    dr = np.arange(NA_WIN_R)[None, :] - np.arange(NA_WIN_R)[:, None] + NA_WIN_R - 1
    tab = rpb[:, dr[:, None, :, None], dc[None, :, None, :]]
    tab = jnp.where(inwin[None, None, :, None, :], tab, NEG_BIG)
    return tab.reshape(rpb.shape[0], NA_WIN_R, GRID_W, NA_WIN_R * GRID_W)


def _heads_out(a, b, t, h):
    return a.reshape(b, t, h, -1).transpose(0, 2, 1, 3)


def _peer_block(oa, ob, oc, x, mod, tiles_out, w_out, ln_g, ln_b, wqt, keys, u_tab, vt_tab):
    x1, h2t = _out_proj(oa, ob, oc, x, mod, w_out, ln_g[0], ln_b[0], tiles_out)
    s2, e2, tau, q1 = _peer_score(h2t, wqt, keys)
    yt = _peer_dense(h2t, u_tab, vt_tab, s2, e2, tau, q1)
    return _peer_ln(yt, x1, mod, ln_g[1], ln_b[1], tiles_out)


def kernel(x_prompt, x_sample, cache_na_k, cache_na_v, cache_df_k1, cache_df_k2, cache_df_v, c, c_ctx,
           w_mod, b_mod, w_in, na_rpb, sg_ln_g, sg_ln_b, sg_w, sg_b, df_lambda, df_subln_g, w_out,
           pk_wq, pk_keys, pk_u, pk_v, ln_g, ln_b):
    bp, tp, d = x_prompt.shape
    bs, ts, _ = x_sample.shape
    n_p, n_s = bp * tp, bs * ts

    cond8 = jnp.zeros((8, d), F32).at[0].set(c_ctx).at[1:1 + bs].set(c)
    mods = _modulation(cond8, w_mod, b_mod).reshape(DEPTH, 8, 6, d)

    col_order = _in_proj_column_order()
    cos_t, sin_t = _rope_tables(ts)
    cache_dk = _pair_lanes(cache_df_k1, cache_df_k2)

    xp = x_prompt.reshape(n_p, d)
    xs = x_sample.reshape(n_s, d)
    st = [[] for _ in range(5)]
    for l in range(DEPTH):
        lam_init = 0.8 - 0.6 * math.exp(-0.3 * l)
        w_in_l = w_in[l][:, col_order].astype(BF16)
        w_out_l = w_out[l].astype(BF16)
        wqt = pk_wq[l].T.astype(BF16)
        keys = pk_keys[l].reshape(2 * PK_HEADS, PK_NKEYS, PK_QDIM // 2).astype(BF16)
        u_tab = pk_u[l].astype(BF16)
        vt_tab = pk_v[l].T.astype(BF16)
        ws = sg_w[l].astype(BF16)
        bs_col = sg_b[l].reshape(SG_GROUPS, SG_CHUNK, 1)
        bias = _na_bias_table(na_rpb[l])
        mod_p = mods[l, 0:1]
        mod_s = mods[l, 1:1 + bs]

        proj = _in_proj(xp, mod_p, w_in_l, n_p // TM_IN)
        proj3 = proj.reshape(bp, tp, IN_WIDTH)
        oa, oc = _ctx_attn(proj3, df_lambda[l], df_subln_g[l], lam_init)
        ob = _spatial_gate(proj3, sg_ln_g[l], sg_ln_b[l], ws, bs_col)
        st[0].append(_heads_out(proj[:, OFF_KA:OFF_KA + NA_WIDTH], bp, tp, NA_HEADS))
        st[1].append(_heads_out(proj[:, OFF_VA:OFF_VA + NA_WIDTH], bp, tp, NA_HEADS))
        k1, k2 = _unpair_lanes(_heads_out(proj[:, OFF_DK:OFF_DV], bp, tp, DF_HEADS))
        st[2].append(k1)
        st[3].append(k2)
        st[4].append(_heads_out(proj[:, OFF_DV:], bp, tp, DF_HEADS))
        xp = _peer_block(oa.reshape(n_p, NA_WIDTH), ob.reshape(n_p, SG_WIDTH), oc.reshape(n_p, DF_WIDTH),
                         xp, mod_p, n_p // TM_OUT, w_out_l, ln_g[l], ln_b[l], wqt, keys, u_tab, vt_tab)

        proj = _in_proj(xs, mod_s, w_in_l, ts // TM_IN)
        proj3 = proj.reshape(bs, ts, IN_WIDTH)
        oa = _lat_na(proj3, cache_na_k, cache_na_v, bias, l)
        ob = _spatial_gate(proj3, sg_ln_g[l], sg_ln_b[l], ws, bs_col)
        oc = _lat_df(proj3, cache_dk, cache_df_v, cos_t, sin_t, df_lambda[l], df_subln_g[l], lam_init, l)
        xs = _peer_block(oa.reshape(n_s, NA_WIDTH), ob.reshape(n_s, SG_WIDTH), oc.reshape(n_s, DF_WIDTH),
                         xs, mod_s, ts // TM_OUT, w_out_l, ln_g[l], ln_b[l], wqt, keys, u_tab, vt_tab)

    outs = [jnp.stack(s, 1) for s in st]
    return (xp.reshape(bp, tp, d), xs.reshape(bs, ts, d), *outs)
```

```python
import functools
import math

import numpy as np
import jax
import jax.numpy as jnp
from jax import lax
from jax.experimental import pallas as pl
from jax.experimental.pallas import tpu as pltpu

F32 = jnp.float32
BF16 = jnp.bfloat16

D_MODEL = 2048
DEPTH = 4
GRID_W = 64
NA_HEADS = 8
NA_DIM = 128
NA_WIN_R = 8
NA_WIN_C = 16
SG_GROUPS = 4
SG_DIM = 128
SG_CHUNK = 128
DF_HEADS = 4
DF_QK = 64
DF_V = 128
ROPE_THETA = 10000.0
NA_WIDTH = NA_HEADS * NA_DIM
SG_WIDTH = SG_GROUPS * SG_DIM
DF_WIDTH = DF_HEADS * DF_V
MIX_WIDTH = NA_WIDTH + SG_WIDTH + DF_WIDTH
IN_WIDTH = 3 * NA_WIDTH + 2 * SG_WIDTH + 4 * DF_HEADS * DF_QK + DF_WIDTH
PK_HEADS = 8
PK_QDIM = 256
PK_NKEYS = 128
PK_TOPK = 16
PK_EXPERTS = PK_NKEYS * PK_NKEYS
DN_ALPHA = (2 * DEPTH) ** 0.25
LN_EPS = 1e-5
RMS_EPS = 1e-6

OFF_QA = 0
OFF_KA = NA_WIDTH
OFF_VA = 2 * NA_WIDTH
OFF_U = 3 * NA_WIDTH
OFF_V = OFF_U + SG_WIDTH
OFF_DQ = OFF_V + SG_WIDTH
OFF_DK = OFF_DQ + 2 * DF_HEADS * DF_QK
OFF_DV = OFF_DK + 2 * DF_HEADS * DF_QK

LANES = 128
MIB = 1024 * 1024

TM_IN = 512
TN_IN = IN_WIDTH // 4
TM_OUT = 512
TM_SCORE = 256
TM_PEER = 512
TE_PEER = 512
RJ_PEER = 16
TQ_DF = 256
TN_MOD = 1024
NEG_BIG = -1e30


def _cp(sem, vmem_mib):
    return pltpu.CompilerParams(dimension_semantics=sem, vmem_limit_bytes=vmem_mib * MIB)


def _dot(a, b):
    return jnp.dot(a, b, preferred_element_type=F32)


def _dot_nt(a, b):
    return lax.dot_general(a, b, (((1,), (1,)), ((), ())), preferred_element_type=F32)


def _layernorm(z, g, b):
    mu = jnp.mean(z, -1, keepdims=True)
    d = z - mu
    var = jnp.mean(d * d, -1, keepdims=True)
    return d * lax.rsqrt(var + LN_EPS) * g + b


def _softmax_rows(s):
    m = jnp.max(s, -1, keepdims=True)
    e = jnp.exp(s - m)
    return e * (1.0 / jnp.sum(e, -1, keepdims=True))


def _diff_lambda(lam_ref, lam_init):
    lf = lam_ref[...]
    a = jnp.sum(lf[0:1, :] * lf[1:2, :], axis=1, keepdims=True)
    b = jnp.sum(lf[2:3, :] * lf[3:4, :], axis=1, keepdims=True)
    return jnp.exp(a) - jnp.exp(b) + lam_init


def _map1_mask():
    lane = lax.broadcasted_iota(jnp.int32, (1, LANES), 1)
    return (lane // (DF_QK // 2)) % 2 == 0


def _subln(o, g, lam_init):
    return o * lax.rsqrt(jnp.mean(o * o, -1, keepdims=True) + RMS_EPS) * g * (1.0 - lam_init)


def _mod_kernel(c_ref, w_ref, b_ref, o_ref):
    c = c_ref[...]
    a = (c * jax.nn.sigmoid(c)).astype(BF16)
    o_ref[0] = _dot(a, w_ref[0].astype(BF16)) + b_ref[0]


def _modulation(cond8, w_mod, b_mod):
    depth, d, n6 = w_mod.shape
    return pl.pallas_call(
        _mod_kernel,
        grid=(depth, n6 // TN_MOD),
        in_specs=[pl.BlockSpec((8, d), lambda l, j: (0, 0)),
                  pl.BlockSpec((1, d, TN_MOD), lambda l, j: (l, 0, j)),
                  pl.BlockSpec((1, 1, TN_MOD), lambda l, j: (l, 0, j))],
        out_specs=pl.BlockSpec((1, 8, TN_MOD), lambda l, j: (l, 0, j)),
        out_shape=jax.ShapeDtypeStruct((depth, 8, n6), F32),
        compiler_params=_cp(("parallel", "parallel"), 32),
        name="modulation",
    )(cond8, w_mod, b_mod.reshape(depth, 1, n6))


def _in_proj_kernel(x_ref, mod_ref, w_ref, o_ref, h_ref):
    @pl.when(pl.program_id(1) == 0)
    def _():
        sh = mod_ref[0, 0:1, :]
        sc = mod_ref[0, 1:2, :]
        h_ref[...] = (x_ref[...] * (1 + sc) + sh).astype(BF16)

    o_ref[...] = _dot(h_ref[...], w_ref[...])


def _in_proj(x, mod, w, tiles_per_mod):
    n, d = x.shape
    nw = w.shape[1]
    return pl.pallas_call(
        _in_proj_kernel,
        grid=(n // TM_IN, nw // TN_IN),
        in_specs=[pl.BlockSpec((TM_IN, d), lambda i, j: (i, 0)),
                  pl.BlockSpec((1, 6, d), lambda i, j: (i // tiles_per_mod, 0, 0)),
                  pl.BlockSpec((d, TN_IN), lambda i, j: (0, j))],
        out_specs=pl.BlockSpec((TM_IN, TN_IN), lambda i, j: (i, j)),
        out_shape=jax.ShapeDtypeStruct((n, nw), F32),
        scratch_shapes=[pltpu.VMEM((TM_IN, d), BF16)],
        compiler_params=_cp(("parallel", "arbitrary"), 40),
        name="in_proj",
    )(x, mod, w)


def _ctx_attn_kernel(p_ref, lam_ref, g_ref, oa_ref, oc_ref, *, lam_init):
    scale = NA_DIM ** -0.5
    for h in range(NA_HEADS):
        sl = slice(h * NA_DIM, (h + 1) * NA_DIM)
        q = p_ref[0, :, OFF_QA + h * NA_DIM:OFF_QA + (h + 1) * NA_DIM].astype(BF16)
        k = p_ref[0, :, OFF_KA + h * NA_DIM:OFF_KA + (h + 1) * NA_DIM].astype(BF16)
        v = p_ref[0, :, OFF_VA + h * NA_DIM:OFF_VA + (h + 1) * NA_DIM].astype(BF16)
        p = _softmax_rows(_dot_nt(q, k) * scale)
        oa_ref[0, :, sl] = _dot(p.astype(BF16), v).astype(BF16)
    lam = _diff_lambda(lam_ref, lam_init)
    m1 = _map1_mask()
    dscale = DF_QK ** -0.5
    for h in range(DF_HEADS):
        sl = slice(h * DF_V, (h + 1) * DF_V)
        q = p_ref[0, :, OFF_DQ + h * LANES:OFF_DQ + (h + 1) * LANES]
        k = p_ref[0, :, OFF_DK + h * LANES:OFF_DK + (h + 1) * LANES].astype(BF16)
        v = p_ref[0, :, OFF_DV + h * DF_V:OFF_DV + (h + 1) * DF_V].astype(BF16)
        qa = jnp.where(m1, q, 0.0).astype(BF16)
        qb = jnp.where(m1, 0.0, q).astype(BF16)
        p = _softmax_rows(_dot_nt(qa, k) * dscale) - lam * _softmax_rows(_dot_nt(qb, k) * dscale)
        o = _dot(p.astype(BF16), v)
        oc_ref[0, :, sl] = _subln(o, g_ref[...], lam_init).astype(BF16)


def _ctx_attn(proj3, lam_p, subln_g, lam_init):
    b, t, nw = proj3.shape
    return pl.pallas_call(
        functools.partial(_ctx_attn_kernel, lam_init=lam_init),
        grid=(b,),
        in_specs=[pl.BlockSpec((1, t, nw), lambda i: (i, 0, 0)),
                  pl.BlockSpec((4, DF_QK), lambda i: (0, 0)),
                  pl.BlockSpec((1, DF_V), lambda i: (0, 0))],
        out_specs=[pl.BlockSpec((1, t, NA_WIDTH), lambda i: (i, 0, 0)),
                   pl.BlockSpec((1, t, DF_WIDTH), lambda i: (i, 0, 0))],
        out_shape=[jax.ShapeDtypeStruct((b, t, NA_WIDTH), BF16),
                   jax.ShapeDtypeStruct((b, t, DF_WIDTH), BF16)],
        compiler_params=_cp(("parallel",), 32),
        name="ctx_attn",
    )(proj3, lam_p, subln_g.reshape(1, DF_V))


def _lat_na_kernel(q_ref, k_ref, v_ref, kc_ref, vc_ref, bias_ref, o_ref, kb_ref, vb_ref, *, rows):
    kb_ref[...] = k_ref[0].astype(BF16)
    vb_ref[...] = v_ref[0].astype(BF16)
    kc = kc_ref[0, 0, 0].astype(BF16)
    vc = vc_ref[0, 0, 0].astype(BF16)
    scale = NA_DIM ** -0.5
    band = NA_WIN_R * GRID_W

    def body(r, carry):
        r0 = jnp.clip(r - NA_WIN_R // 2, 0, rows - NA_WIN_R)
        q = q_ref[0, pl.ds(pl.multiple_of(r * GRID_W, GRID_W), GRID_W), :].astype(BF16)
        k0 = pl.multiple_of(r0 * GRID_W, GRID_W)
        kw = kb_ref[pl.ds(k0, band), :]
        vw = vb_ref[pl.ds(k0, band), :]
        s_loc = _dot_nt(q, kw) * scale + bias_ref[0, r - r0]
        s_ctx = _dot_nt(q, kc) * scale
        m = jnp.maximum(jnp.max(s_loc, -1, keepdims=True), jnp.max(s_ctx, -1, keepdims=True))
        e_loc = jnp.exp(s_loc - m)
        e_ctx = jnp.exp(s_ctx - m)
        inv = 1.0 / (jnp.sum(e_loc, -1, keepdims=True) + jnp.sum(e_ctx, -1, keepdims=True))
        o = _dot((e_loc * inv).astype(BF16), vw) + _dot((e_ctx * inv).astype(BF16), vc)
        o_ref[0, pl.ds(pl.multiple_of(r * GRID_W, GRID_W), GRID_W), :] = o.astype(BF16)
        return carry

    lax.fori_loop(0, rows, body, 0)


def _lat_na(proj3, cache_k, cache_v, bias, layer):
    b, t, _ = proj3.shape
    past = cache_k.shape[3]
    rows = t // GRID_W
    qb, kb, vb = OFF_QA // NA_DIM, OFF_KA // NA_DIM, OFF_VA // NA_DIM
    return pl.pallas_call(
        functools.partial(_lat_na_kernel, rows=rows),
        grid=(b, NA_HEADS),
        in_specs=[pl.BlockSpec((1, t, NA_DIM), lambda i, h: (i, 0, qb + h)),
                  pl.BlockSpec((1, t, NA_DIM), lambda i, h: (i, 0, kb + h)),
                  pl.BlockSpec((1, t, NA_DIM), lambda i, h: (i, 0, vb + h)),
                  pl.BlockSpec((1, 1, 1, past, NA_DIM), lambda i, h: (i, layer, h, 0, 0)),
                  pl.BlockSpec((1, 1, 1, past, NA_DIM), lambda i, h: (i, layer, h, 0, 0)),
                  pl.BlockSpec((1, NA_WIN_R, GRID_W, NA_WIN_R * GRID_W), lambda i, h: (h, 0, 0, 0))],
        out_specs=pl.BlockSpec((1, t, NA_DIM), lambda i, h: (i, 0, h)),
        out_shape=jax.ShapeDtypeStruct((b, t, NA_WIDTH), BF16),
        scratch_shapes=[pltpu.VMEM((t, NA_DIM), BF16), pltpu.VMEM((t, NA_DIM), BF16)],
        compiler_params=_cp(("parallel", "parallel"), 40),
        name="lat_na",
    )(proj3, proj3, proj3, cache_k, cache_v, bias)


def _lat_df_kernel(q_ref, k_ref, v_ref, kc_ref, vc_ref, cos_ref, sin_ref, lam_ref, g_ref, o_ref,
                   kall_ref, vall_ref, *, t, lam_init):
    qi = pl.program_id(2)

    @pl.when(qi == 0)
    def _():
        k = k_ref[0]
        kall_ref[0:t, :] = (k * cos_ref[...] + pltpu.roll(k, LANES // 2, 1) * sin_ref[...]).astype(BF16)
        kall_ref[t:, :] = kc_ref[0, 0, 0].astype(BF16)
        vall_ref[0:t, :] = v_ref[0].astype(BF16)
        vall_ref[t:, :] = vc_ref[0, 0, 0].astype(BF16)

    t0 = pl.multiple_of(qi * TQ_DF, TQ_DF)
    q = q_ref[0]
    q = q * cos_ref[pl.ds(t0, TQ_DF), :] + pltpu.roll(q, LANES // 2, 1) * sin_ref[pl.ds(t0, TQ_DF), :]
    m1 = _map1_mask()
    qa = jnp.where(m1, q, 0.0).astype(BF16)
    qb = jnp.where(m1, 0.0, q).astype(BF16)
    dscale = DF_QK ** -0.5
    lam = _diff_lambda(lam_ref, lam_init)
    kall = kall_ref[...]
    p = _softmax_rows(_dot_nt(qa, kall) * dscale) - lam * _softmax_rows(_dot_nt(qb, kall) * dscale)
    o = _dot(p.astype(BF16), vall_ref[...])
    o_ref[0] = _subln(o, g_ref[...], lam_init).astype(BF16)


def _lat_df(proj3, cache_kp, cache_v, cos_t, sin_t, lam_p, subln_g, lam_init, layer):
    b, t, _ = proj3.shape
    past = cache_v.shape[3]
    qb, kb, vb = OFF_DQ // LANES, OFF_DK // LANES, OFF_DV // LANES
    return pl.pallas_call(
        functools.partial(_lat_df_kernel, t=t, lam_init=lam_init),
        grid=(b, DF_HEADS, t // TQ_DF),
        in_specs=[pl.BlockSpec((1, TQ_DF, LANES), lambda i, h, j: (i, j, qb + h)),
                  pl.BlockSpec((1, t, LANES), lambda i, h, j: (i, 0, kb + h)),
                  pl.BlockSpec((1, t, LANES), lambda i, h, j: (i, 0, vb + h)),
                  pl.BlockSpec((1, 1, 1, past, LANES), lambda i, h, j: (i, layer, h, 0, 0)),
                  pl.BlockSpec((1, 1, 1, past, DF_V), lambda i, h, j: (i, layer, h, 0, 0)),
                  pl.BlockSpec((t, LANES), lambda i, h, j: (0, 0)),
                  pl.BlockSpec((t, LANES), lambda i, h, j: (0, 0)),
                  pl.BlockSpec((4, DF_QK), lambda i, h, j: (0, 0)),
                  pl.BlockSpec((1, DF_V), lambda i, h, j: (0, 0))],
        out_specs=pl.BlockSpec((1, TQ_DF, DF_V), lambda i, h, j: (i, j, h)),
        out_shape=jax.ShapeDtypeStruct((b, t, DF_WIDTH), BF16),
        scratch_shapes=[pltpu.VMEM((t + past, LANES), BF16), pltpu.VMEM((t + past, DF_V), BF16)],
        compiler_params=_cp(("parallel", "parallel", "arbitrary"), 48),
        name="lat_df",
    )(proj3, proj3, proj3, cache_kp, cache_v, cos_t, sin_t, lam_p, subln_g.reshape(1, DF_V))


def _sg_kernel(u_ref, v_ref, g_ref, b_ref, ws_ref, bs_ref, o_ref, *, nch):
    for c in range(nch):
        rs = slice(c * SG_CHUNK, (c + 1) * SG_CHUNK)
        vn = _layernorm(v_ref[0, rs, :], g_ref[...], b_ref[...]).astype(BF16)
        for g in range(SG_GROUPS):
            cs = slice(g * SG_DIM, (g + 1) * SG_DIM)
            mixed = _dot(ws_ref[g], vn[:, cs]) + bs_ref[g]
            o_ref[0, rs, cs] = (u_ref[0, rs, cs] * mixed).astype(BF16)


def _spatial_gate(proj3, ln_g, ln_b, ws, bs):
    b, t, _ = proj3.shape
    nch = min(4, t // SG_CHUNK)
    tt = nch * SG_CHUNK
    ub, vb = OFF_U // SG_WIDTH, OFF_V // SG_WIDTH
    return pl.pallas_call(
        functools.partial(_sg_kernel, nch=nch),
        grid=(b, t // tt),
        in_specs=[pl.BlockSpec((1, tt, SG_WIDTH), lambda i, j: (i, j, ub)),
                  pl.BlockSpec((1, tt, SG_WIDTH), lambda i, j: (i, j, vb)),
                  pl.BlockSpec((1, SG_WIDTH), lambda i, j: (0, 0)),
                  pl.BlockSpec((1, SG_WIDTH), lambda i, j: (0, 0)),
                  pl.BlockSpec((SG_GROUPS, SG_CHUNK, SG_CHUNK), lambda i, j: (0, 0, 0)),
                  pl.BlockSpec((SG_GROUPS, SG_CHUNK, 1), lambda i, j: (0, 0, 0))],
        out_specs=pl.BlockSpec((1, tt, SG_WIDTH), lambda i, j: (i, j, 0)),
        out_shape=jax.ShapeDtypeStruct((b, t, SG_WIDTH), BF16),
        compiler_params=_cp(("parallel", "parallel"), 16),
        name="spatial_gate",
    )(proj3, proj3, ln_g.reshape(1, SG_WIDTH), ln_b.reshape(1, SG_WIDTH), ws, bs)


def _out_proj_kernel(oa_ref, ob_ref, oc_ref, x_ref, mod_ref, w_ref, lg_ref, lb_ref, x1_ref, h2t_ref):
    y = (_dot(oa_ref[...], w_ref[0:NA_WIDTH, :])
         + _dot(ob_ref[...], w_ref[NA_WIDTH:NA_WIDTH + SG_WIDTH, :])
         + _dot(oc_ref[...], w_ref[NA_WIDTH + SG_WIDTH:, :]))
    g1 = mod_ref[0, 2:3, :]
    sh2 = mod_ref[0, 3:4, :]
    sc2 = mod_ref[0, 4:5, :]
    x1 = _layernorm(DN_ALPHA * x_ref[...] + g1 * y, lg_ref[...], lb_ref[...])
    x1_ref[...] = x1
    h2t_ref[...] = (x1 * (1 + sc2) + sh2).T.astype(BF16)


def _out_proj(oa, ob, oc, x, mod, w, ln_g, ln_b, tiles_per_mod):
    n, d = x.shape
    return pl.pallas_call(
        _out_proj_kernel,
        grid=(n // TM_OUT,),
        in_specs=[pl.BlockSpec((TM_OUT, NA_WIDTH), lambda i: (i, 0)),
                  pl.BlockSpec((TM_OUT, SG_WIDTH), lambda i: (i, 0)),
                  pl.BlockSpec((TM_OUT, DF_WIDTH), lambda i: (i, 0)),
                  pl.BlockSpec((TM_OUT, d), lambda i: (i, 0)),
                  pl.BlockSpec((1, 6, d), lambda i: (i // tiles_per_mod, 0, 0)),
                  pl.BlockSpec((MIX_WIDTH, d), lambda i: (0, 0)),
                  pl.BlockSpec((1, d), lambda i: (0, 0)),
                  pl.BlockSpec((1, d), lambda i: (0, 0))],
        out_specs=[pl.BlockSpec((TM_OUT, d), lambda i: (i, 0)),
                   pl.BlockSpec((d, TM_OUT), lambda i: (0, i))],
        out_shape=[jax.ShapeDtypeStruct((n, d), F32),
                   jax.ShapeDtypeStruct((d, n), BF16)],
        compiler_params=_cp(("parallel",), 48),
        name="out_proj",
    )(oa, ob, oc, x, mod, w, ln_g.reshape(1, d), ln_b.reshape(1, d))


_PK_PAIRS = [(a, b) for a in range(PK_TOPK) for b in range(PK_TOPK) if (a + 1) * (b + 1) <= PK_TOPK]


def _peer_score_kernel(h2t_ref, wqt_ref, keys_ref, s2_out, e2_out, tau_out, q1_out,
                       qt_ref, s_ref, tt_ref, thr_ref, z_ref):
    tm = h2t_ref.shape[1]
    qt_ref[...] = _dot(wqt_ref[...], h2t_ref[...]).astype(BF16)
    iota_k = lax.broadcasted_iota(jnp.int32, (PK_NKEYS, tm), 0)
    neg_inf = -jnp.inf

    for c in range(2 * PK_HEADS):
        s = _dot(keys_ref[c], qt_ref[c * PK_NKEYS:(c + 1) * PK_NKEYS, :])
        s_ref[c] = s
        head, half = c // 2, c % 2

        def extract(k, s, head=head, half=half):
            m = jnp.max(s, axis=0, keepdims=True)
            first = jnp.min(jnp.where(s == m, iota_k, PK_NKEYS), axis=0, keepdims=True)
            tt_ref[half, k, head:head + 1, :] = m
            return jnp.where(iota_k == first, neg_inf, s)

        lax.fori_loop(0, PK_TOPK, extract, s)

    def per_lane_chunk(ch, carry):
        l0 = pl.multiple_of(ch * LANES, LANES)
        t1 = [tt_ref[0, a, :, pl.ds(l0, LANES)] for a in range(PK_TOPK)]
        t2 = [tt_ref[1, b, :, pl.ds(l0, LANES)] for b in range(PK_TOPK)]
        cands = [t1[a] + t2[b] for a, b in _PK_PAIRS]
        cur = list(cands)
        for _ in range(PK_TOPK - 1):
            m = functools.reduce(jnp.maximum, cur)
            found = jnp.zeros(m.shape, jnp.bool_)
            nxt = []
            for cnd in cur:
                is_m = cnd == m
                nxt.append(jnp.where(jnp.logical_and(is_m, jnp.logical_not(found)), neg_inf, cnd))
                found = jnp.logical_or(found, is_m)
            cur = nxt
        thr = functools.reduce(jnp.maximum, cur)
        e1 = [jnp.exp(t1[a] - t1[0]) for a in range(PK_TOPK)]
        e2 = [jnp.exp(t2[b] - t2[0]) for b in range(PK_TOPK)]
        z = jnp.zeros(thr.shape, F32)
        for (a, b), cnd in zip(_PK_PAIRS, cands):
            z = z + jnp.where(cnd >= thr, e1[a] * e2[b], 0.0)
        thr_ref[:, pl.ds(l0, LANES)] = thr
        z_ref[:, pl.ds(l0, LANES)] = z
        return carry

    lax.fori_loop(0, tm // LANES, per_lane_chunk, 0)

    for h in range(PK_HEADS):
        s1 = s_ref[2 * h]
        s2 = s_ref[2 * h + 1]
        thr = thr_ref[h:h + 1, :]
        tau = jnp.full(s1.shape, jnp.inf, F32)
        for b in range(PK_TOPK):
            t2b = tt_ref[1, b, h:h + 1, :]
            tau = jnp.minimum(tau, jnp.where(s1 + t2b >= thr, t2b, jnp.inf))
        tau_out[h] = tau
        q1_out[h] = jnp.exp(s1 - tt_ref[0, 0, h:h + 1, :]) * (0.5 / z_ref[h:h + 1, :])
        e2_out[h] = jnp.exp(s2 - tt_ref[1, 0, h:h + 1, :])
        s2_out[h] = s2


def _peer_score(h2t, wqt, keys):
    d, n = h2t.shape
    spec = pl.BlockSpec((PK_HEADS, PK_NKEYS, TM_SCORE), lambda i: (0, 0, i))
    shape = jax.ShapeDtypeStruct((PK_HEADS, PK_NKEYS, n), F32)
    return pl.pallas_call(
        _peer_score_kernel,
        grid=(n // TM_SCORE,),
        in_specs=[pl.BlockSpec((d, TM_SCORE), lambda i: (0, i)),
                  pl.BlockSpec(wqt.shape, lambda i: (0, 0)),
                  pl.BlockSpec(keys.shape, lambda i: (0, 0, 0))],
        out_specs=[spec, spec, spec, spec],
        out_shape=[shape, shape, shape, shape],
        scratch_shapes=[pltpu.VMEM((PK_HEADS * PK_QDIM, TM_SCORE), BF16),
                        pltpu.VMEM((2 * PK_HEADS, PK_NKEYS, TM_SCORE), F32),
                        pltpu.VMEM((2, PK_TOPK, PK_HEADS, TM_SCORE), F32),
                        pltpu.VMEM((PK_HEADS, TM_SCORE), F32),
                        pltpu.VMEM((PK_HEADS, TM_SCORE), F32)],
        compiler_params=_cp(("parallel",), 48),
        name="peer_score",
    )(h2t, wqt, keys)


def _peer_dense_kernel(h2t_ref, u_ref, vt_ref, s2_ref, e2_ref, tau_ref, q1_ref, o_ref,
                       acc_ref, a_ref, w_ref):
    e = pl.program_id(1)
    tm = h2t_ref.shape[1]
    ni = TE_PEER // PK_NKEYS
    nj = PK_NKEYS // RJ_PEER
    sqrt_half = math.sqrt(0.5)

    @pl.when(e == 0)
    def _():
        acc_ref[...] = jnp.zeros_like(acc_ref)

    a_ref[...] = _dot(u_ref[...], h2t_ref[...])

    def chunk(t, carry):
        ii = t // nj
        jc = t % nj
        i = e * ni + ii
        j0 = pl.multiple_of(jc * RJ_PEER, RJ_PEER)
        row0 = pl.multiple_of(ii * PK_NKEYS + jc * RJ_PEER, RJ_PEER)
        g = jnp.zeros((RJ_PEER, tm), F32)
        for h in range(PK_HEADS):
            tau = tau_ref[h, pl.ds(i, 1), :]
            q1 = q1_ref[h, pl.ds(i, 1), :]
            s2 = s2_ref[h, pl.ds(j0, RJ_PEER), :]
            e2 = e2_ref[h, pl.ds(j0, RJ_PEER), :]
            g = g + jnp.where(s2 >= tau, e2, 0.0) * q1
        a = a_ref[pl.ds(row0, RJ_PEER), :]
        act = a * (1.0 + lax.erf(a * sqrt_half))
        w_ref[pl.ds(row0, RJ_PEER), :] = (g * act).astype(BF16)
        return carry

    lax.fori_loop(0, ni * nj, chunk, 0)
    acc_ref[...] += _dot(vt_ref[...], w_ref[...])

    @pl.when(e == pl.num_programs(1) - 1)
    def _():
        o_ref[...] = acc_ref[...]


def _peer_dense(h2t, u_tab, vt_tab, s2, e2, tau, q1):
    d, n = h2t.shape
    ne = u_tab.shape[0]
    gspec = pl.BlockSpec((PK_HEADS, PK_NKEYS, TM_PEER), lambda i, e: (0, 0, i))
    return pl.pallas_call(
        _peer_dense_kernel,
        grid=(n // TM_PEER, ne // TE_PEER),
        in_specs=[pl.BlockSpec((d, TM_PEER), lambda i, e: (0, i)),
                  pl.BlockSpec((TE_PEER, d), lambda i, e: (e, 0)),
                  pl.BlockSpec((d, TE_PEER), lambda i, e: (0, e)),
                  gspec, gspec, gspec, gspec],
        out_specs=pl.BlockSpec((d, TM_PEER), lambda i, e: (0, i)),
        out_shape=jax.ShapeDtypeStruct((d, n), F32),
        scratch_shapes=[pltpu.VMEM((d, TM_PEER), F32),
                        pltpu.VMEM((TE_PEER, TM_PEER), F32),
                        pltpu.VMEM((TE_PEER, TM_PEER), BF16)],
        compiler_params=_cp(("parallel", "arbitrary"), 56),
        name="peer_dense",
    )(h2t, u_tab, vt_tab, s2, e2, tau, q1)


def _peer_ln_kernel(yt_ref, x1_ref, mod_ref, lg_ref, lb_ref, o_ref):
    g2 = mod_ref[0, 5:6, :]
    o_ref[...] = _layernorm(DN_ALPHA * x1_ref[...] + g2 * yt_ref[...].T, lg_ref[...], lb_ref[...])


def _peer_ln(yt, x1, mod, ln_g, ln_b, tiles_per_mod):
    n, d = x1.shape
    return pl.pallas_call(
        _peer_ln_kernel,
        grid=(n // TM_OUT,),
        in_specs=[pl.BlockSpec((d, TM_OUT), lambda i: (0, i)),
                  pl.BlockSpec((TM_OUT, d), lambda i: (i, 0)),
                  pl.BlockSpec((1, 6, d), lambda i: (i // tiles_per_mod, 0, 0)),
                  pl.BlockSpec((1, d), lambda i: (0, 0)),
                  pl.BlockSpec((1, d), lambda i: (0, 0))],
        out_specs=pl.BlockSpec((TM_OUT, d), lambda i: (i, 0)),
        out_shape=jax.ShapeDtypeStruct((n, d), F32),
        compiler_params=_cp(("parallel",), 40),
        name="peer_ln",
    )(yt, x1, mod, ln_g.reshape(1, d), ln_b.reshape(1, d))


def _permute_in_proj(w):
    d = w.shape[0]

    def interleave(cols):
        c = cols.reshape(d, 2, DF_HEADS, 2, DF_QK // 2)
        return c.transpose(0, 2, 3, 1, 4).reshape(d, 2 * DF_HEADS * DF_QK)

    return jnp.concatenate([w[:, :OFF_DQ], interleave(w[:, OFF_DQ:OFF_DK]),
                            interleave(w[:, OFF_DK:OFF_DV]), w[:, OFF_DV:]], axis=1).astype(BF16)


def _pair_lanes(a, b):
    half = DF_QK // 2
    return jnp.concatenate([a[..., :half], b[..., :half], a[..., half:], b[..., half:]], -1)


def _unpair_lanes(kk):
    half = DF_QK // 2
    a = jnp.concatenate([kk[..., 0:half], kk[..., 2 * half:3 * half]], -1)
    b = jnp.concatenate([kk[..., half:2 * half], kk[..., 3 * half:]], -1)
    return a, b


def _rope_tables(t):
    tok = jnp.arange(t)
    row = (tok // GRID_W).astype(F32)
    col = (tok % GRID_W).astype(F32)
    n_freq = DF_QK // 4
    inv = 1.0 / (ROPE_THETA ** (jnp.arange(n_freq, dtype=F32) / n_freq))
    ang = jnp.concatenate([row[:, None] * inv, col[:, None] * inv], -1)
    cos, sin = jnp.cos(ang), jnp.sin(ang)
    return jnp.concatenate([cos] * 4, -1), jnp.concatenate([-sin, -sin, sin, sin], -1)


def _na_bias_table(rpb):
    cols = np.arange(GRID_W)
    start = np.clip(cols - NA_WIN_C // 2, 0, GRID_W - NA_WIN_C)
    inwin = (cols[None, :] >= start[:, None]) & (cols[None, :] < start[:, None] + NA_WIN_C)
    dc = np.clip(cols[None, :] - cols[:, None] + NA_WIN_C - 1, 0, 2 * NA_WIN_C - 2)
    dr = np.arange(NA_WIN_R)[None, :] - np.arange(NA_WIN_R)[:, None] + NA_WIN_R - 1
    tab = rpb[:, dr[:, None, :, None], dc[None, :, None, :]]
    tab = jnp.where(inwin[None, None, :, None, :], tab, NEG_BIG)
    return tab.reshape(rpb.shape[0], NA_WIN_R, GRID_W, NA_WIN_R * GRID_W)


def _heads_out(a, b, t, h):
    return a.reshape(b, t, h, -1).transpose(0, 2, 1, 3)


def _peer_block(oa, ob, oc, x, mod, tiles_out, w_out, ln_g, ln_b, wqt, keys, u_tab, vt_tab):
    x1, h2t = _out_proj(oa, ob, oc, x, mod, w_out, ln_g[0], ln_b[0], tiles_out)
    s2, e2, tau, q1 = _peer_score(h2t, wqt, keys)
    yt = _peer_dense(h2t, u_tab, vt_tab, s2, e2, tau, q1)
    return _peer_ln(yt, x1, mod, ln_g[1], ln_b[1], tiles_out)


def kernel(x_prompt, x_sample, cache_na_k, cache_na_v, cache_df_k1, cache_df_k2, cache_df_v, c, c_ctx,
           w_mod, b_mod, w_in, na_rpb, sg_ln_g, sg_ln_b, sg_w, sg_b, df_lambda, df_subln_g, w_out,
           pk_wq, pk_keys, pk_u, pk_v, ln_g, ln_b):
    bp, tp, d = x_prompt.shape
    bs, ts, _ = x_sample.shape
    n_p, n_s = bp * tp, bs * ts

    cond8 = jnp.zeros((8, d), F32).at[0].set(c_ctx).at[1:1 + bs].set(c)
    mods = _modulation(cond8, w_mod, b_mod).reshape(DEPTH, 8, 6, d)

    cos_t, sin_t = _rope_tables(ts)
    cache_dk = _pair_lanes(cache_df_k1, cache_df_k2)

    xp = x_prompt.reshape(n_p, d)
    xs = x_sample.reshape(n_s, d)
    st = [[] for _ in range(5)]
    for l in range(DEPTH):
        lam_init = 0.8 - 0.6 * math.exp(-0.3 * l)
        w_in_l = _permute_in_proj(w_in[l])
        w_out_l = w_out[l].astype(BF16)
        wqt = pk_wq[l].T.astype(BF16)
        keys = pk_keys[l].reshape(2 * PK_HEADS, PK_NKEYS, PK_QDIM // 2).astype(BF16)
        u_tab = pk_u[l].astype(BF16)
        vt_tab = pk_v[l].T.astype(BF16)
        ws = sg_w[l].astype(BF16)
        bs_col = sg_b[l].reshape(SG_GROUPS, SG_CHUNK, 1)
        bias = _na_bias_table(na_rpb[l])
        mod_p = mods[l, 0:1]
        mod_s = mods[l, 1:1 + bs]

        proj = _in_proj(xp, mod_p, w_in_l, n_p // TM_IN)
        proj3 = proj.reshape(bp, tp, IN_WIDTH)
        oa, oc = _ctx_attn(proj3, df_lambda[l], df_subln_g[l], lam_init)
        ob = _spatial_gate(proj3, sg_ln_g[l], sg_ln_b[l], ws, bs_col)
        st[0].append(_heads_out(proj[:, OFF_KA:OFF_KA + NA_WIDTH], bp, tp, NA_HEADS))
        st[1].append(_heads_out(proj[:, OFF_VA:OFF_VA + NA_WIDTH], bp, tp, NA_HEADS))
        k1, k2 = _unpair_lanes(_heads_out(proj[:, OFF_DK:OFF_DV], bp, tp, DF_HEADS))
        st[2].append(k1)
        st[3].append(k2)
        st[4].append(_heads_out(proj[:, OFF_DV:], bp, tp, DF_HEADS))
        xp = _peer_block(oa.reshape(n_p, NA_WIDTH), ob.reshape(n_p, SG_WIDTH), oc.reshape(n_p, DF_WIDTH),
                         xp, mod_p, n_p // TM_OUT, w_out_l, ln_g[l], ln_b[l], wqt, keys, u_tab, vt_tab)

        proj = _in_proj(xs, mod_s, w_in_l, ts // TM_IN)
        proj3 = proj.reshape(bs, ts, IN_WIDTH)
        oa = _lat_na(proj3, cache_na_k, cache_na_v, bias, l)
        ob = _spatial_gate(proj3, sg_ln_g[l], sg_ln_b[l], ws, bs_col)
        oc = _lat_df(proj3, cache_dk, cache_df_v, cos_t, sin_t, df_lambda[l], df_subln_g[l], lam_init, l)
        xs = _peer_block(oa.reshape(n_s, NA_WIDTH), ob.reshape(n_s, SG_WIDTH), oc.reshape(n_s, DF_WIDTH),
                         xs, mod_s, ts // TM_OUT, w_out_l, ln_g[l], ln_b[l], wqt, keys, u_tab, vt_tab)

    outs = [jnp.stack(s, 1) for s in st]
    return (xp.reshape(bp, tp, d), xs.reshape(bs, ts, d), *outs)
```

```python
import functools
import math

import numpy as np
import jax
import jax.numpy as jnp
from jax import lax
from jax.experimental import pallas as pl
from jax.experimental.pallas import tpu as pltpu

F32 = jnp.float32
BF16 = jnp.bfloat16

D_MODEL = 2048
DEPTH = 4
GRID_W = 64
NA_HEADS = 8
NA_DIM = 128
NA_WIN_R = 8
NA_WIN_C = 16
SG_GROUPS = 4
SG_DIM = 128
SG_CHUNK = 128
DF_HEADS = 4
DF_QK = 64
DF_V = 128
ROPE_THETA = 10000.0
NA_WIDTH = NA_HEADS * NA_DIM
SG_WIDTH = SG_GROUPS * SG_DIM
DF_WIDTH = DF_HEADS * DF_V
MIX_WIDTH = NA_WIDTH + SG_WIDTH + DF_WIDTH
IN_WIDTH = 3 * NA_WIDTH + 2 * SG_WIDTH + 4 * DF_HEADS * DF_QK + DF_WIDTH
PK_HEADS = 8
PK_QDIM = 256
PK_NKEYS = 128
PK_TOPK = 16
PK_EXPERTS = PK_NKEYS * PK_NKEYS
DN_ALPHA = (2 * DEPTH) ** 0.25
LN_EPS = 1e-5
RMS_EPS = 1e-6

OFF_QA = 0
OFF_KA = NA_WIDTH
OFF_VA = 2 * NA_WIDTH
OFF_U = 3 * NA_WIDTH
OFF_V = OFF_U + SG_WIDTH
OFF_DQ = OFF_V + SG_WIDTH
OFF_DK = OFF_DQ + 2 * DF_HEADS * DF_QK
OFF_DV = OFF_DK + 2 * DF_HEADS * DF_QK

LANES = 128
MIB = 1024 * 1024

TM_IN = 512
TN_IN = IN_WIDTH // 4
TM_OUT = 512
TM_SCORE = 256
TM_PEER = 512
TE_PEER = 512
RJ_PEER = 16
TQ_DF = 256
TN_MOD = 1024
NEG_BIG = -1e30


def _cp(sem, vmem_mib):
    return pltpu.CompilerParams(dimension_semantics=sem, vmem_limit_bytes=vmem_mib * MIB)


def _dot(a, b):
    return jnp.dot(a, b, preferred_element_type=F32)


def _dot_nt(a, b):
    return lax.dot_general(a, b, (((1,), (1,)), ((), ())), preferred_element_type=F32)


def _layernorm(z, g, b):
    mu = jnp.mean(z, -1, keepdims=True)
    d = z - mu
    var = jnp.mean(d * d, -1, keepdims=True)
    return d * lax.rsqrt(var + LN_EPS) * g + b


def _softmax_rows(s):
    m = jnp.max(s, -1, keepdims=True)
    e = jnp.exp(s - m)
    return e * (1.0 / jnp.sum(e, -1, keepdims=True))


def _diff_lambda(lam_ref, lam_init):
    lf = lam_ref[...]
    a = jnp.sum(lf[0:1, :] * lf[1:2, :], axis=1, keepdims=True)
    b = jnp.sum(lf[2:3, :] * lf[3:4, :], axis=1, keepdims=True)
    return jnp.exp(a) - jnp.exp(b) + lam_init


def _map1_mask():
    lane = lax.broadcasted_iota(jnp.int32, (1, LANES), 1)
    return (lane // (DF_QK // 2)) % 2 == 0


def _subln(o, g, lam_init):
    return o * lax.rsqrt(jnp.mean(o * o, -1, keepdims=True) + RMS_EPS) * g * (1.0 - lam_init)


def _mod_kernel(c_ref, w_ref, b_ref, o_ref):
    c = c_ref[...]
    a = (c * jax.nn.sigmoid(c)).astype(BF16)
    o_ref[0] = _dot(a, w_ref[0].astype(BF16)) + b_ref[0]


def _modulation(cond8, w_mod, b_mod):
    depth, d, n6 = w_mod.shape
    return pl.pallas_call(
        _mod_kernel,
        grid=(depth, n6 // TN_MOD),
        in_specs=[pl.BlockSpec((8, d), lambda l, j: (0, 0)),
                  pl.BlockSpec((1, d, TN_MOD), lambda l, j: (l, 0, j)),
                  pl.BlockSpec((1, 1, TN_MOD), lambda l, j: (l, 0, j))],
        out_specs=pl.BlockSpec((1, 8, TN_MOD), lambda l, j: (l, 0, j)),
        out_shape=jax.ShapeDtypeStruct((depth, 8, n6), F32),
        compiler_params=_cp(("parallel", "parallel"), 32),
        name="modulation",
    )(cond8, w_mod, b_mod.reshape(depth, 1, n6))


def _in_proj_kernel(x_ref, mod_ref, w_ref, o_ref, h_ref):
    @pl.when(pl.program_id(1) == 0)
    def _():
        sh = mod_ref[0, 0:1, :]
        sc = mod_ref[0, 1:2, :]
        h_ref[...] = (x_ref[...] * (1 + sc) + sh).astype(BF16)

    o_ref[...] = _dot(h_ref[...], w_ref[...])


def _in_proj(x, mod, w, tiles_per_mod):
    n, d = x.shape
    nw = w.shape[1]
    return pl.pallas_call(
        _in_proj_kernel,
        grid=(n // TM_IN, nw // TN_IN),
        in_specs=[pl.BlockSpec((TM_IN, d), lambda i, j: (i, 0)),
                  pl.BlockSpec((1, 6, d), lambda i, j: (i // tiles_per_mod, 0, 0)),
                  pl.BlockSpec((d, TN_IN), lambda i, j: (0, j))],
        out_specs=pl.BlockSpec((TM_IN, TN_IN), lambda i, j: (i, j)),
        out_shape=jax.ShapeDtypeStruct((n, nw), F32),
        scratch_shapes=[pltpu.VMEM((TM_IN, d), BF16)],
        compiler_params=_cp(("parallel", "arbitrary"), 40),
        name="in_proj",
    )(x, mod, w)


def _ctx_attn_kernel(p_ref, lam_ref, g_ref, oa_ref, oc_ref, *, lam_init):
    scale = NA_DIM ** -0.5
    for h in range(NA_HEADS):
        sl = slice(h * NA_DIM, (h + 1) * NA_DIM)
        q = p_ref[0, :, OFF_QA + h * NA_DIM:OFF_QA + (h + 1) * NA_DIM].astype(BF16)
        k = p_ref[0, :, OFF_KA + h * NA_DIM:OFF_KA + (h + 1) * NA_DIM].astype(BF16)
        v = p_ref[0, :, OFF_VA + h * NA_DIM:OFF_VA + (h + 1) * NA_DIM].astype(BF16)
        p = _softmax_rows(_dot_nt(q, k) * scale)
        oa_ref[0, :, sl] = _dot(p.astype(BF16), v).astype(BF16)
    lam = _diff_lambda(lam_ref, lam_init)
    m1 = _map1_mask()
    dscale = DF_QK ** -0.5
    for h in range(DF_HEADS):
        sl = slice(h * DF_V, (h + 1) * DF_V)
        q = p_ref[0, :, OFF_DQ + h * LANES:OFF_DQ + (h + 1) * LANES] * dscale
        k = p_ref[0, :, OFF_DK + h * LANES:OFF_DK + (h + 1) * LANES].astype(BF16)
        v = p_ref[0, :, OFF_DV + h * DF_V:OFF_DV + (h + 1) * DF_V].astype(BF16)
        qa = jnp.where(m1, q, 0.0).astype(BF16)
        qb = jnp.where(m1, 0.0, q).astype(BF16)
        p = _softmax_rows(_dot_nt(qa, k)) - lam * _softmax_rows(_dot_nt(qb, k))
        o = _dot(p.astype(BF16), v)
        oc_ref[0, :, sl] = _subln(o, g_ref[...], lam_init).astype(BF16)


def _ctx_attn(proj3, lam_p, subln_g, lam_init):
    b, t, nw = proj3.shape
    return pl.pallas_call(
        functools.partial(_ctx_attn_kernel, lam_init=lam_init),
        grid=(b,),
        in_specs=[pl.BlockSpec((1, t, nw), lambda i: (i, 0, 0)),
                  pl.BlockSpec((4, DF_QK), lambda i: (0, 0)),
                  pl.BlockSpec((1, DF_V), lambda i: (0, 0))],
        out_specs=[pl.BlockSpec((1, t, NA_WIDTH), lambda i: (i, 0, 0)),
                   pl.BlockSpec((1, t, DF_WIDTH), lambda i: (i, 0, 0))],
        out_shape=[jax.ShapeDtypeStruct((b, t, NA_WIDTH), BF16),
                   jax.ShapeDtypeStruct((b, t, DF_WIDTH), BF16)],
        compiler_params=_cp(("parallel",), 32),
        name="ctx_attn",
    )(proj3, lam_p, subln_g.reshape(1, DF_V))


def _lat_na_kernel(q_ref, k_ref, v_ref, kc_ref, vc_ref, bias_ref, o_ref, kb_ref, vb_ref, *, rows):
    kb_ref[...] = k_ref[0].astype(BF16)
    vb_ref[...] = v_ref[0].astype(BF16)
    kc = kc_ref[0, 0, 0].astype(BF16)
    vc = vc_ref[0, 0, 0].astype(BF16)
    scale = NA_DIM ** -0.5
    band = NA_WIN_R * GRID_W

    def body(r, carry):
        r0 = jnp.clip(r - NA_WIN_R // 2, 0, rows - NA_WIN_R)
        q = q_ref[0, pl.ds(pl.multiple_of(r * GRID_W, GRID_W), GRID_W), :].astype(BF16)
        k0 = pl.multiple_of(r0 * GRID_W, GRID_W)
        kw = kb_ref[pl.ds(k0, band), :]
        vw = vb_ref[pl.ds(k0, band), :]
        s_loc = _dot_nt(q, kw) * scale + bias_ref[0, r - r0]
        s_ctx = _dot_nt(q, kc) * scale
        m = jnp.maximum(jnp.max(s_loc, -1, keepdims=True), jnp.max(s_ctx, -1, keepdims=True))
        e_loc = jnp.exp(s_loc - m)
        e_ctx = jnp.exp(s_ctx - m)
        inv = 1.0 / (jnp.sum(e_loc, -1, keepdims=True) + jnp.sum(e_ctx, -1, keepdims=True))
        o = _dot((e_loc * inv).astype(BF16), vw) + _dot((e_ctx * inv).astype(BF16), vc)
        o_ref[0, pl.ds(pl.multiple_of(r * GRID_W, GRID_W), GRID_W), :] = o.astype(BF16)
        return carry

    lax.fori_loop(0, rows, body, 0)


def _lat_na(proj3, cache_k, cache_v, bias, layer):
    b, t, _ = proj3.shape
    past = cache_k.shape[3]
    rows = t // GRID_W
    qb, kb, vb = OFF_QA // NA_DIM, OFF_KA // NA_DIM, OFF_VA // NA_DIM
    return pl.pallas_call(
        functools.partial(_lat_na_kernel, rows=rows),
        grid=(b, NA_HEADS),
        in_specs=[pl.BlockSpec((1, t, NA_DIM), lambda i, h: (i, 0, qb + h)),
                  pl.BlockSpec((1, t, NA_DIM), lambda i, h: (i, 0, kb + h)),
                  pl.BlockSpec((1, t, NA_DIM), lambda i, h: (i, 0, vb + h)),
                  pl.BlockSpec((1, 1, 1, past, NA_DIM), lambda i, h: (i, layer, h, 0, 0)),
                  pl.BlockSpec((1, 1, 1, past, NA_DIM), lambda i, h: (i, layer, h, 0, 0)),
                  pl.BlockSpec((1, NA_WIN_R, GRID_W, NA_WIN_R * GRID_W), lambda i, h: (h, 0, 0, 0))],
        out_specs=pl.BlockSpec((1, t, NA_DIM), lambda i, h: (i, 0, h)),
        out_shape=jax.ShapeDtypeStruct((b, t, NA_WIDTH), BF16),
        scratch_shapes=[pltpu.VMEM((t, NA_DIM), BF16), pltpu.VMEM((t, NA_DIM), BF16)],
        compiler_params=_cp(("parallel", "parallel"), 40),
        name="lat_na",
    )(proj3, proj3, proj3, cache_k, cache_v, bias)


def _lat_df_kernel(q_ref, k_ref, v_ref, kc_ref, vc_ref, cos_ref, sin_ref, lam_ref, g_ref, o_ref,
                   kall_ref, vall_ref, *, t, lam_init):
    qi = pl.program_id(2)

    @pl.when(qi == 0)
    def _():
        k = k_ref[0]
        kall_ref[0:t, :] = (k * cos_ref[...] + pltpu.roll(k, LANES // 2, 1) * sin_ref[...]).astype(BF16)
        kall_ref[t:, :] = kc_ref[0, 0, 0].astype(BF16)
        vall_ref[0:t, :] = v_ref[0].astype(BF16)
        vall_ref[t:, :] = vc_ref[0, 0, 0].astype(BF16)

    t0 = pl.multiple_of(qi * TQ_DF, TQ_DF)
    q = q_ref[0]
    q = q * cos_ref[pl.ds(t0, TQ_DF), :] + pltpu.roll(q, LANES // 2, 1) * sin_ref[pl.ds(t0, TQ_DF), :]
    q = q * DF_QK ** -0.5
    m1 = _map1_mask()
    qa = jnp.where(m1, q, 0.0).astype(BF16)
    qb = jnp.where(m1, 0.0, q).astype(BF16)
    lam = _diff_lambda(lam_ref, lam_init)
    kall = kall_ref[...]
    p = _softmax_rows(_dot_nt(qa, kall)) - lam * _softmax_rows(_dot_nt(qb, kall))
    o = _dot(p.astype(BF16), vall_ref[...])
    o_ref[0] = _subln(o, g_ref[...], lam_init).astype(BF16)


def _lat_df(proj3, cache_kp, cache_v, cos_t, sin_t, lam_p, subln_g, lam_init, layer):
    b, t, _ = proj3.shape
    past = cache_v.shape[3]
    qb, kb, vb = OFF_DQ // LANES, OFF_DK // LANES, OFF_DV // LANES
    return pl.pallas_call(
        functools.partial(_lat_df_kernel, t=t, lam_init=lam_init),
        grid=(b, DF_HEADS, t // TQ_DF),
        in_specs=[pl.BlockSpec((1, TQ_DF, LANES), lambda i, h, j: (i, j, qb + h)),
                  pl.BlockSpec((1, t, LANES), lambda i, h, j: (i, 0, kb + h)),
                  pl.BlockSpec((1, t, LANES), lambda i, h, j: (i, 0, vb + h)),
                  pl.BlockSpec((1, 1, 1, past, LANES), lambda i, h, j: (i, layer, h, 0, 0)),
                  pl.BlockSpec((1, 1, 1, past, DF_V), lambda i, h, j: (i, layer, h, 0, 0)),
                  pl.BlockSpec((t, LANES), lambda i, h, j: (0, 0)),
                  pl.BlockSpec((t, LANES), lambda i, h, j: (0, 0)),
                  pl.BlockSpec((4, DF_QK), lambda i, h, j: (0, 0)),
                  pl.BlockSpec((1, DF_V), lambda i, h, j: (0, 0))],
        out_specs=pl.BlockSpec((1, TQ_DF, DF_V), lambda i, h, j: (i, j, h)),
        out_shape=jax.ShapeDtypeStruct((b, t, DF_WIDTH), BF16),
        scratch_shapes=[pltpu.VMEM((t + past, LANES), BF16), pltpu.VMEM((t + past, DF_V), BF16)],
        compiler_params=_cp(("parallel", "parallel", "arbitrary"), 48),
        name="lat_df",
    )(proj3, proj3, proj3, cache_kp, cache_v, cos_t, sin_t, lam_p, subln_g.reshape(1, DF_V))


def _sg_kernel(u_ref, v_ref, g_ref, b_ref, ws_ref, bs_ref, o_ref, *, nch):
    for c in range(nch):
        rs = slice(c * SG_CHUNK, (c + 1) * SG_CHUNK)
        vn = _layernorm(v_ref[0, rs, :], g_ref[...], b_ref[...]).astype(BF16)
        for g in range(SG_GROUPS):
            cs = slice(g * SG_DIM, (g + 1) * SG_DIM)
            mixed = _dot(ws_ref[g], vn[:, cs]) + bs_ref[g]
            o_ref[0, rs, cs] = (u_ref[0, rs, cs] * mixed).astype(BF16)


def _spatial_gate(proj3, ln_g, ln_b, ws, bs):
    b, t, _ = proj3.shape
    nch = min(4, t // SG_CHUNK)
    tt = nch * SG_CHUNK
    ub, vb = OFF_U // SG_WIDTH, OFF_V // SG_WIDTH
    return pl.pallas_call(
        functools.partial(_sg_kernel, nch=nch),
        grid=(b, t // tt),
        in_specs=[pl.BlockSpec((1, tt, SG_WIDTH), lambda i, j: (i, j, ub)),
                  pl.BlockSpec((1, tt, SG_WIDTH), lambda i, j: (i, j, vb)),
                  pl.BlockSpec((1, SG_WIDTH), lambda i, j: (0, 0)),
                  pl.BlockSpec((1, SG_WIDTH), lambda i, j: (0, 0)),
                  pl.BlockSpec((SG_GROUPS, SG_CHUNK, SG_CHUNK), lambda i, j: (0, 0, 0)),
                  pl.BlockSpec((SG_GROUPS, SG_CHUNK, 1), lambda i, j: (0, 0, 0))],
        out_specs=pl.BlockSpec((1, tt, SG_WIDTH), lambda i, j: (i, j, 0)),
        out_shape=jax.ShapeDtypeStruct((b, t, SG_WIDTH), BF16),
        compiler_params=_cp(("parallel", "parallel"), 16),
        name="spatial_gate",
    )(proj3, proj3, ln_g.reshape(1, SG_WIDTH), ln_b.reshape(1, SG_WIDTH), ws, bs)


def _out_proj_kernel(oa_ref, ob_ref, oc_ref, x_ref, mod_ref, w_ref, lg_ref, lb_ref, x1_ref, h2t_ref):
    y = (_dot(oa_ref[...], w_ref[0:NA_WIDTH, :])
         + _dot(ob_ref[...], w_ref[NA_WIDTH:NA_WIDTH + SG_WIDTH, :])
         + _dot(oc_ref[...], w_ref[NA_WIDTH + SG_WIDTH:, :]))
    g1 = mod_ref[0, 2:3, :]
    sh2 = mod_ref[0, 3:4, :]
    sc2 = mod_ref[0, 4:5, :]
    x1 = _layernorm(DN_ALPHA * x_ref[...] + g1 * y, lg_ref[...], lb_ref[...])
    x1_ref[...] = x1
    h2t_ref[...] = (x1 * (1 + sc2) + sh2).T.astype(BF16)


def _out_proj(oa, ob, oc, x, mod, w, ln_g, ln_b, tiles_per_mod):
    n, d = x.shape
    return pl.pallas_call(
        _out_proj_kernel,
        grid=(n // TM_OUT,),
        in_specs=[pl.BlockSpec((TM_OUT, NA_WIDTH), lambda i: (i, 0)),
                  pl.BlockSpec((TM_OUT, SG_WIDTH), lambda i: (i, 0)),
                  pl.BlockSpec((TM_OUT, DF_WIDTH), lambda i: (i, 0)),
                  pl.BlockSpec((TM_OUT, d), lambda i: (i, 0)),
                  pl.BlockSpec((1, 6, d), lambda i: (i // tiles_per_mod, 0, 0)),
                  pl.BlockSpec((MIX_WIDTH, d), lambda i: (0, 0)),
                  pl.BlockSpec((1, d), lambda i: (0, 0)),
                  pl.BlockSpec((1, d), lambda i: (0, 0))],
        out_specs=[pl.BlockSpec((TM_OUT, d), lambda i: (i, 0)),
                   pl.BlockSpec((d, TM_OUT), lambda i: (0, i))],
        out_shape=[jax.ShapeDtypeStruct((n, d), F32),
                   jax.ShapeDtypeStruct((d, n), BF16)],
        compiler_params=_cp(("parallel",), 48),
        name="out_proj",
    )(oa, ob, oc, x, mod, w, ln_g.reshape(1, d), ln_b.reshape(1, d))


_PK_PAIRS = [(a, b) for a in range(PK_TOPK) for b in range(PK_TOPK) if (a + 1) * (b + 1) <= PK_TOPK]


def _merge_exchange_network(n):
    t = int(math.ceil(math.log2(n)))
    p = 2 ** (t - 1)
    pairs = []
    while p > 0:
        q, r, d = 2 ** (t - 1), 0, p
        while d > 0:
            pairs.extend((i, i + d) for i in range(n - d) if i & p == r)
            d, q, r = q - p, q // 2, p
        p //= 2
    return pairs


def _bitonic_merge_network(n):
    pairs, d = [], n // 2
    while d >= 1:
        pairs.extend((i, i + d) for i in range(n) if i & d == 0)
        d //= 2
    return pairs


_SORT16 = _merge_exchange_network(PK_TOPK)
_MERGE16 = _bitonic_merge_network(PK_TOPK)


def _compare_exchange(xs, network):
    xs = list(xs)
    for i, j in network:
        xs[i], xs[j] = jnp.maximum(xs[i], xs[j]), jnp.minimum(xs[i], xs[j])
    return xs


def _peer_score_kernel(h2t_ref, wqt_ref, keys_ref, s2_out, e2_out, tau_out, q1_out,
                       qt_ref, s_ref, tt_ref, thr_ref, z_ref):
    tm = h2t_ref.shape[1]
    qt_ref[...] = _dot(wqt_ref[...], h2t_ref[...]).astype(BF16)
    neg_inf = -jnp.inf
    sub = 8

    for c in range(2 * PK_HEADS):
        s = _dot(keys_ref[c], qt_ref[c * PK_NKEYS:(c + 1) * PK_NKEYS, :])
        s_ref[c] = s
        head, half = c // 2, c % 2
        xs = _compare_exchange([s[r * sub:(r + 1) * sub, :] for r in range(PK_TOPK)], _SORT16)
        shift = 1
        while shift < sub:
            ys = [pltpu.roll(xs[PK_TOPK - 1 - r], shift, 0) for r in range(PK_TOPK)]
            xs = _compare_exchange([jnp.maximum(x, y) for x, y in zip(xs, ys)], _MERGE16)
            shift *= 2
        for r in range(PK_TOPK):
            tt_ref[half, r, head:head + 1, :] = xs[r][0:1, :]

    def per_lane_chunk(ch, carry):
        l0 = pl.multiple_of(ch * LANES, LANES)
        t1 = [tt_ref[0, a, :, pl.ds(l0, LANES)] for a in range(PK_TOPK)]
        t2 = [tt_ref[1, b, :, pl.ds(l0, LANES)] for b in range(PK_TOPK)]
        cands = [t1[a] + t2[b] for a, b in _PK_PAIRS]
        cur = list(cands)
        for _ in range(PK_TOPK - 1):
            m = functools.reduce(jnp.maximum, cur)
            found = jnp.zeros(m.shape, jnp.bool_)
            nxt = []
            for cnd in cur:
                is_m = cnd == m
                nxt.append(jnp.where(jnp.logical_and(is_m, jnp.logical_not(found)), neg_inf, cnd))
                found = jnp.logical_or(found, is_m)
            cur = nxt
        thr = functools.reduce(jnp.maximum, cur)
        e1 = [jnp.exp(t1[a] - t1[0]) for a in range(PK_TOPK)]
        e2 = [jnp.exp(t2[b] - t2[0]) for b in range(PK_TOPK)]
        z = jnp.zeros(thr.shape, F32)
        for (a, b), cnd in zip(_PK_PAIRS, cands):
            z = z + jnp.where(cnd >= thr, e1[a] * e2[b], 0.0)
        thr_ref[:, pl.ds(l0, LANES)] = thr
        z_ref[:, pl.ds(l0, LANES)] = z
        return carry

    lax.fori_loop(0, tm // LANES, per_lane_chunk, 0)

    for h in range(PK_HEADS):
        s1 = s_ref[2 * h]
        s2 = s_ref[2 * h + 1]
        thr = thr_ref[h:h + 1, :]
        tau = jnp.full(s1.shape, jnp.inf, F32)
        for b in range(PK_TOPK):
            t2b = tt_ref[1, b, h:h + 1, :]
            tau = jnp.minimum(tau, jnp.where(s1 + t2b >= thr, t2b, jnp.inf))
        tau_out[h] = tau
        q1_out[h] = jnp.exp(s1 - tt_ref[0, 0, h:h + 1, :]) * (0.5 / z_ref[h:h + 1, :])
        e2_out[h] = jnp.exp(s2 - tt_ref[1, 0, h:h + 1, :])
        s2_out[h] = s2


def _peer_score(h2t, wqt, keys):
    d, n = h2t.shape
    spec = pl.BlockSpec((PK_HEADS, PK_NKEYS, TM_SCORE), lambda i: (0, 0, i))
    shape = jax.ShapeDtypeStruct((PK_HEADS, PK_NKEYS, n), F32)
    return pl.pallas_call(
        _peer_score_kernel,
        grid=(n // TM_SCORE,),
        in_specs=[pl.BlockSpec((d, TM_SCORE), lambda i: (0, i)),
                  pl.BlockSpec(wqt.shape, lambda i: (0, 0)),
                  pl.BlockSpec(keys.shape, lambda i: (0, 0, 0))],
        out_specs=[spec, spec, spec, spec],
        out_shape=[shape, shape, shape, shape],
        scratch_shapes=[pltpu.VMEM((PK_HEADS * PK_QDIM, TM_SCORE), BF16),
                        pltpu.VMEM((2 * PK_HEADS, PK_NKEYS, TM_SCORE), F32),
                        pltpu.VMEM((2, PK_TOPK, PK_HEADS, TM_SCORE), F32),
                        pltpu.VMEM((PK_HEADS, TM_SCORE), F32),
                        pltpu.VMEM((PK_HEADS, TM_SCORE), F32)],
        compiler_params=_cp(("parallel",), 48),
        name="peer_score",
    )(h2t, wqt, keys)


def _peer_dense_kernel(h2t_ref, u_ref, vt_ref, s2_ref, e2_ref, tau_ref, q1_ref, o_ref,
                       acc_ref, a_ref, w_ref):
    e = pl.program_id(1)
    tm = h2t_ref.shape[1]
    ni = TE_PEER // PK_NKEYS
    nj = PK_NKEYS // RJ_PEER
    sqrt_half = math.sqrt(0.5)

    @pl.when(e == 0)
    def _():
        acc_ref[...] = jnp.zeros_like(acc_ref)

    a_ref[...] = _dot(u_ref[...], h2t_ref[...])

    def chunk(t, carry):
        ii = t // nj
        jc = t % nj
        i = e * ni + ii
        j0 = pl.multiple_of(jc * RJ_PEER, RJ_PEER)
        row0 = pl.multiple_of(ii * PK_NKEYS + jc * RJ_PEER, RJ_PEER)
        g = jnp.zeros((RJ_PEER, tm), F32)
        for h in range(PK_HEADS):
            tau = tau_ref[h, pl.ds(i, 1), :]
            q1 = q1_ref[h, pl.ds(i, 1), :]
            s2 = s2_ref[h, pl.ds(j0, RJ_PEER), :]
            e2 = e2_ref[h, pl.ds(j0, RJ_PEER), :]
            g = g + jnp.where(s2 >= tau, e2, 0.0) * q1
        a = a_ref[pl.ds(row0, RJ_PEER), :]
        act = a * (1.0 + lax.erf(a * sqrt_half))
        w_ref[pl.ds(row0, RJ_PEER), :] = (g * act).astype(BF16)
        return carry

    lax.fori_loop(0, ni * nj, chunk, 0, unroll=2)
    acc_ref[...] += _dot(vt_ref[...], w_ref[...])

    @pl.when(e == pl.num_programs(1) - 1)
    def _():
        o_ref[...] = acc_ref[...]


def _peer_dense(h2t, u_tab, vt_tab, s2, e2, tau, q1):
    d, n = h2t.shape
    ne = u_tab.shape[0]
    gspec = pl.BlockSpec((PK_HEADS, PK_NKEYS, TM_PEER), lambda i, e: (0, 0, i))
    return pl.pallas_call(
        _peer_dense_kernel,
        grid=(n // TM_PEER, ne // TE_PEER),
        in_specs=[pl.BlockSpec((d, TM_PEER), lambda i, e: (0, i)),
                  pl.BlockSpec((TE_PEER, d), lambda i, e: (e, 0)),
                  pl.BlockSpec((d, TE_PEER), lambda i, e: (0, e)),
                  gspec, gspec, gspec, gspec],
        out_specs=pl.BlockSpec((d, TM_PEER), lambda i, e: (0, i)),
        out_shape=jax.ShapeDtypeStruct((d, n), F32),
        scratch_shapes=[pltpu.VMEM((d, TM_PEER), F32),
                        pltpu.VMEM((TE_PEER, TM_PEER), F32),
                        pltpu.VMEM((TE_PEER, TM_PEER), BF16)],
        compiler_params=_cp(("parallel", "arbitrary"), 56),
        name="peer_dense",
    )(h2t, u_tab, vt_tab, s2, e2, tau, q1)


def _peer_ln_kernel(yt_ref, x1_ref, mod_ref, lg_ref, lb_ref, o_ref):
    g2 = mod_ref[0, 5:6, :]
    o_ref[...] = _layernorm(DN_ALPHA * x1_ref[...] + g2 * yt_ref[...].T, lg_ref[...], lb_ref[...])


def _peer_ln(yt, x1, mod, ln_g, ln_b, tiles_per_mod):
    n, d = x1.shape
    return pl.pallas_call(
        _peer_ln_kernel,
        grid=(n // TM_OUT,),
        in_specs=[pl.BlockSpec((d, TM_OUT), lambda i: (0, i)),
                  pl.BlockSpec((TM_OUT, d), lambda i: (i, 0)),
                  pl.BlockSpec((1, 6, d), lambda i: (i // tiles_per_mod, 0, 0)),
                  pl.BlockSpec((1, d), lambda i: (0, 0)),
                  pl.BlockSpec((1, d), lambda i: (0, 0))],
        out_specs=pl.BlockSpec((TM_OUT, d), lambda i: (i, 0)),
        out_shape=jax.ShapeDtypeStruct((n, d), F32),
        compiler_params=_cp(("parallel",), 40),
        name="peer_ln",
    )(yt, x1, mod, ln_g.reshape(1, d), ln_b.reshape(1, d))


def _permute_in_proj(w):
    d = w.shape[0]

    def interleave(cols):
        c = cols.reshape(d, 2, DF_HEADS, 2, DF_QK // 2)
        return c.transpose(0, 2, 3, 1, 4).reshape(d, 2 * DF_HEADS * DF_QK)

    return jnp.concatenate([w[:, :OFF_DQ], interleave(w[:, OFF_DQ:OFF_DK]),
                            interleave(w[:, OFF_DK:OFF_DV]), w[:, OFF_DV:]], axis=1).astype(BF16)


def _pair_lanes(a, b):
    half = DF_QK // 2
    return jnp.concatenate([a[..., :half], b[..., :half], a[..., half:], b[..., half:]], -1)


def _unpair_lanes(kk):
    half = DF_QK // 2
    a = jnp.concatenate([kk[..., 0:half], kk[..., 2 * half:3 * half]], -1)
    b = jnp.concatenate([kk[..., half:2 * half], kk[..., 3 * half:]], -1)
    return a, b


def _rope_tables(t):
    tok = jnp.arange(t)
    row = (tok // GRID_W).astype(F32)
    col = (tok % GRID_W).astype(F32)
    n_freq = DF_QK // 4
    inv = 1.0 / (ROPE_THETA ** (jnp.arange(n_freq, dtype=F32) / n_freq))
    ang = jnp.concatenate([row[:, None] * inv, col[:, None] * inv], -1)
    cos, sin = jnp.cos(ang), jnp.sin(ang)
    return jnp.concatenate([cos] * 4, -1), jnp.concatenate([-sin, -sin, sin, sin], -1)


def _na_bias_table(rpb):
    cols = np.arange(GRID_W)
    start = np.clip(cols - NA_WIN_C // 2, 0, GRID_W - NA_WIN_C)
    inwin = (cols[None, :] >= start[:, None]) & (cols[None, :] < start[:, None] + NA_WIN_C)
    nh = rpb.shape[0]
    rows = jnp.stack([rpb[:, NA_WIN_R - 1 - cs:2 * NA_WIN_R - 1 - cs, :] for cs in range(NA_WIN_R)], 1)
    lead = GRID_W - NA_WIN_C
    p = jnp.pad(rows, ((0, 0), (0, 0), (0, 0), (lead, 2 * GRID_W - lead - (2 * NA_WIN_C - 1))))
    m = jnp.tile(p, (1, 1, 1, GRID_W))[..., :GRID_W * (2 * GRID_W - 1)]
    m = m.reshape(nh, NA_WIN_R, NA_WIN_R, GRID_W, 2 * GRID_W - 1)
    tab = m[..., GRID_W - 1:].transpose(0, 1, 3, 2, 4)
    tab = jnp.where(inwin[None, None, :, None, :], tab, NEG_BIG)
    return tab.reshape(nh, NA_WIN_R, GRID_W, NA_WIN_R * GRID_W)


def _heads_out(a, b, t, h):
    return a.reshape(b, t, h, -1).transpose(0, 2, 1, 3)


def _peer_block(oa, ob, oc, x, mod, tiles_out, w_out, ln_g, ln_b, wqt, keys, u_tab, vt_tab):
    x1, h2t = _out_proj(oa, ob, oc, x, mod, w_out, ln_g[0], ln_b[0], tiles_out)
    s2, e2, tau, q1 = _peer_score(h2t, wqt, keys)
    yt = _peer_dense(h2t, u_tab, vt_tab, s2, e2, tau, q1)
    return _peer_ln(yt, x1, mod, ln_g[1], ln_b[1], tiles_out)


def kernel(x_prompt, x_sample, cache_na_k, cache_na_v, cache_df_k1, cache_df_k2, cache_df_v, c, c_ctx,
           w_mod, b_mod, w_in, na_rpb, sg_ln_g, sg_ln_b, sg_w, sg_b, df_lambda, df_subln_g, w_out,
           pk_wq, pk_keys, pk_u, pk_v, ln_g, ln_b):
    bp, tp, d = x_prompt.shape
    bs, ts, _ = x_sample.shape
    n_p, n_s = bp * tp, bs * ts

    cond8 = jnp.zeros((8, d), F32).at[0].set(c_ctx).at[1:1 + bs].set(c)
    mods = _modulation(cond8, w_mod, b_mod).reshape(DEPTH, 8, 6, d)

    cos_t, sin_t = _rope_tables(ts)
    cache_dk = _pair_lanes(cache_df_k1, cache_df_k2)

    xp = x_prompt.reshape(n_p, d)
    xs = x_sample.reshape(n_s, d)
    st = [[] for _ in range(5)]
    for l in range(DEPTH):
        lam_init = 0.8 - 0.6 * math.exp(-0.3 * l)
        w_in_l = _permute_in_proj(w_in[l])
        w_out_l = w_out[l].astype(BF16)
        wqt = pk_wq[l].T.astype(BF16)
        keys = pk_keys[l].reshape(2 * PK_HEADS, PK_NKEYS, PK_QDIM // 2).astype(BF16)
        u_tab = pk_u[l].astype(BF16)
        vt_tab = pk_v[l].T.astype(BF16)
        ws = sg_w[l].astype(BF16)
        bs_col = sg_b[l].reshape(SG_GROUPS, SG_CHUNK, 1)
        bias = _na_bias_table(na_rpb[l])
        mod_p = mods[l, 0:1]
        mod_s = mods[l, 1:1 + bs]

        proj = _in_proj(xp, mod_p, w_in_l, n_p // TM_IN)
        proj3 = proj.reshape(bp, tp, IN_WIDTH)
        oa, oc = _ctx_attn(proj3, df_lambda[l], df_subln_g[l], lam_init)
        ob = _spatial_gate(proj3, sg_ln_g[l], sg_ln_b[l], ws, bs_col)
        st[0].append(_heads_out(proj[:, OFF_KA:OFF_KA + NA_WIDTH], bp, tp, NA_HEADS))
        st[1].append(_heads_out(proj[:, OFF_VA:OFF_VA + NA_WIDTH], bp, tp, NA_HEADS))
        k1, k2 = _unpair_lanes(_heads_out(proj[:, OFF_DK:OFF_DV], bp, tp, DF_HEADS))
        st[2].append(k1)
        st[3].append(k2)
        st[4].append(_heads_out(proj[:, OFF_DV:], bp, tp, DF_HEADS))
        xp = _peer_block(oa.reshape(n_p, NA_WIDTH), ob.reshape(n_p, SG_WIDTH), oc.reshape(n_p, DF_WIDTH),
                         xp, mod_p, n_p // TM_OUT, w_out_l, ln_g[l], ln_b[l], wqt, keys, u_tab, vt_tab)

        proj = _in_proj(xs, mod_s, w_in_l, ts // TM_IN)
        proj3 = proj.reshape(bs, ts, IN_WIDTH)
        oa = _lat_na(proj3, cache_na_k, cache_na_v, bias, l)
        ob = _spatial_gate(proj3, sg_ln_g[l], sg_ln_b[l], ws, bs_col)
        oc = _lat_df(proj3, cache_dk, cache_df_v, cos_t, sin_t, df_lambda[l], df_subln_g[l], lam_init, l)
        xs = _peer_block(oa.reshape(n_s, NA_WIDTH), ob.reshape(n_s, SG_WIDTH), oc.reshape(n_s, DF_WIDTH),
                         xs, mod_s, ts // TM_OUT, w_out_l, ln_g[l], ln_b[l], wqt, keys, u_tab, vt_tab)

    outs = [jnp.stack(s, 1) for s in st]
    return (xp.reshape(bp, tp, d), xs.reshape(bs, ts, d), *outs)
```

```python
import functools
import math

import numpy as np
import jax
import jax.numpy as jnp
from jax import lax
from jax.experimental import pallas as pl
from jax.experimental.pallas import tpu as pltpu

F32 = jnp.float32
BF16 = jnp.bfloat16

D_MODEL = 2048
DEPTH = 4
GRID_W = 64
NA_HEADS = 8
NA_DIM = 128
NA_WIN_R = 8
NA_WIN_C = 16
SG_GROUPS = 4
SG_DIM = 128
SG_CHUNK = 128
DF_HEADS = 4
DF_QK = 64
DF_V = 128
ROPE_THETA = 10000.0
NA_WIDTH = NA_HEADS * NA_DIM
SG_WIDTH = SG_GROUPS * SG_DIM
DF_WIDTH = DF_HEADS * DF_V
MIX_WIDTH = NA_WIDTH + SG_WIDTH + DF_WIDTH
IN_WIDTH = 3 * NA_WIDTH + 2 * SG_WIDTH + 4 * DF_HEADS * DF_QK + DF_WIDTH
PK_HEADS = 8
PK_QDIM = 256
PK_NKEYS = 128
PK_TOPK = 16
PK_EXPERTS = PK_NKEYS * PK_NKEYS
DN_ALPHA = (2 * DEPTH) ** 0.25
LN_EPS = 1e-5
RMS_EPS = 1e-6

OFF_QA = 0
OFF_KA = NA_WIDTH
OFF_VA = 2 * NA_WIDTH
OFF_U = 3 * NA_WIDTH
OFF_V = OFF_U + SG_WIDTH
OFF_DQ = OFF_V + SG_WIDTH
OFF_DK = OFF_DQ + 2 * DF_HEADS * DF_QK
OFF_DV = OFF_DK + 2 * DF_HEADS * DF_QK

LANES = 128
MIB = 1024 * 1024

TM_IN = 512
TN_IN = IN_WIDTH // 4
TM_OUT = 512
TM_SCORE = 256
TM_PEER = 512
TE_PEER = 512
RJ_PEER = 16
TQ_DF = 256
TN_MOD = 1024
NEG_BIG = -1e30


def _cp(sem, vmem_mib):
    return pltpu.CompilerParams(dimension_semantics=sem, vmem_limit_bytes=vmem_mib * MIB)


def _dot(a, b):
    return jnp.dot(a, b, preferred_element_type=F32)


def _dot_nt(a, b):
    return lax.dot_general(a, b, (((1,), (1,)), ((), ())), preferred_element_type=F32)


def _layernorm(z, g, b):
    mu = jnp.mean(z, -1, keepdims=True)
    d = z - mu
    var = jnp.mean(d * d, -1, keepdims=True)
    return d * lax.rsqrt(var + LN_EPS) * g + b


def _softmax_rows(s):
    m = jnp.max(s, -1, keepdims=True)
    e = jnp.exp(s - m)
    return e * (1.0 / jnp.sum(e, -1, keepdims=True))


def _diff_lambda(lam_ref, lam_init):
    lf = lam_ref[...]
    a = jnp.sum(lf[0:1, :] * lf[1:2, :], axis=1, keepdims=True)
    b = jnp.sum(lf[2:3, :] * lf[3:4, :], axis=1, keepdims=True)
    return jnp.exp(a) - jnp.exp(b) + lam_init


def _map1_mask():
    lane = lax.broadcasted_iota(jnp.int32, (1, LANES), 1)
    return (lane // (DF_QK // 2)) % 2 == 0


def _subln(o, g, lam_init):
    return o * lax.rsqrt(jnp.mean(o * o, -1, keepdims=True) + RMS_EPS) * g * (1.0 - lam_init)


def _mod_kernel(c_ref, w_ref, b_ref, o_ref):
    c = c_ref[...]
    a = (c * jax.nn.sigmoid(c)).astype(BF16)
    o_ref[0] = _dot(a, w_ref[0].astype(BF16)) + b_ref[0]


def _modulation(cond8, w_mod, b_mod):
    depth, d, n6 = w_mod.shape
    return pl.pallas_call(
        _mod_kernel,
        grid=(depth, n6 // TN_MOD),
        in_specs=[pl.BlockSpec((8, d), lambda l, j: (0, 0)),
                  pl.BlockSpec((1, d, TN_MOD), lambda l, j: (l, 0, j)),
                  pl.BlockSpec((1, 1, TN_MOD), lambda l, j: (l, 0, j))],
        out_specs=pl.BlockSpec((1, 8, TN_MOD), lambda l, j: (l, 0, j)),
        out_shape=jax.ShapeDtypeStruct((depth, 8, n6), F32),
        compiler_params=_cp(("parallel", "parallel"), 32),
        name="modulation",
    )(cond8, w_mod, b_mod.reshape(depth, 1, n6))


def _in_proj_kernel(x_ref, mod_ref, w_ref, o_ref, h_ref):
    @pl.when(pl.program_id(1) == 0)
    def _():
        sh = mod_ref[0, 0:1, :]
        sc = mod_ref[0, 1:2, :]
        h_ref[...] = (x_ref[...] * (1 + sc) + sh).astype(BF16)

    o_ref[...] = _dot(h_ref[...], w_ref[...])


def _in_proj(x, mod, w, tiles_per_mod):
    n, d = x.shape
    nw = w.shape[1]
    return pl.pallas_call(
        _in_proj_kernel,
        grid=(n // TM_IN, nw // TN_IN),
        in_specs=[pl.BlockSpec((TM_IN, d), lambda i, j: (i, 0)),
                  pl.BlockSpec((1, 6, d), lambda i, j: (i // tiles_per_mod, 0, 0)),
                  pl.BlockSpec((d, TN_IN), lambda i, j: (0, j))],
        out_specs=pl.BlockSpec((TM_IN, TN_IN), lambda i, j: (i, j)),
        out_shape=jax.ShapeDtypeStruct((n, nw), F32),
        scratch_shapes=[pltpu.VMEM((TM_IN, d), BF16)],
        compiler_params=_cp(("parallel", "arbitrary"), 40),
        name="in_proj",
    )(x, mod, w)


def _ctx_attn_kernel(p_ref, lam_ref, g_ref, oa_ref, oc_ref, *, lam_init):
    scale = NA_DIM ** -0.5
    for h in range(NA_HEADS):
        sl = slice(h * NA_DIM, (h + 1) * NA_DIM)
        q = p_ref[0, :, OFF_QA + h * NA_DIM:OFF_QA + (h + 1) * NA_DIM].astype(BF16)
        k = p_ref[0, :, OFF_KA + h * NA_DIM:OFF_KA + (h + 1) * NA_DIM].astype(BF16)
        v = p_ref[0, :, OFF_VA + h * NA_DIM:OFF_VA + (h + 1) * NA_DIM].astype(BF16)
        p = _softmax_rows(_dot_nt(q, k) * scale)
        oa_ref[0, :, sl] = _dot(p.astype(BF16), v).astype(BF16)
    lam = _diff_lambda(lam_ref, lam_init)
    m1 = _map1_mask()
    dscale = DF_QK ** -0.5
    for h in range(DF_HEADS):
        sl = slice(h * DF_V, (h + 1) * DF_V)
        q = p_ref[0, :, OFF_DQ + h * LANES:OFF_DQ + (h + 1) * LANES] * dscale
        k = p_ref[0, :, OFF_DK + h * LANES:OFF_DK + (h + 1) * LANES].astype(BF16)
        v = p_ref[0, :, OFF_DV + h * DF_V:OFF_DV + (h + 1) * DF_V].astype(BF16)
        qa = jnp.where(m1, q, 0.0).astype(BF16)
        qb = jnp.where(m1, 0.0, q).astype(BF16)
        p = _softmax_rows(_dot_nt(qa, k)) - lam * _softmax_rows(_dot_nt(qb, k))
        o = _dot(p.astype(BF16), v)
        oc_ref[0, :, sl] = _subln(o, g_ref[...], lam_init).astype(BF16)


def _ctx_attn(proj3, lam_p, subln_g, lam_init):
    b, t, nw = proj3.shape
    return pl.pallas_call(
        functools.partial(_ctx_attn_kernel, lam_init=lam_init),
        grid=(b,),
        in_specs=[pl.BlockSpec((1, t, nw), lambda i: (i, 0, 0)),
                  pl.BlockSpec((4, DF_QK), lambda i: (0, 0)),
                  pl.BlockSpec((1, DF_V), lambda i: (0, 0))],
        out_specs=[pl.BlockSpec((1, t, NA_WIDTH), lambda i: (i, 0, 0)),
                   pl.BlockSpec((1, t, DF_WIDTH), lambda i: (i, 0, 0))],
        out_shape=[jax.ShapeDtypeStruct((b, t, NA_WIDTH), BF16),
                   jax.ShapeDtypeStruct((b, t, DF_WIDTH), BF16)],
        compiler_params=_cp(("parallel",), 32),
        name="ctx_attn",
    )(proj3, lam_p, subln_g.reshape(1, DF_V))


def _lat_na_kernel(q_ref, k_ref, v_ref, kc_ref, vc_ref, bias_ref, o_ref, kb_ref, vb_ref, *, rows):
    kb_ref[...] = k_ref[0].astype(BF16)
    vb_ref[...] = v_ref[0].astype(BF16)
    kc = kc_ref[0, 0, 0].astype(BF16)
    vc = vc_ref[0, 0, 0].astype(BF16)
    scale = NA_DIM ** -0.5
    band = NA_WIN_R * GRID_W

    def body(r, carry):
        r0 = jnp.clip(r - NA_WIN_R // 2, 0, rows - NA_WIN_R)
        q = q_ref[0, pl.ds(pl.multiple_of(r * GRID_W, GRID_W), GRID_W), :].astype(BF16)
        k0 = pl.multiple_of(r0 * GRID_W, GRID_W)
        kw = kb_ref[pl.ds(k0, band), :]
        vw = vb_ref[pl.ds(k0, band), :]
        s_loc = _dot_nt(q, kw) * scale + bias_ref[0, r - r0]
        s_ctx = _dot_nt(q, kc) * scale
        m = jnp.maximum(jnp.max(s_loc, -1, keepdims=True), jnp.max(s_ctx, -1, keepdims=True))
        e_loc = jnp.exp(s_loc - m)
        e_ctx = jnp.exp(s_ctx - m)
        inv = 1.0 / (jnp.sum(e_loc, -1, keepdims=True) + jnp.sum(e_ctx, -1, keepdims=True))
        o = _dot((e_loc * inv).astype(BF16), vw) + _dot((e_ctx * inv).astype(BF16), vc)
        o_ref[0, pl.ds(pl.multiple_of(r * GRID_W, GRID_W), GRID_W), :] = o.astype(BF16)
        return carry

    lax.fori_loop(0, rows, body, 0)


def _lat_na(proj3, cache_k, cache_v, bias, layer):
    b, t, _ = proj3.shape
    past = cache_k.shape[3]
    rows = t // GRID_W
    qb, kb, vb = OFF_QA // NA_DIM, OFF_KA // NA_DIM, OFF_VA // NA_DIM
    return pl.pallas_call(
        functools.partial(_lat_na_kernel, rows=rows),
        grid=(b, NA_HEADS),
        in_specs=[pl.BlockSpec((1, t, NA_DIM), lambda i, h: (i, 0, qb + h)),
                  pl.BlockSpec((1, t, NA_DIM), lambda i, h: (i, 0, kb + h)),
                  pl.BlockSpec((1, t, NA_DIM), lambda i, h: (i, 0, vb + h)),
                  pl.BlockSpec((1, 1, 1, past, NA_DIM), lambda i, h: (i, layer, h, 0, 0)),
                  pl.BlockSpec((1, 1, 1, past, NA_DIM), lambda i, h: (i, layer, h, 0, 0)),
                  pl.BlockSpec((1, NA_WIN_R, GRID_W, NA_WIN_R * GRID_W), lambda i, h: (h, 0, 0, 0))],
        out_specs=pl.BlockSpec((1, t, NA_DIM), lambda i, h: (i, 0, h)),
        out_shape=jax.ShapeDtypeStruct((b, t, NA_WIDTH), BF16),
        scratch_shapes=[pltpu.VMEM((t, NA_DIM), BF16), pltpu.VMEM((t, NA_DIM), BF16)],
        compiler_params=_cp(("parallel", "parallel"), 40),
        name="lat_na",
    )(proj3, proj3, proj3, cache_k, cache_v, bias)


def _lat_df_kernel(q_ref, k_ref, v_ref, kc_ref, vc_ref, cos_ref, sin_ref, lam_ref, g_ref, o_ref,
                   kall_ref, vall_ref, *, t, lam_init):
    qi = pl.program_id(2)

    @pl.when(qi == 0)
    def _():
        k = k_ref[0]
        kall_ref[0:t, :] = (k * cos_ref[...] + pltpu.roll(k, LANES // 2, 1) * sin_ref[...]).astype(BF16)
        kall_ref[t:, :] = kc_ref[0, 0, 0].astype(BF16)
        vall_ref[0:t, :] = v_ref[0].astype(BF16)
        vall_ref[t:, :] = vc_ref[0, 0, 0].astype(BF16)

    t0 = pl.multiple_of(qi * TQ_DF, TQ_DF)
    q = q_ref[0]
    q = q * cos_ref[pl.ds(t0, TQ_DF), :] + pltpu.roll(q, LANES // 2, 1) * sin_ref[pl.ds(t0, TQ_DF), :]
    q = q * DF_QK ** -0.5
    m1 = _map1_mask()
    qa = jnp.where(m1, q, 0.0).astype(BF16)
    qb = jnp.where(m1, 0.0, q).astype(BF16)
    lam = _diff_lambda(lam_ref, lam_init)
    kall = kall_ref[...]
    p = _softmax_rows(_dot_nt(qa, kall)) - lam * _softmax_rows(_dot_nt(qb, kall))
    o = _dot(p.astype(BF16), vall_ref[...])
    o_ref[0] = _subln(o, g_ref[...], lam_init).astype(BF16)


def _lat_df(proj3, cache_kp, cache_v, cos_t, sin_t, lam_p, subln_g, lam_init, layer):
    b, t, _ = proj3.shape
    past = cache_v.shape[3]
    qb, kb, vb = OFF_DQ // LANES, OFF_DK // LANES, OFF_DV // LANES
    return pl.pallas_call(
        functools.partial(_lat_df_kernel, t=t, lam_init=lam_init),
        grid=(b, DF_HEADS, t // TQ_DF),
        in_specs=[pl.BlockSpec((1, TQ_DF, LANES), lambda i, h, j: (i, j, qb + h)),
                  pl.BlockSpec((1, t, LANES), lambda i, h, j: (i, 0, kb + h)),
                  pl.BlockSpec((1, t, LANES), lambda i, h, j: (i, 0, vb + h)),
                  pl.BlockSpec((1, 1, 1, past, LANES), lambda i, h, j: (i, layer, h, 0, 0)),
                  pl.BlockSpec((1, 1, 1, past, DF_V), lambda i, h, j: (i, layer, h, 0, 0)),
                  pl.BlockSpec((t, LANES), lambda i, h, j: (0, 0)),
                  pl.BlockSpec((t, LANES), lambda i, h, j: (0, 0)),
                  pl.BlockSpec((4, DF_QK), lambda i, h, j: (0, 0)),
                  pl.BlockSpec((1, DF_V), lambda i, h, j: (0, 0))],
        out_specs=pl.BlockSpec((1, TQ_DF, DF_V), lambda i, h, j: (i, j, h)),
        out_shape=jax.ShapeDtypeStruct((b, t, DF_WIDTH), BF16),
        scratch_shapes=[pltpu.VMEM((t + past, LANES), BF16), pltpu.VMEM((t + past, DF_V), BF16)],
        compiler_params=_cp(("parallel", "parallel", "arbitrary"), 48),
        name="lat_df",
    )(proj3, proj3, proj3, cache_kp, cache_v, cos_t, sin_t, lam_p, subln_g.reshape(1, DF_V))


def _sg_kernel(u_ref, v_ref, g_ref, b_ref, ws_ref, bs_ref, o_ref, *, nch):
    for c in range(nch):
        rs = slice(c * SG_CHUNK, (c + 1) * SG_CHUNK)
        vn = _layernorm(v_ref[0, rs, :], g_ref[...], b_ref[...]).astype(BF16)
        for g in range(SG_GROUPS):
            cs = slice(g * SG_DIM, (g + 1) * SG_DIM)
            mixed = _dot(ws_ref[g], vn[:, cs]) + bs_ref[g]
            o_ref[0, rs, cs] = (u_ref[0, rs, cs] * mixed).astype(BF16)


def _spatial_gate(proj3, ln_g, ln_b, ws, bs):
    b, t, _ = proj3.shape
    nch = min(4, t // SG_CHUNK)
    tt = nch * SG_CHUNK
    ub, vb = OFF_U // SG_WIDTH, OFF_V // SG_WIDTH
    return pl.pallas_call(
        functools.partial(_sg_kernel, nch=nch),
        grid=(b, t // tt),
        in_specs=[pl.BlockSpec((1, tt, SG_WIDTH), lambda i, j: (i, j, ub)),
                  pl.BlockSpec((1, tt, SG_WIDTH), lambda i, j: (i, j, vb)),
                  pl.BlockSpec((1, SG_WIDTH), lambda i, j: (0, 0)),
                  pl.BlockSpec((1, SG_WIDTH), lambda i, j: (0, 0)),
                  pl.BlockSpec((SG_GROUPS, SG_CHUNK, SG_CHUNK), lambda i, j: (0, 0, 0)),
                  pl.BlockSpec((SG_GROUPS, SG_CHUNK, 1), lambda i, j: (0, 0, 0))],
        out_specs=pl.BlockSpec((1, tt, SG_WIDTH), lambda i, j: (i, j, 0)),
        out_shape=jax.ShapeDtypeStruct((b, t, SG_WIDTH), BF16),
        compiler_params=_cp(("parallel", "parallel"), 16),
        name="spatial_gate",
    )(proj3, proj3, ln_g.reshape(1, SG_WIDTH), ln_b.reshape(1, SG_WIDTH), ws, bs)


def _out_proj_kernel(oa_ref, ob_ref, oc_ref, x_ref, mod_ref, w_ref, lg_ref, lb_ref, x1_ref, h2t_ref):
    y = (_dot(oa_ref[...], w_ref[0:NA_WIDTH, :])
         + _dot(ob_ref[...], w_ref[NA_WIDTH:NA_WIDTH + SG_WIDTH, :])
         + _dot(oc_ref[...], w_ref[NA_WIDTH + SG_WIDTH:, :]))
    g1 = mod_ref[0, 2:3, :]
    sh2 = mod_ref[0, 3:4, :]
    sc2 = mod_ref[0, 4:5, :]
    x1 = _layernorm(DN_ALPHA * x_ref[...] + g1 * y, lg_ref[...], lb_ref[...])
    x1_ref[...] = x1
    h2t_ref[...] = (x1 * (1 + sc2) + sh2).T.astype(BF16)


def _out_proj(oa, ob, oc, x, mod, w, ln_g, ln_b, tiles_per_mod):
    n, d = x.shape
    return pl.pallas_call(
        _out_proj_kernel,
        grid=(n // TM_OUT,),
        in_specs=[pl.BlockSpec((TM_OUT, NA_WIDTH), lambda i: (i, 0)),
                  pl.BlockSpec((TM_OUT, SG_WIDTH), lambda i: (i, 0)),
                  pl.BlockSpec((TM_OUT, DF_WIDTH), lambda i: (i, 0)),
                  pl.BlockSpec((TM_OUT, d), lambda i: (i, 0)),
                  pl.BlockSpec((1, 6, d), lambda i: (i // tiles_per_mod, 0, 0)),
                  pl.BlockSpec((MIX_WIDTH, d), lambda i: (0, 0)),
                  pl.BlockSpec((1, d), lambda i: (0, 0)),
                  pl.BlockSpec((1, d), lambda i: (0, 0))],
        out_specs=[pl.BlockSpec((TM_OUT, d), lambda i: (i, 0)),
                   pl.BlockSpec((d, TM_OUT), lambda i: (0, i))],
        out_shape=[jax.ShapeDtypeStruct((n, d), F32),
                   jax.ShapeDtypeStruct((d, n), BF16)],
        compiler_params=_cp(("parallel",), 48),
        name="out_proj",
    )(oa, ob, oc, x, mod, w, ln_g.reshape(1, d), ln_b.reshape(1, d))


_PK_PAIRS = [(a, b) for a in range(PK_TOPK) for b in range(PK_TOPK) if (a + 1) * (b + 1) <= PK_TOPK]


def _merge_exchange_network(n):
    t = int(math.ceil(math.log2(n)))
    p = 2 ** (t - 1)
    pairs = []
    while p > 0:
        q, r, d = 2 ** (t - 1), 0, p
        while d > 0:
            pairs.extend((i, i + d) for i in range(n - d) if i & p == r)
            d, q, r = q - p, q // 2, p
        p //= 2
    return pairs


def _bitonic_merge_network(n):
    pairs, d = [], n // 2
    while d >= 1:
        pairs.extend((i, i + d) for i in range(n) if i & d == 0)
        d //= 2
    return pairs


_SORT16 = _merge_exchange_network(PK_TOPK)
_MERGE16 = _bitonic_merge_network(PK_TOPK)


def _compare_exchange(xs, network):
    xs = list(xs)
    for i, j in network:
        xs[i], xs[j] = jnp.maximum(xs[i], xs[j]), jnp.minimum(xs[i], xs[j])
    return xs


def _peer_score_kernel(h2t_ref, wqt_ref, keys_ref, s2_out, e2_out, tau_out, q1_out,
                       qt_ref, s_ref, tt_ref, thr_ref, z_ref):
    tm = h2t_ref.shape[1]
    qt_ref[...] = _dot(wqt_ref[...], h2t_ref[...]).astype(BF16)
    neg_inf = -jnp.inf
    sub = 8

    for c in range(2 * PK_HEADS):
        s = _dot(keys_ref[c], qt_ref[c * PK_NKEYS:(c + 1) * PK_NKEYS, :])
        s_ref[c] = s
        head, half = c // 2, c % 2
        xs = _compare_exchange([s[r * sub:(r + 1) * sub, :] for r in range(PK_TOPK)], _SORT16)
        shift = 1
        while shift < sub:
            ys = [pltpu.roll(xs[PK_TOPK - 1 - r], shift, 0) for r in range(PK_TOPK)]
            xs = _compare_exchange([jnp.maximum(x, y) for x, y in zip(xs, ys)], _MERGE16)
            shift *= 2
        for r in range(PK_TOPK):
            tt_ref[half, r, head:head + 1, :] = xs[r][0:1, :]

    def per_lane_chunk(ch, carry):
        l0 = pl.multiple_of(ch * LANES, LANES)
        t1 = [tt_ref[0, a, :, pl.ds(l0, LANES)] for a in range(PK_TOPK)]
        t2 = [tt_ref[1, b, :, pl.ds(l0, LANES)] for b in range(PK_TOPK)]
        cands = [t1[a] + t2[b] for a, b in _PK_PAIRS]
        cur = list(cands)
        for _ in range(PK_TOPK - 1):
            m = functools.reduce(jnp.maximum, cur)
            found = jnp.zeros(m.shape, jnp.bool_)
            nxt = []
            for cnd in cur:
                is_m = cnd == m
                nxt.append(jnp.where(jnp.logical_and(is_m, jnp.logical_not(found)), neg_inf, cnd))
                found = jnp.logical_or(found, is_m)
            cur = nxt
        thr = functools.reduce(jnp.maximum, cur)
        e1 = [jnp.exp(t1[a] - t1[0]) for a in range(PK_TOPK)]
        e2 = [jnp.exp(t2[b] - t2[0]) for b in range(PK_TOPK)]
        z = jnp.zeros(thr.shape, F32)
        for (a, b), cnd in zip(_PK_PAIRS, cands):
            z = z + jnp.where(cnd >= thr, e1[a] * e2[b], 0.0)
        thr_ref[:, pl.ds(l0, LANES)] = thr
        z_ref[:, pl.ds(l0, LANES)] = z
        return carry

    lax.fori_loop(0, tm // LANES, per_lane_chunk, 0)

    for h in range(PK_HEADS):
        s1 = s_ref[2 * h]
        s2 = s_ref[2 * h + 1]
        thr = thr_ref[h:h + 1, :]
        tau = jnp.full(s1.shape, jnp.inf, F32)
        for b in range(PK_TOPK):
            t2b = tt_ref[1, b, h:h + 1, :]
            tau = jnp.minimum(tau, jnp.where(s1 + t2b >= thr, t2b, jnp.inf))
        tau_out[h] = tau
        q1_out[h] = jnp.exp(s1 - tt_ref[0, 0, h:h + 1, :]) * (0.5 / z_ref[h:h + 1, :])
        e2_out[h] = jnp.exp(s2 - tt_ref[1, 0, h:h + 1, :])
        s2_out[h] = s2


def _peer_score(h2t, wqt, keys):
    d, n = h2t.shape
    spec = pl.BlockSpec((PK_HEADS, PK_NKEYS, TM_SCORE), lambda i: (0, 0, i))
    shape = jax.ShapeDtypeStruct((PK_HEADS, PK_NKEYS, n), F32)
    return pl.pallas_call(
        _peer_score_kernel,
        grid=(n // TM_SCORE,),
        in_specs=[pl.BlockSpec((d, TM_SCORE), lambda i: (0, i)),
                  pl.BlockSpec(wqt.shape, lambda i: (0, 0)),
                  pl.BlockSpec(keys.shape, lambda i: (0, 0, 0))],
        out_specs=[spec, spec, spec, spec],
        out_shape=[shape, shape, shape, shape],
        scratch_shapes=[pltpu.VMEM((PK_HEADS * PK_QDIM, TM_SCORE), BF16),
                        pltpu.VMEM((2 * PK_HEADS, PK_NKEYS, TM_SCORE), F32),
                        pltpu.VMEM((2, PK_TOPK, PK_HEADS, TM_SCORE), F32),
                        pltpu.VMEM((PK_HEADS, TM_SCORE), F32),
                        pltpu.VMEM((PK_HEADS, TM_SCORE), F32)],
        compiler_params=_cp(("parallel",), 48),
        name="peer_score",
    )(h2t, wqt, keys)


def _peer_gate_tile_loop(a_ref, w_ref, s2_ref, e2_ref, tau_ref, q1_ref, i0):
    sqrt_half = math.sqrt(0.5)
    nj = PK_NKEYS // RJ_PEER
    tm = a_ref.shape[1]

    def chunk(t, carry):
        ii = t // nj
        jc = t % nj
        j0 = pl.multiple_of(jc * RJ_PEER, RJ_PEER)
        row0 = pl.multiple_of(ii * PK_NKEYS + jc * RJ_PEER, RJ_PEER)
        g = jnp.zeros((RJ_PEER, tm), F32)
        for h in range(PK_HEADS):
            tau = tau_ref[h, pl.ds(i0 + ii, 1), :]
            q1 = q1_ref[h, pl.ds(i0 + ii, 1), :]
            g = g + jnp.where(s2_ref[h, pl.ds(j0, RJ_PEER), :] >= tau, e2_ref[h, pl.ds(j0, RJ_PEER), :], 0.0) * q1
        a = a_ref[pl.ds(row0, RJ_PEER), :]
        w_ref[pl.ds(row0, RJ_PEER), :] = (g * (a * (1.0 + lax.erf(a * sqrt_half)))).astype(BF16)
        return carry

    lax.fori_loop(0, (TE_PEER // PK_NKEYS) * nj, chunk, 0, unroll=2)


def _peer_dense_loop_kernel(h2t_ref, u_ref, vt_ref, s2_ref, e2_ref, taup_ref, q1p_ref, tauc_ref, q1c_ref, o_ref,
                            acc_ref, a0_ref, a1_ref, w0_ref, w1_ref):
    g = pl.program_id(1)
    ni = TE_PEER // PK_NKEYS

    @pl.when(g == 0)
    def _():
        acc_ref[...] = jnp.zeros_like(acc_ref)
        a1_ref[...] = jnp.zeros_like(a1_ref)
        w0_ref[...] = jnp.zeros_like(w0_ref)
        w1_ref[...] = jnp.zeros_like(w1_ref)

    a0_ref[...] = _dot(u_ref[0:TE_PEER, :], h2t_ref[...])
    _peer_gate_tile_loop(a1_ref, w1_ref, s2_ref, e2_ref, taup_ref, q1p_ref, ni)
    acc_ref[...] += _dot(vt_ref[:, 0:TE_PEER], w0_ref[...])
    a1_ref[...] = _dot(u_ref[TE_PEER:, :], h2t_ref[...])
    _peer_gate_tile_loop(a0_ref, w0_ref, s2_ref, e2_ref, tauc_ref, q1c_ref, 0)
    acc_ref[...] += _dot(vt_ref[:, TE_PEER:], w1_ref[...])

    @pl.when(g == pl.num_programs(1) - 1)
    def _():
        o_ref[...] = acc_ref[...]


def _peer_dense_flat_kernel(h2t_ref, u_ref, vt_ref, s2_ref, e2_ref, tau_ref, q1_ref, o_ref,
                            acc_ref, a_ref, w_ref):
    e = pl.program_id(1)

    @pl.when(e == 0)
    def _():
        acc_ref[...] = jnp.zeros_like(acc_ref)

    a_ref[...] = _dot(u_ref[...], h2t_ref[...])
    _peer_gate_tile(a_ref, w_ref, s2_ref, e2_ref, tau_ref, q1_ref, e * (TE_PEER // PK_NKEYS))
    acc_ref[...] += _dot(vt_ref[...], w_ref[...])

    @pl.when(e == pl.num_programs(1) - 1)
    def _():
        o_ref[...] = acc_ref[...]


def _peer_dense_flat(h2t, u_tab, vt_tab, s2, e2, tau, q1):
    d, n = h2t.shape
    ne = u_tab.shape[0]
    gspec = pl.BlockSpec((PK_HEADS, PK_NKEYS, TM_PEER), lambda i, e: (0, 0, i))
    return pl.pallas_call(
        _peer_dense_flat_kernel,
        grid=(n // TM_PEER, ne // TE_PEER),
        in_specs=[pl.BlockSpec((d, TM_PEER), lambda i, e: (0, i)),
                  pl.BlockSpec((TE_PEER, d), lambda i, e: (e, 0)),
                  pl.BlockSpec((d, TE_PEER), lambda i, e: (0, e)),
                  gspec, gspec, gspec, gspec],
        out_specs=pl.BlockSpec((d, TM_PEER), lambda i, e: (0, i)),
        out_shape=jax.ShapeDtypeStruct((d, n), F32),
        scratch_shapes=[pltpu.VMEM((d, TM_PEER), F32),
                        pltpu.VMEM((TE_PEER, TM_PEER), F32),
                        pltpu.VMEM((TE_PEER, TM_PEER), BF16)],
        compiler_params=_cp(("parallel", "arbitrary"), 56),
        name="peer_dense_flat",
    )(h2t, u_tab, vt_tab, s2, e2, tau, q1)


def _peer_gate_tile(a_ref, w_ref, s2_ref, e2_ref, tau_ref, q1_ref, i0):
    sqrt_half = math.sqrt(0.5)
    for ii in range(TE_PEER // PK_NKEYS):
        taus = [tau_ref[h, pl.ds(i0 + ii, 1), :] for h in range(PK_HEADS)]
        q1s = [q1_ref[h, pl.ds(i0 + ii, 1), :] for h in range(PK_HEADS)]
        for jc in range(PK_NKEYS // RJ_PEER):
            js = slice(jc * RJ_PEER, (jc + 1) * RJ_PEER)
            rs = slice(ii * PK_NKEYS + jc * RJ_PEER, ii * PK_NKEYS + (jc + 1) * RJ_PEER)
            g = None
            for h in range(PK_HEADS):
                term = jnp.where(s2_ref[h, js, :] >= taus[h], e2_ref[h, js, :], 0.0) * q1s[h]
                g = term if g is None else g + term
            a = a_ref[rs, :]
            w_ref[rs, :] = (g * (a * (1.0 + lax.erf(a * sqrt_half)))).astype(BF16)


def _peer_dense_kernel(h2t_ref, u_ref, vt_ref, s2_ref, e2_ref, taup_ref, q1p_ref, tauc_ref, q1c_ref, o_ref,
                       acc_ref, a0_ref, a1_ref, w0_ref, w1_ref):
    g = pl.program_id(1)
    ni = TE_PEER // PK_NKEYS

    @pl.when(g == 0)
    def _():
        acc_ref[...] = jnp.zeros_like(acc_ref)
        a1_ref[...] = jnp.zeros_like(a1_ref)
        w0_ref[...] = jnp.zeros_like(w0_ref)
        w1_ref[...] = jnp.zeros_like(w1_ref)

    a0_ref[...] = _dot(u_ref[0:TE_PEER, :], h2t_ref[...])
    _peer_gate_tile(a1_ref, w1_ref, s2_ref, e2_ref, taup_ref, q1p_ref, ni)
    acc_ref[...] += _dot(vt_ref[:, 0:TE_PEER], w0_ref[...])
    a1_ref[...] = _dot(u_ref[TE_PEER:, :], h2t_ref[...])
    _peer_gate_tile(a0_ref, w0_ref, s2_ref, e2_ref, tauc_ref, q1c_ref, 0)
    acc_ref[...] += _dot(vt_ref[:, TE_PEER:], w1_ref[...])

    @pl.when(g == pl.num_programs(1) - 1)
    def _():
        o_ref[...] = acc_ref[...]


def _peer_dense(h2t, u_tab, vt_tab, s2, e2, tau, q1):
    d, n = h2t.shape
    ne = u_tab.shape[0]
    te2 = 2 * TE_PEER
    ng = ne // te2
    ri = te2 // PK_NKEYS
    gspec = pl.BlockSpec((PK_HEADS, PK_NKEYS, TM_PEER), lambda i, g: (0, 0, i))
    prev = pl.BlockSpec((PK_HEADS, ri, TM_PEER), lambda i, g: (0, (g + ng - 1) % ng, i))
    cur = pl.BlockSpec((PK_HEADS, ri, TM_PEER), lambda i, g: (0, g % ng, i))
    return pl.pallas_call(
        _peer_dense_loop_kernel,
        grid=(n // TM_PEER, ng + 1),
        in_specs=[pl.BlockSpec((d, TM_PEER), lambda i, g: (0, i)),
                  pl.BlockSpec((te2, d), lambda i, g: (g % ng, 0)),
                  pl.BlockSpec((d, te2), lambda i, g: (0, (g + ng - 1) % ng)),
                  gspec, gspec, prev, prev, cur, cur],
        out_specs=pl.BlockSpec((d, TM_PEER), lambda i, g: (0, i)),
        out_shape=jax.ShapeDtypeStruct((d, n), F32),
        scratch_shapes=[pltpu.VMEM((d, TM_PEER), F32),
                        pltpu.VMEM((TE_PEER, TM_PEER), F32),
                        pltpu.VMEM((TE_PEER, TM_PEER), F32),
                        pltpu.VMEM((TE_PEER, TM_PEER), BF16),
                        pltpu.VMEM((TE_PEER, TM_PEER), BF16)],
        compiler_params=_cp(("parallel", "arbitrary"), 56),
        name="peer_dense",
    )(h2t, u_tab, vt_tab, s2, e2, tau, q1, tau, q1)


def _peer_ln_kernel(yt_ref, x1_ref, mod_ref, lg_ref, lb_ref, o_ref):
    g2 = mod_ref[0, 5:6, :]
    o_ref[...] = _layernorm(DN_ALPHA * x1_ref[...] + g2 * yt_ref[...].T, lg_ref[...], lb_ref[...])


def _peer_ln(yt, x1, mod, ln_g, ln_b, tiles_per_mod):
    n, d = x1.shape
    return pl.pallas_call(
        _peer_ln_kernel,
        grid=(n // TM_OUT,),
        in_specs=[pl.BlockSpec((d, TM_OUT), lambda i: (0, i)),
                  pl.BlockSpec((TM_OUT, d), lambda i: (i, 0)),
                  pl.BlockSpec((1, 6, d), lambda i: (i // tiles_per_mod, 0, 0)),
                  pl.BlockSpec((1, d), lambda i: (0, 0)),
                  pl.BlockSpec((1, d), lambda i: (0, 0))],
        out_specs=pl.BlockSpec((TM_OUT, d), lambda i: (i, 0)),
        out_shape=jax.ShapeDtypeStruct((n, d), F32),
        compiler_params=_cp(("parallel",), 40),
        name="peer_ln",
    )(yt, x1, mod, ln_g.reshape(1, d), ln_b.reshape(1, d))


def _permute_in_proj(w):
    d = w.shape[0]

    def interleave(cols):
        c = cols.reshape(d, 2, DF_HEADS, 2, DF_QK // 2)
        return c.transpose(0, 2, 3, 1, 4).reshape(d, 2 * DF_HEADS * DF_QK)

    return jnp.concatenate([w[:, :OFF_DQ], interleave(w[:, OFF_DQ:OFF_DK]),
                            interleave(w[:, OFF_DK:OFF_DV]), w[:, OFF_DV:]], axis=1).astype(BF16)


def _pair_lanes(a, b):
    half = DF_QK // 2
    return jnp.concatenate([a[..., :half], b[..., :half], a[..., half:], b[..., half:]], -1)


def _unpair_lanes(kk):
    half = DF_QK // 2
    a = jnp.concatenate([kk[..., 0:half], kk[..., 2 * half:3 * half]], -1)
    b = jnp.concatenate([kk[..., half:2 * half], kk[..., 3 * half:]], -1)
    return a, b


def _rope_tables(t):
    tok = jnp.arange(t)
    row = (tok // GRID_W).astype(F32)
    col = (tok % GRID_W).astype(F32)
    n_freq = DF_QK // 4
    inv = 1.0 / (ROPE_THETA ** (jnp.arange(n_freq, dtype=F32) / n_freq))
    ang = jnp.concatenate([row[:, None] * inv, col[:, None] * inv], -1)
    cos, sin = jnp.cos(ang), jnp.sin(ang)
    return jnp.concatenate([cos] * 4, -1), jnp.concatenate([-sin, -sin, sin, sin], -1)


def _na_bias_table(rpb):
    cols = np.arange(GRID_W)
    start = np.clip(cols - NA_WIN_C // 2, 0, GRID_W - NA_WIN_C)
    inwin = (cols[None, :] >= start[:, None]) & (cols[None, :] < start[:, None] + NA_WIN_C)
    nh = rpb.shape[0]
    rows = jnp.stack([rpb[:, NA_WIN_R - 1 - cs:2 * NA_WIN_R - 1 - cs, :] for cs in range(NA_WIN_R)], 1)
    lead = GRID_W - NA_WIN_C
    p = jnp.pad(rows, ((0, 0), (0, 0), (0, 0), (lead, 2 * GRID_W - lead - (2 * NA_WIN_C - 1))))
    m = jnp.tile(p, (1, 1, 1, GRID_W))[..., :GRID_W * (2 * GRID_W - 1)]
    m = m.reshape(nh, NA_WIN_R, NA_WIN_R, GRID_W, 2 * GRID_W - 1)
    tab = m[..., GRID_W - 1:].transpose(0, 1, 3, 2, 4)
    tab = jnp.where(inwin[None, None, :, None, :], tab, NEG_BIG)
    return tab.reshape(nh, NA_WIN_R, GRID_W, NA_WIN_R * GRID_W)


def _heads_out(a, b, t, h):
    return a.reshape(b, t, h, -1).transpose(0, 2, 1, 3)


def _peer_block(oa, ob, oc, x, mod, tiles_out, w_out, ln_g, ln_b, wqt, keys, u_tab, vt_tab):
    x1, h2t = _out_proj(oa, ob, oc, x, mod, w_out, ln_g[0], ln_b[0], tiles_out)
    s2, e2, tau, q1 = _peer_score(h2t, wqt, keys)
    dense = _peer_dense if h2t.shape[1] == 4096 else _peer_dense_flat
    yt = dense(h2t, u_tab, vt_tab, s2, e2, tau, q1)
    return _peer_ln(yt, x1, mod, ln_g[1], ln_b[1], tiles_out)


def kernel(x_prompt, x_sample, cache_na_k, cache_na_v, cache_df_k1, cache_df_k2, cache_df_v, c, c_ctx,
           w_mod, b_mod, w_in, na_rpb, sg_ln_g, sg_ln_b, sg_w, sg_b, df_lambda, df_subln_g, w_out,
           pk_wq, pk_keys, pk_u, pk_v, ln_g, ln_b):
    bp, tp, d = x_prompt.shape
    bs, ts, _ = x_sample.shape
    n_p, n_s = bp * tp, bs * ts

    cond8 = jnp.zeros((8, d), F32).at[0].set(c_ctx).at[1:1 + bs].set(c)
    mods = _modulation(cond8, w_mod, b_mod).reshape(DEPTH, 8, 6, d)

    cos_t, sin_t = _rope_tables(ts)
    cache_dk = _pair_lanes(cache_df_k1, cache_df_k2)

    xp = x_prompt.reshape(n_p, d)
    xs = x_sample.reshape(n_s, d)
    st = [[] for _ in range(5)]
    for l in range(DEPTH):
        lam_init = 0.8 - 0.6 * math.exp(-0.3 * l)
        w_in_l = _permute_in_proj(w_in[l])
        w_out_l = w_out[l].astype(BF16)
        wqt = pk_wq[l].T.astype(BF16)
        keys = pk_keys[l].reshape(2 * PK_HEADS, PK_NKEYS, PK_QDIM // 2).astype(BF16)
        u_tab = pk_u[l].astype(BF16)
        vt_tab = pk_v[l].T.astype(BF16)
        ws = sg_w[l].astype(BF16)
        bs_col = sg_b[l].reshape(SG_GROUPS, SG_CHUNK, 1)
        bias = _na_bias_table(na_rpb[l])
        mod_p = mods[l, 0:1]
        mod_s = mods[l, 1:1 + bs]

        proj = _in_proj(xp, mod_p, w_in_l, n_p // TM_IN)
        proj3 = proj.reshape(bp, tp, IN_WIDTH)
        oa, oc = _ctx_attn(proj3, df_lambda[l], df_subln_g[l], lam_init)
        ob = _spatial_gate(proj3, sg_ln_g[l], sg_ln_b[l], ws, bs_col)
        st[0].append(_heads_out(proj[:, OFF_KA:OFF_KA + NA_WIDTH], bp, tp, NA_HEADS))
        st[1].append(_heads_out(proj[:, OFF_VA:OFF_VA + NA_WIDTH], bp, tp, NA_HEADS))
        k1, k2 = _unpair_lanes(_heads_out(proj[:, OFF_DK:OFF_DV], bp, tp, DF_HEADS))
        st[2].append(k1)
        st[3].append(k2)
        st[4].append(_heads_out(proj[:, OFF_DV:], bp, tp, DF_HEADS))
        xp = _peer_block(oa.reshape(n_p, NA_WIDTH), ob.reshape(n_p, SG_WIDTH), oc.reshape(n_p, DF_WIDTH),
                         xp, mod_p, n_p // TM_OUT, w_out_l, ln_g[l], ln_b[l], wqt, keys, u_tab, vt_tab)

        proj = _in_proj(xs, mod_s, w_in_l, ts // TM_IN)
        proj3 = proj.reshape(bs, ts, IN_WIDTH)
        oa = _lat_na(proj3, cache_na_k, cache_na_v, bias, l)
        ob = _spatial_gate(proj3, sg_ln_g[l], sg_ln_b[l], ws, bs_col)
        oc = _lat_df(proj3, cache_dk, cache_df_v, cos_t, sin_t, df_lambda[l], df_subln_g[l], lam_init, l)
        xs = _peer_block(oa.reshape(n_s, NA_WIDTH), ob.reshape(n_s, SG_WIDTH), oc.reshape(n_s, DF_WIDTH),
                         xs, mod_s, ts // TM_OUT, w_out_l, ln_g[l], ln_b[l], wqt, keys, u_tab, vt_tab)

    outs = [jnp.stack(s, 1) for s in st]
    return (xp.reshape(bp, tp, d), xs.reshape(bs, ts, d), *outs)
```

```python
import functools
import math

import numpy as np
import jax
import jax.numpy as jnp
from jax import lax
from jax.experimental import pallas as pl
from jax.experimental.pallas import tpu as pltpu

F32 = jnp.float32
BF16 = jnp.bfloat16

D_MODEL = 2048
DEPTH = 4
GRID_W = 64
NA_HEADS = 8
NA_DIM = 128
NA_WIN_R = 8
NA_WIN_C = 16
SG_GROUPS = 4
SG_DIM = 128
SG_CHUNK = 128
DF_HEADS = 4
DF_QK = 64
DF_V = 128
ROPE_THETA = 10000.0
NA_WIDTH = NA_HEADS * NA_DIM
SG_WIDTH = SG_GROUPS * SG_DIM
DF_WIDTH = DF_HEADS * DF_V
MIX_WIDTH = NA_WIDTH + SG_WIDTH + DF_WIDTH
IN_WIDTH = 3 * NA_WIDTH + 2 * SG_WIDTH + 4 * DF_HEADS * DF_QK + DF_WIDTH
PK_HEADS = 8
PK_QDIM = 256
PK_NKEYS = 128
PK_TOPK = 16
PK_EXPERTS = PK_NKEYS * PK_NKEYS
DN_ALPHA = (2 * DEPTH) ** 0.25
LN_EPS = 1e-5
RMS_EPS = 1e-6

OFF_QA = 0
OFF_KA = NA_WIDTH
OFF_VA = 2 * NA_WIDTH
OFF_U = 3 * NA_WIDTH
OFF_V = OFF_U + SG_WIDTH
OFF_DQ = OFF_V + SG_WIDTH
OFF_DK = OFF_DQ + 2 * DF_HEADS * DF_QK
OFF_DV = OFF_DK + 2 * DF_HEADS * DF_QK

LANES = 128
MIB = 1024 * 1024

TM_IN = 512
TN_IN = IN_WIDTH // 4
TM_OUT = 512
TM_SCORE = 256
TM_PEER = 512
TE_PEER = 1024
RJ_PEER = 16
TQ_DF = 256
TN_MOD = 1024
NEG_BIG = -1e30


def _cp(sem, vmem_mib):
    return pltpu.CompilerParams(dimension_semantics=sem, vmem_limit_bytes=vmem_mib * MIB)


def _dot(a, b):
    return jnp.dot(a, b, preferred_element_type=F32)


def _dot_nt(a, b):
    return lax.dot_general(a, b, (((1,), (1,)), ((), ())), preferred_element_type=F32)


def _layernorm(z, g, b):
    mu = jnp.mean(z, -1, keepdims=True)
    d = z - mu
    var = jnp.mean(d * d, -1, keepdims=True)
    return d * lax.rsqrt(var + LN_EPS) * g + b


def _softmax_rows(s):
    m = jnp.max(s, -1, keepdims=True)
    e = jnp.exp(s - m)
    return e * (1.0 / jnp.sum(e, -1, keepdims=True))


def _diff_lambda(lam_ref, lam_init):
    lf = lam_ref[...]
    a = jnp.sum(lf[0:1, :] * lf[1:2, :], axis=1, keepdims=True)
    b = jnp.sum(lf[2:3, :] * lf[3:4, :], axis=1, keepdims=True)
    return jnp.exp(a) - jnp.exp(b) + lam_init


def _map1_mask():
    lane = lax.broadcasted_iota(jnp.int32, (1, LANES), 1)
    return (lane // (DF_QK // 2)) % 2 == 0


def _subln(o, g, lam_init):
    return o * lax.rsqrt(jnp.mean(o * o, -1, keepdims=True) + RMS_EPS) * g * (1.0 - lam_init)


def _mod_kernel(c_ref, w_ref, b_ref, o_ref):
    c = c_ref[...]
    a = (c * jax.nn.sigmoid(c)).astype(BF16)
    o_ref[0] = _dot(a, w_ref[0].astype(BF16)) + b_ref[0]


def _modulation(cond8, w_mod, b_mod):
    depth, d, n6 = w_mod.shape
    return pl.pallas_call(
        _mod_kernel,
        grid=(depth, n6 // TN_MOD),
        in_specs=[pl.BlockSpec((8, d), lambda l, j: (0, 0)),
                  pl.BlockSpec((1, d, TN_MOD), lambda l, j: (l, 0, j)),
                  pl.BlockSpec((1, 1, TN_MOD), lambda l, j: (l, 0, j))],
        out_specs=pl.BlockSpec((1, 8, TN_MOD), lambda l, j: (l, 0, j)),
        out_shape=jax.ShapeDtypeStruct((depth, 8, n6), F32),
        compiler_params=_cp(("parallel", "parallel"), 32),
        name="modulation",
    )(cond8, w_mod, b_mod.reshape(depth, 1, n6))


def _in_proj_kernel(x_ref, mod_ref, w_ref, o_ref, h_ref):
    @pl.when(pl.program_id(1) == 0)
    def _():
        sh = mod_ref[0, 0:1, :]
        sc = mod_ref[0, 1:2, :]
        h_ref[...] = (x_ref[...] * (1 + sc) + sh).astype(BF16)

    o_ref[...] = _dot(h_ref[...], w_ref[...])


def _in_proj(x, mod, w, tiles_per_mod):
    n, d = x.shape
    nw = w.shape[1]
    return pl.pallas_call(
        _in_proj_kernel,
        grid=(n // TM_IN, nw // TN_IN),
        in_specs=[pl.BlockSpec((TM_IN, d), lambda i, j: (i, 0)),
                  pl.BlockSpec((1, 6, d), lambda i, j: (i // tiles_per_mod, 0, 0)),
                  pl.BlockSpec((d, TN_IN), lambda i, j: (0, j))],
        out_specs=pl.BlockSpec((TM_IN, TN_IN), lambda i, j: (i, j)),
        out_shape=jax.ShapeDtypeStruct((n, nw), F32),
        scratch_shapes=[pltpu.VMEM((TM_IN, d), BF16)],
        compiler_params=_cp(("parallel", "arbitrary"), 40),
        name="in_proj",
    )(x, mod, w)


def _ctx_attn_kernel(p_ref, lam_ref, g_ref, oa_ref, oc_ref, *, lam_init):
    scale = NA_DIM ** -0.5
    for h in range(NA_HEADS):
        sl = slice(h * NA_DIM, (h + 1) * NA_DIM)
        q = p_ref[0, :, OFF_QA + h * NA_DIM:OFF_QA + (h + 1) * NA_DIM].astype(BF16)
        k = p_ref[0, :, OFF_KA + h * NA_DIM:OFF_KA + (h + 1) * NA_DIM].astype(BF16)
        v = p_ref[0, :, OFF_VA + h * NA_DIM:OFF_VA + (h + 1) * NA_DIM].astype(BF16)
        p = _softmax_rows(_dot_nt(q, k) * scale)
        oa_ref[0, :, sl] = _dot(p.astype(BF16), v).astype(BF16)
    lam = _diff_lambda(lam_ref, lam_init)
    m1 = _map1_mask()
    dscale = DF_QK ** -0.5
    for h in range(DF_HEADS):
        sl = slice(h * DF_V, (h + 1) * DF_V)
        q = p_ref[0, :, OFF_DQ + h * LANES:OFF_DQ + (h + 1) * LANES] * dscale
        k = p_ref[0, :, OFF_DK + h * LANES:OFF_DK + (h + 1) * LANES].astype(BF16)
        v = p_ref[0, :, OFF_DV + h * DF_V:OFF_DV + (h + 1) * DF_V].astype(BF16)
        qa = jnp.where(m1, q, 0.0).astype(BF16)
        qb = jnp.where(m1, 0.0, q).astype(BF16)
        p = _softmax_rows(_dot_nt(qa, k)) - lam * _softmax_rows(_dot_nt(qb, k))
        o = _dot(p.astype(BF16), v)
        oc_ref[0, :, sl] = _subln(o, g_ref[...], lam_init).astype(BF16)


def _ctx_attn(proj3, lam_p, subln_g, lam_init):
    b, t, nw = proj3.shape
    return pl.pallas_call(
        functools.partial(_ctx_attn_kernel, lam_init=lam_init),
        grid=(b,),
        in_specs=[pl.BlockSpec((1, t, nw), lambda i: (i, 0, 0)),
                  pl.BlockSpec((4, DF_QK), lambda i: (0, 0)),
                  pl.BlockSpec((1, DF_V), lambda i: (0, 0))],
        out_specs=[pl.BlockSpec((1, t, NA_WIDTH), lambda i: (i, 0, 0)),
                   pl.BlockSpec((1, t, DF_WIDTH), lambda i: (i, 0, 0))],
        out_shape=[jax.ShapeDtypeStruct((b, t, NA_WIDTH), BF16),
                   jax.ShapeDtypeStruct((b, t, DF_WIDTH), BF16)],
        compiler_params=_cp(("parallel",), 32),
        name="ctx_attn",
    )(proj3, lam_p, subln_g.reshape(1, DF_V))


def _lat_na_kernel(q_ref, k_ref, v_ref, kc_ref, vc_ref, bias_ref, o_ref, kb_ref, vb_ref, *, rows):
    kb_ref[...] = k_ref[0].astype(BF16)
    vb_ref[...] = v_ref[0].astype(BF16)
    kc = kc_ref[0, 0, 0].astype(BF16)
    vc = vc_ref[0, 0, 0].astype(BF16)
    scale = NA_DIM ** -0.5
    band = NA_WIN_R * GRID_W

    def body(r, carry):
        r0 = jnp.clip(r - NA_WIN_R // 2, 0, rows - NA_WIN_R)
        q = q_ref[0, pl.ds(pl.multiple_of(r * GRID_W, GRID_W), GRID_W), :].astype(BF16)
        k0 = pl.multiple_of(r0 * GRID_W, GRID_W)
        kw = kb_ref[pl.ds(k0, band), :]
        vw = vb_ref[pl.ds(k0, band), :]
        s_loc = _dot_nt(q, kw) * scale + bias_ref[0, r - r0]
        s_ctx = _dot_nt(q, kc) * scale
        m = jnp.maximum(jnp.max(s_loc, -1, keepdims=True), jnp.max(s_ctx, -1, keepdims=True))
        e_loc = jnp.exp(s_loc - m)
        e_ctx = jnp.exp(s_ctx - m)
        inv = 1.0 / (jnp.sum(e_loc, -1, keepdims=True) + jnp.sum(e_ctx, -1, keepdims=True))
        o = _dot((e_loc * inv).astype(BF16), vw) + _dot((e_ctx * inv).astype(BF16), vc)
        o_ref[0, pl.ds(pl.multiple_of(r * GRID_W, GRID_W), GRID_W), :] = o.astype(BF16)
        return carry

    lax.fori_loop(0, rows, body, 0)


def _lat_na(proj3, cache_k, cache_v, bias, layer):
    b, t, _ = proj3.shape
    past = cache_k.shape[3]
    rows = t // GRID_W
    qb, kb, vb = OFF_QA // NA_DIM, OFF_KA // NA_DIM, OFF_VA // NA_DIM
    return pl.pallas_call(
        functools.partial(_lat_na_kernel, rows=rows),
        grid=(b, NA_HEADS),
        in_specs=[pl.BlockSpec((1, t, NA_DIM), lambda i, h: (i, 0, qb + h)),
                  pl.BlockSpec((1, t, NA_DIM), lambda i, h: (i, 0, kb + h)),
                  pl.BlockSpec((1, t, NA_DIM), lambda i, h: (i, 0, vb + h)),
                  pl.BlockSpec((1, 1, 1, past, NA_DIM), lambda i, h: (i, layer, h, 0, 0)),
                  pl.BlockSpec((1, 1, 1, past, NA_DIM), lambda i, h: (i, layer, h, 0, 0)),
                  pl.BlockSpec((1, NA_WIN_R, GRID_W, NA_WIN_R * GRID_W), lambda i, h: (h, 0, 0, 0))],
        out_specs=pl.BlockSpec((1, t, NA_DIM), lambda i, h: (i, 0, h)),
        out_shape=jax.ShapeDtypeStruct((b, t, NA_WIDTH), BF16),
        scratch_shapes=[pltpu.VMEM((t, NA_DIM), BF16), pltpu.VMEM((t, NA_DIM), BF16)],
        compiler_params=_cp(("parallel", "parallel"), 40),
        name="lat_na",
    )(proj3, proj3, proj3, cache_k, cache_v, bias)


def _lat_df_kernel(q_ref, k_ref, v_ref, kc_ref, vc_ref, cos_ref, sin_ref, lam_ref, g_ref, o_ref,
                   kall_ref, vall_ref, *, t, lam_init):
    qi = pl.program_id(2)

    @pl.when(qi == 0)
    def _():
        k = k_ref[0]
        kall_ref[0:t, :] = (k * cos_ref[...] + pltpu.roll(k, LANES // 2, 1) * sin_ref[...]).astype(BF16)
        kall_ref[t:, :] = kc_ref[0, 0, 0].astype(BF16)
        vall_ref[0:t, :] = v_ref[0].astype(BF16)
        vall_ref[t:, :] = vc_ref[0, 0, 0].astype(BF16)

    t0 = pl.multiple_of(qi * TQ_DF, TQ_DF)
    q = q_ref[0]
    q = q * cos_ref[pl.ds(t0, TQ_DF), :] + pltpu.roll(q, LANES // 2, 1) * sin_ref[pl.ds(t0, TQ_DF), :]
    q = q * DF_QK ** -0.5
    m1 = _map1_mask()
    qa = jnp.where(m1, q, 0.0).astype(BF16)
    qb = jnp.where(m1, 0.0, q).astype(BF16)
    lam = _diff_lambda(lam_ref, lam_init)
    kall = kall_ref[...]
    p = _softmax_rows(_dot_nt(qa, kall)) - lam * _softmax_rows(_dot_nt(qb, kall))
    o = _dot(p.astype(BF16), vall_ref[...])
    o_ref[0] = _subln(o, g_ref[...], lam_init).astype(BF16)


def _lat_df(proj3, cache_kp, cache_v, cos_t, sin_t, lam_p, subln_g, lam_init, layer):
    b, t, _ = proj3.shape
    past = cache_v.shape[3]
    qb, kb, vb = OFF_DQ // LANES, OFF_DK // LANES, OFF_DV // LANES
    return pl.pallas_call(
        functools.partial(_lat_df_kernel, t=t, lam_init=lam_init),
        grid=(b, DF_HEADS, t // TQ_DF),
        in_specs=[pl.BlockSpec((1, TQ_DF, LANES), lambda i, h, j: (i, j, qb + h)),
                  pl.BlockSpec((1, t, LANES), lambda i, h, j: (i, 0, kb + h)),
                  pl.BlockSpec((1, t, LANES), lambda i, h, j: (i, 0, vb + h)),
                  pl.BlockSpec((1, 1, 1, past, LANES), lambda i, h, j: (i, layer, h, 0, 0)),
                  pl.BlockSpec((1, 1, 1, past, DF_V), lambda i, h, j: (i, layer, h, 0, 0)),
                  pl.BlockSpec((t, LANES), lambda i, h, j: (0, 0)),
                  pl.BlockSpec((t, LANES), lambda i, h, j: (0, 0)),
                  pl.BlockSpec((4, DF_QK), lambda i, h, j: (0, 0)),
                  pl.BlockSpec((1, DF_V), lambda i, h, j: (0, 0))],
        out_specs=pl.BlockSpec((1, TQ_DF, DF_V), lambda i, h, j: (i, j, h)),
        out_shape=jax.ShapeDtypeStruct((b, t, DF_WIDTH), BF16),
        scratch_shapes=[pltpu.VMEM((t + past, LANES), BF16), pltpu.VMEM((t + past, DF_V), BF16)],
        compiler_params=_cp(("parallel", "parallel", "arbitrary"), 48),
        name="lat_df",
    )(proj3, proj3, proj3, cache_kp, cache_v, cos_t, sin_t, lam_p, subln_g.reshape(1, DF_V))


def _sg_kernel(u_ref, v_ref, g_ref, b_ref, ws_ref, bs_ref, o_ref, *, nch):
    for c in range(nch):
        rs = slice(c * SG_CHUNK, (c + 1) * SG_CHUNK)
        vn = _layernorm(v_ref[0, rs, :], g_ref[...], b_ref[...]).astype(BF16)
        for g in range(SG_GROUPS):
            cs = slice(g * SG_DIM, (g + 1) * SG_DIM)
            mixed = _dot(ws_ref[g], vn[:, cs]) + bs_ref[g]
            o_ref[0, rs, cs] = (u_ref[0, rs, cs] * mixed).astype(BF16)


def _spatial_gate(proj3, ln_g, ln_b, ws, bs):
    b, t, _ = proj3.shape
    nch = min(4, t // SG_CHUNK)
    tt = nch * SG_CHUNK
    ub, vb = OFF_U // SG_WIDTH, OFF_V // SG_WIDTH
    return pl.pallas_call(
        functools.partial(_sg_kernel, nch=nch),
        grid=(b, t // tt),
        in_specs=[pl.BlockSpec((1, tt, SG_WIDTH), lambda i, j: (i, j, ub)),
                  pl.BlockSpec((1, tt, SG_WIDTH), lambda i, j: (i, j, vb)),
                  pl.BlockSpec((1, SG_WIDTH), lambda i, j: (0, 0)),
                  pl.BlockSpec((1, SG_WIDTH), lambda i, j: (0, 0)),
                  pl.BlockSpec((SG_GROUPS, SG_CHUNK, SG_CHUNK), lambda i, j: (0, 0, 0)),
                  pl.BlockSpec((SG_GROUPS, SG_CHUNK, 1), lambda i, j: (0, 0, 0))],
        out_specs=pl.BlockSpec((1, tt, SG_WIDTH), lambda i, j: (i, j, 0)),
        out_shape=jax.ShapeDtypeStruct((b, t, SG_WIDTH), BF16),
        compiler_params=_cp(("parallel", "parallel"), 16),
        name="spatial_gate",
    )(proj3, proj3, ln_g.reshape(1, SG_WIDTH), ln_b.reshape(1, SG_WIDTH), ws, bs)


def _out_proj_kernel(oa_ref, ob_ref, oc_ref, x_ref, mod_ref, w_ref, lg_ref, lb_ref, x1_ref, h2t_ref):
    y = (_dot(oa_ref[...], w_ref[0:NA_WIDTH, :])
         + _dot(ob_ref[...], w_ref[NA_WIDTH:NA_WIDTH + SG_WIDTH, :])
         + _dot(oc_ref[...], w_ref[NA_WIDTH + SG_WIDTH:, :]))
    g1 = mod_ref[0, 2:3, :]
    sh2 = mod_ref[0, 3:4, :]
    sc2 = mod_ref[0, 4:5, :]
    x1 = _layernorm(DN_ALPHA * x_ref[...] + g1 * y, lg_ref[...], lb_ref[...])
    x1_ref[...] = x1
    h2t_ref[...] = (x1 * (1 + sc2) + sh2).T.astype(BF16)


def _out_proj(oa, ob, oc, x, mod, w, ln_g, ln_b, tiles_per_mod):
    n, d = x.shape
    return pl.pallas_call(
        _out_proj_kernel,
        grid=(n // TM_OUT,),
        in_specs=[pl.BlockSpec((TM_OUT, NA_WIDTH), lambda i: (i, 0)),
                  pl.BlockSpec((TM_OUT, SG_WIDTH), lambda i: (i, 0)),
                  pl.BlockSpec((TM_OUT, DF_WIDTH), lambda i: (i, 0)),
                  pl.BlockSpec((TM_OUT, d), lambda i: (i, 0)),
                  pl.BlockSpec((1, 6, d), lambda i: (i // tiles_per_mod, 0, 0)),
                  pl.BlockSpec((MIX_WIDTH, d), lambda i: (0, 0)),
                  pl.BlockSpec((1, d), lambda i: (0, 0)),
                  pl.BlockSpec((1, d), lambda i: (0, 0))],
        out_specs=[pl.BlockSpec((TM_OUT, d), lambda i: (i, 0)),
                   pl.BlockSpec((d, TM_OUT), lambda i: (0, i))],
        out_shape=[jax.ShapeDtypeStruct((n, d), F32),
                   jax.ShapeDtypeStruct((d, n), BF16)],
        compiler_params=_cp(("parallel",), 48),
        name="out_proj",
    )(oa, ob, oc, x, mod, w, ln_g.reshape(1, d), ln_b.reshape(1, d))


_PK_PAIRS = [(a, b) for a in range(PK_TOPK) for b in range(PK_TOPK) if (a + 1) * (b + 1) <= PK_TOPK]


def _merge_exchange_network(n):
    t = int(math.ceil(math.log2(n)))
    p = 2 ** (t - 1)
    pairs = []
    while p > 0:
        q, r, d = 2 ** (t - 1), 0, p
        while d > 0:
            pairs.extend((i, i + d) for i in range(n - d) if i & p == r)
            d, q, r = q - p, q // 2, p
        p //= 2
    return pairs


def _bitonic_merge_network(n):
    pairs, d = [], n // 2
    while d >= 1:
        pairs.extend((i, i + d) for i in range(n) if i & d == 0)
        d //= 2
    return pairs


_SORT16 = _merge_exchange_network(PK_TOPK)
_MERGE16 = _bitonic_merge_network(PK_TOPK)


def _compare_exchange(xs, network):
    xs = list(xs)
    for i, j in network:
        xs[i], xs[j] = jnp.maximum(xs[i], xs[j]), jnp.minimum(xs[i], xs[j])
    return xs


def _peer_score_kernel(h2t_ref, wqt_ref, keys_ref, s2_out, e2_out, tau_out, q1_out,
                       qt_ref, s_ref, tt_ref, thr_ref, z_ref):
    tm = h2t_ref.shape[1]
    qt_ref[...] = _dot(wqt_ref[...], h2t_ref[...]).astype(BF16)
    neg_inf = -jnp.inf
    sub = 8

    for c in range(2 * PK_HEADS):
        s = _dot(keys_ref[c], qt_ref[c * PK_NKEYS:(c + 1) * PK_NKEYS, :])
        s_ref[c] = s
        head, half = c // 2, c % 2
        xs = _compare_exchange([s[r * sub:(r + 1) * sub, :] for r in range(PK_TOPK)], _SORT16)
        shift = 1
        while shift < sub:
            ys = [pltpu.roll(xs[PK_TOPK - 1 - r], shift, 0) for r in range(PK_TOPK)]
            xs = _compare_exchange([jnp.maximum(x, y) for x, y in zip(xs, ys)], _MERGE16)
            shift *= 2
        for r in range(PK_TOPK):
            tt_ref[half, r, head:head + 1, :] = xs[r][0:1, :]

    def per_lane_chunk(ch, carry):
        l0 = pl.multiple_of(ch * LANES, LANES)
        t1 = [tt_ref[0, a, :, pl.ds(l0, LANES)] for a in range(PK_TOPK)]
        t2 = [tt_ref[1, b, :, pl.ds(l0, LANES)] for b in range(PK_TOPK)]
        cands = [t1[a] + t2[b] for a, b in _PK_PAIRS]
        cur = list(cands)
        for _ in range(PK_TOPK - 1):
            m = functools.reduce(jnp.maximum, cur)
            found = jnp.zeros(m.shape, jnp.bool_)
            nxt = []
            for cnd in cur:
                is_m = cnd == m
                nxt.append(jnp.where(jnp.logical_and(is_m, jnp.logical_not(found)), neg_inf, cnd))
                found = jnp.logical_or(found, is_m)
            cur = nxt
        thr = functools.reduce(jnp.maximum, cur)
        e1 = [jnp.exp(t1[a] - t1[0]) for a in range(PK_TOPK)]
        e2 = [jnp.exp(t2[b] - t2[0]) for b in range(PK_TOPK)]
        z = jnp.zeros(thr.shape, F32)
        for (a, b), cnd in zip(_PK_PAIRS, cands):
            z = z + jnp.where(cnd >= thr, e1[a] * e2[b], 0.0)
        thr_ref[:, pl.ds(l0, LANES)] = thr
        z_ref[:, pl.ds(l0, LANES)] = z
        return carry

    lax.fori_loop(0, tm // LANES, per_lane_chunk, 0)

    for h in range(PK_HEADS):
        s1 = s_ref[2 * h]
        s2 = s_ref[2 * h + 1]
        thr = thr_ref[h:h + 1, :]
        tau = jnp.full(s1.shape, jnp.inf, F32)
        for b in range(PK_TOPK):
            t2b = tt_ref[1, b, h:h + 1, :]
            tau = jnp.minimum(tau, jnp.where(s1 + t2b >= thr, t2b, jnp.inf))
        tau_out[h] = tau
        q1_out[h] = jnp.exp(s1 - tt_ref[0, 0, h:h + 1, :]) * (0.5 / z_ref[h:h + 1, :])
        e2_out[h] = jnp.exp(s2 - tt_ref[1, 0, h:h + 1, :])
        s2_out[h] = s2


def _peer_score(h2t, wqt, keys):
    d, n = h2t.shape
    spec = pl.BlockSpec((PK_HEADS, PK_NKEYS, TM_SCORE), lambda i: (0, 0, i))
    shape = jax.ShapeDtypeStruct((PK_HEADS, PK_NKEYS, n), F32)
    return pl.pallas_call(
        _peer_score_kernel,
        grid=(n // TM_SCORE,),
        in_specs=[pl.BlockSpec((d, TM_SCORE), lambda i: (0, i)),
                  pl.BlockSpec(wqt.shape, lambda i: (0, 0)),
                  pl.BlockSpec(keys.shape, lambda i: (0, 0, 0))],
        out_specs=[spec, spec, spec, spec],
        out_shape=[shape, shape, shape, shape],
        scratch_shapes=[pltpu.VMEM((PK_HEADS * PK_QDIM, TM_SCORE), BF16),
                        pltpu.VMEM((2 * PK_HEADS, PK_NKEYS, TM_SCORE), F32),
                        pltpu.VMEM((2, PK_TOPK, PK_HEADS, TM_SCORE), F32),
                        pltpu.VMEM((PK_HEADS, TM_SCORE), F32),
                        pltpu.VMEM((PK_HEADS, TM_SCORE), F32)],
        compiler_params=_cp(("parallel",), 48),
        name="peer_score",
    )(h2t, wqt, keys)


def _peer_dense_kernel(h2t_ref, u_ref, vt_ref, s2_ref, e2_ref, tau_ref, q1_ref, o_ref, acc_ref, a_ref, w_ref):
    sqrt_half = math.sqrt(0.5)

    @pl.when(pl.program_id(1) == 0)
    def _():
        acc_ref[...] = jnp.zeros_like(acc_ref)

    a_ref[...] = _dot(u_ref[...], h2t_ref[...])
    for ii in range(TE_PEER // PK_NKEYS):
        taus = [tau_ref[h, ii:ii + 1, :] for h in range(PK_HEADS)]
        q1s = [q1_ref[h, ii:ii + 1, :] for h in range(PK_HEADS)]
        for jc in range(PK_NKEYS // RJ_PEER):
            js = slice(jc * RJ_PEER, (jc + 1) * RJ_PEER)
            rs = slice(ii * PK_NKEYS + jc * RJ_PEER, ii * PK_NKEYS + (jc + 1) * RJ_PEER)
            g = None
            for h in range(PK_HEADS):
                term = jnp.where(s2_ref[h, js, :] >= taus[h], e2_ref[h, js, :], 0.0) * q1s[h]
                g = term if g is None else g + term
            a = a_ref[rs, :]
            w_ref[rs, :] = (g * (a * (1.0 + lax.erf(a * sqrt_half)))).astype(BF16)
    acc_ref[...] += _dot(vt_ref[...], w_ref[...])

    @pl.when(pl.program_id(1) == pl.num_programs(1) - 1)
    def _():
        o_ref[...] = acc_ref[...]


def _peer_dense(h2t, u_tab, vt_tab, s2, e2, tau, q1):
    d, n = h2t.shape
    ne = u_tab.shape[0]
    ri = TE_PEER // PK_NKEYS
    gspec = pl.BlockSpec((PK_HEADS, PK_NKEYS, TM_PEER), lambda i, e: (0, 0, i))
    rspec = pl.BlockSpec((PK_HEADS, ri, TM_PEER), lambda i, e: (0, e, i))
    return pl.pallas_call(
        _peer_dense_kernel,
        grid=(n // TM_PEER, ne // TE_PEER),
        in_specs=[pl.BlockSpec((d, TM_PEER), lambda i, e: (0, i)),
                  pl.BlockSpec((TE_PEER, d), lambda i, e: (e, 0)),
                  pl.BlockSpec((d, TE_PEER), lambda i, e: (0, e)),
                  gspec, gspec, rspec, rspec],
        out_specs=pl.BlockSpec((d, TM_PEER), lambda i, e: (0, i)),
        out_shape=jax.ShapeDtypeStruct((d, n), F32),
        scratch_shapes=[pltpu.VMEM((d, TM_PEER), F32),
                        pltpu.VMEM((TE_PEER, TM_PEER), F32),
                        pltpu.VMEM((TE_PEER, TM_PEER), BF16)],
        compiler_params=_cp(("parallel", "arbitrary"), 56),
        name="peer_dense",
    )(h2t, u_tab, vt_tab, s2, e2, tau, q1)


def _peer_ln_kernel(yt_ref, x1_ref, mod_ref, lg_ref, lb_ref, o_ref):
    g2 = mod_ref[0, 5:6, :]
    o_ref[...] = _layernorm(DN_ALPHA * x1_ref[...] + g2 * yt_ref[...].T, lg_ref[...], lb_ref[...])


def _peer_ln(yt, x1, mod, ln_g, ln_b, tiles_per_mod):
    n, d = x1.shape
    return pl.pallas_call(
        _peer_ln_kernel,
        grid=(n // TM_OUT,),
        in_specs=[pl.BlockSpec((d, TM_OUT), lambda i: (0, i)),
                  pl.BlockSpec((TM_OUT, d), lambda i: (i, 0)),
                  pl.BlockSpec((1, 6, d), lambda i: (i // tiles_per_mod, 0, 0)),
                  pl.BlockSpec((1, d), lambda i: (0, 0)),
                  pl.BlockSpec((1, d), lambda i: (0, 0))],
        out_specs=pl.BlockSpec((TM_OUT, d), lambda i: (i, 0)),
        out_shape=jax.ShapeDtypeStruct((n, d), F32),
        compiler_params=_cp(("parallel",), 40),
        name="peer_ln",
    )(yt, x1, mod, ln_g.reshape(1, d), ln_b.reshape(1, d))


def _permute_in_proj(w):
    d = w.shape[0]

    def interleave(cols):
        c = cols.reshape(d, 2, DF_HEADS, 2, DF_QK // 2)
        return c.transpose(0, 2, 3, 1, 4).reshape(d, 2 * DF_HEADS * DF_QK)

    return jnp.concatenate([w[:, :OFF_DQ], interleave(w[:, OFF_DQ:OFF_DK]),
                            interleave(w[:, OFF_DK:OFF_DV]), w[:, OFF_DV:]], axis=1).astype(BF16)


def _pair_lanes(a, b):
    half = DF_QK // 2
    return jnp.concatenate([a[..., :half], b[..., :half], a[..., half:], b[..., half:]], -1)


def _unpair_lanes(kk):
    half = DF_QK // 2
    a = jnp.concatenate([kk[..., 0:half], kk[..., 2 * half:3 * half]], -1)
    b = jnp.concatenate([kk[..., half:2 * half], kk[..., 3 * half:]], -1)
    return a, b


def _rope_tables(t):
    tok = jnp.arange(t)
    row = (tok // GRID_W).astype(F32)
    col = (tok % GRID_W).astype(F32)
    n_freq = DF_QK // 4
    inv = 1.0 / (ROPE_THETA ** (jnp.arange(n_freq, dtype=F32) / n_freq))
    ang = jnp.concatenate([row[:, None] * inv, col[:, None] * inv], -1)
    cos, sin = jnp.cos(ang), jnp.sin(ang)
    return jnp.concatenate([cos] * 4, -1), jnp.concatenate([-sin, -sin, sin, sin], -1)


def _na_bias_table(rpb):
    cols = np.arange(GRID_W)
    start = np.clip(cols - NA_WIN_C // 2, 0, GRID_W - NA_WIN_C)
    inwin = (cols[None, :] >= start[:, None]) & (cols[None, :] < start[:, None] + NA_WIN_C)
    nh = rpb.shape[0]
    rows = jnp.stack([rpb[:, NA_WIN_R - 1 - cs:2 * NA_WIN_R - 1 - cs, :] for cs in range(NA_WIN_R)], 1)
    lead = GRID_W - NA_WIN_C
    p = jnp.pad(rows, ((0, 0), (0, 0), (0, 0), (lead, 2 * GRID_W - lead - (2 * NA_WIN_C - 1))))
    m = jnp.tile(p, (1, 1, 1, GRID_W))[..., :GRID_W * (2 * GRID_W - 1)]
    m = m.reshape(nh, NA_WIN_R, NA_WIN_R, GRID_W, 2 * GRID_W - 1)
    tab = m[..., GRID_W - 1:].transpose(0, 1, 3, 2, 4)
    tab = jnp.where(inwin[None, None, :, None, :], tab, NEG_BIG)
    return tab.reshape(nh, NA_WIN_R, GRID_W, NA_WIN_R * GRID_W)


def _heads_out(a, b, t, h):
    return a.reshape(b, t, h, -1).transpose(0, 2, 1, 3)


def _peer_block(oa, ob, oc, x, mod, tiles_out, w_out, ln_g, ln_b, wqt, keys, u_tab, vt_tab):
    x1, h2t = _out_proj(oa, ob, oc, x, mod, w_out, ln_g[0], ln_b[0], tiles_out)
    s2, e2, tau, q1 = _peer_score(h2t, wqt, keys)
    yt = _peer_dense(h2t, u_tab, vt_tab, s2, e2, tau, q1)
    return _peer_ln(yt, x1, mod, ln_g[1], ln_b[1], tiles_out)


def kernel(x_prompt, x_sample, cache_na_k, cache_na_v, cache_df_k1, cache_df_k2, cache_df_v, c, c_ctx,
           w_mod, b_mod, w_in, na_rpb, sg_ln_g, sg_ln_b, sg_w, sg_b, df_lambda, df_subln_g, w_out,
           pk_wq, pk_keys, pk_u, pk_v, ln_g, ln_b):
    bp, tp, d = x_prompt.shape
    bs, ts, _ = x_sample.shape
    n_p, n_s = bp * tp, bs * ts

    cond8 = jnp.zeros((8, d), F32).at[0].set(c_ctx).at[1:1 + bs].set(c)
    mods = _modulation(cond8, w_mod, b_mod).reshape(DEPTH, 8, 6, d)

    cos_t, sin_t = _rope_tables(ts)
    cache_dk = _pair_lanes(cache_df_k1, cache_df_k2)

    xp = x_prompt.reshape(n_p, d)
    xs = x_sample.reshape(n_s, d)
    st = [[] for _ in range(5)]
    for l in range(DEPTH):
        lam_init = 0.8 - 0.6 * math.exp(-0.3 * l)
        w_in_l = _permute_in_proj(w_in[l])
        w_out_l = w_out[l].astype(BF16)
        wqt = pk_wq[l].T.astype(BF16)
        keys = pk_keys[l].reshape(2 * PK_HEADS, PK_NKEYS, PK_QDIM // 2).astype(BF16)
        u_tab = pk_u[l].astype(BF16)
        vt_tab = pk_v[l].T.astype(BF16)
        ws = sg_w[l].astype(BF16)
        bs_col = sg_b[l].reshape(SG_GROUPS, SG_CHUNK, 1)
        bias = _na_bias_table(na_rpb[l])
        mod_p = mods[l, 0:1]
        mod_s = mods[l, 1:1 + bs]

        proj = _in_proj(xp, mod_p, w_in_l, n_p // TM_IN)
        proj3 = proj.reshape(bp, tp, IN_WIDTH)
        oa, oc = _ctx_attn(proj3, df_lambda[l], df_subln_g[l], lam_init)
        ob = _spatial_gate(proj3, sg_ln_g[l], sg_ln_b[l], ws, bs_col)
        st[0].append(_heads_out(proj[:, OFF_KA:OFF_KA + NA_WIDTH], bp, tp, NA_HEADS))
        st[1].append(_heads_out(proj[:, OFF_VA:OFF_VA + NA_WIDTH], bp, tp, NA_HEADS))
        k1, k2 = _unpair_lanes(_heads_out(proj[:, OFF_DK:OFF_DV], bp, tp, DF_HEADS))
        st[2].append(k1)
        st[3].append(k2)
        st[4].append(_heads_out(proj[:, OFF_DV:], bp, tp, DF_HEADS))
        xp = _peer_block(oa.reshape(n_p, NA_WIDTH), ob.reshape(n_p, SG_WIDTH), oc.reshape(n_p, DF_WIDTH),
                         xp, mod_p, n_p // TM_OUT, w_out_l, ln_g[l], ln_b[l], wqt, keys, u_tab, vt_tab)

        proj = _in_proj(xs, mod_s, w_in_l, ts // TM_IN)
        proj3 = proj.reshape(bs, ts, IN_WIDTH)
        oa = _lat_na(proj3, cache_na_k, cache_na_v, bias, l)
        ob = _spatial_gate(proj3, sg_ln_g[l], sg_ln_b[l], ws, bs_col)
        oc = _lat_df(proj3, cache_dk, cache_df_v, cos_t, sin_t, df_lambda[l], df_subln_g[l], lam_init, l)
        xs = _peer_block(oa.reshape(n_s, NA_WIDTH), ob.reshape(n_s, SG_WIDTH), oc.reshape(n_s, DF_WIDTH),
                         xs, mod_s, ts // TM_OUT, w_out_l, ln_g[l], ln_b[l], wqt, keys, u_tab, vt_tab)

    outs = [jnp.stack(s, 1) for s in st]
    return (xp.reshape(bp, tp, d), xs.reshape(bs, ts, d), *outs)
```

```python
import functools
import math

import numpy as np
import jax
import jax.numpy as jnp
from jax import lax
from jax.experimental import pallas as pl
from jax.experimental.pallas import tpu as pltpu

F32 = jnp.float32
BF16 = jnp.bfloat16

D_MODEL = 2048
DEPTH = 4
GRID_W = 64
NA_HEADS = 8
NA_DIM = 128
NA_WIN_R = 8
NA_WIN_C = 16
SG_GROUPS = 4
SG_DIM = 128
SG_CHUNK = 128
DF_HEADS = 4
DF_QK = 64
DF_V = 128
ROPE_THETA = 10000.0
NA_WIDTH = NA_HEADS * NA_DIM
SG_WIDTH = SG_GROUPS * SG_DIM
DF_WIDTH = DF_HEADS * DF_V
MIX_WIDTH = NA_WIDTH + SG_WIDTH + DF_WIDTH
IN_WIDTH = 3 * NA_WIDTH + 2 * SG_WIDTH + 4 * DF_HEADS * DF_QK + DF_WIDTH
PK_HEADS = 8
PK_QDIM = 256
PK_NKEYS = 128
PK_TOPK = 16
PK_EXPERTS = PK_NKEYS * PK_NKEYS
DN_ALPHA = (2 * DEPTH) ** 0.25
LN_EPS = 1e-5
RMS_EPS = 1e-6

OFF_QA = 0
OFF_KA = NA_WIDTH
OFF_VA = 2 * NA_WIDTH
OFF_U = 3 * NA_WIDTH
OFF_V = OFF_U + SG_WIDTH
OFF_DQ = OFF_V + SG_WIDTH
OFF_DK = OFF_DQ + 2 * DF_HEADS * DF_QK
OFF_DV = OFF_DK + 2 * DF_HEADS * DF_QK

LANES = 128
MIB = 1024 * 1024

TM_IN = 512
TN_IN = IN_WIDTH // 4
TM_OUT = 512
TM_SCORE = 256
TM_PEER = 512
TE_PEER = 1024
RJ_PEER = 16
TQ_DF = 256
TN_MOD = 1024
NEG_BIG = -1e30


def _cp(sem, vmem_mib):
    return pltpu.CompilerParams(dimension_semantics=sem, vmem_limit_bytes=vmem_mib * MIB)


def _dot(a, b):
    return jnp.dot(a, b, preferred_element_type=F32)


def _dot_nt(a, b):
    return lax.dot_general(a, b, (((1,), (1,)), ((), ())), preferred_element_type=F32)


def _layernorm(z, g, b):
    mu = jnp.mean(z, -1, keepdims=True)
    d = z - mu
    var = jnp.mean(d * d, -1, keepdims=True)
    return d * lax.rsqrt(var + LN_EPS) * g + b


def _softmax_rows(s):
    m = jnp.max(s, -1, keepdims=True)
    e = jnp.exp(s - m)
    return e * (1.0 / jnp.sum(e, -1, keepdims=True))


def _diff_lambda(lam_ref, lam_init):
    lf = lam_ref[...]
    a = jnp.sum(lf[0:1, :] * lf[1:2, :], axis=1, keepdims=True)
    b = jnp.sum(lf[2:3, :] * lf[3:4, :], axis=1, keepdims=True)
    return jnp.exp(a) - jnp.exp(b) + lam_init


def _map1_mask():
    lane = lax.broadcasted_iota(jnp.int32, (1, LANES), 1)
    return (lane // (DF_QK // 2)) % 2 == 0


def _subln(o, g, lam_init):
    return o * lax.rsqrt(jnp.mean(o * o, -1, keepdims=True) + RMS_EPS) * g * (1.0 - lam_init)


def _mod_kernel(c_ref, w_ref, b_ref, o_ref):
    c = c_ref[...]
    a = (c * jax.nn.sigmoid(c)).astype(BF16)
    o_ref[0] = _dot(a, w_ref[0].astype(BF16)) + b_ref[0]


def _modulation(cond8, w_mod, b_mod):
    depth, d, n6 = w_mod.shape
    return pl.pallas_call(
        _mod_kernel,
        grid=(depth, n6 // TN_MOD),
        in_specs=[pl.BlockSpec((8, d), lambda l, j: (0, 0)),
                  pl.BlockSpec((1, d, TN_MOD), lambda l, j: (l, 0, j)),
                  pl.BlockSpec((1, 1, TN_MOD), lambda l, j: (l, 0, j))],
        out_specs=pl.BlockSpec((1, 8, TN_MOD), lambda l, j: (l, 0, j)),
        out_shape=jax.ShapeDtypeStruct((depth, 8, n6), F32),
        compiler_params=_cp(("parallel", "parallel"), 32),
        name="modulation",
    )(cond8, w_mod, b_mod.reshape(depth, 1, n6))


def _in_proj_kernel(x_ref, mod_ref, w_ref, o_ref, h_ref):
    @pl.when(pl.program_id(1) == 0)
    def _():
        sh = mod_ref[0, 0:1, :]
        sc = mod_ref[0, 1:2, :]
        h_ref[...] = (x_ref[...] * (1 + sc) + sh).astype(BF16)

    o_ref[...] = _dot(h_ref[...], w_ref[...])


def _in_proj(x, mod, w, tiles_per_mod):
    n, d = x.shape
    nw = w.shape[1]
    return pl.pallas_call(
        _in_proj_kernel,
        grid=(n // TM_IN, nw // TN_IN),
        in_specs=[pl.BlockSpec((TM_IN, d), lambda i, j: (i, 0)),
                  pl.BlockSpec((1, 6, d), lambda i, j: (i // tiles_per_mod, 0, 0)),
                  pl.BlockSpec((d, TN_IN), lambda i, j: (0, j))],
        out_specs=pl.BlockSpec((TM_IN, TN_IN), lambda i, j: (i, j)),
        out_shape=jax.ShapeDtypeStruct((n, nw), F32),
        scratch_shapes=[pltpu.VMEM((TM_IN, d), BF16)],
        compiler_params=_cp(("parallel", "arbitrary"), 40),
        name="in_proj",
    )(x, mod, w)


def _ctx_attn_kernel(p_ref, lam_ref, g_ref, oa_ref, oc_ref, *, lam_init):
    scale = NA_DIM ** -0.5
    for h in range(NA_HEADS):
        sl = slice(h * NA_DIM, (h + 1) * NA_DIM)
        q = p_ref[0, :, OFF_QA + h * NA_DIM:OFF_QA + (h + 1) * NA_DIM].astype(BF16)
        k = p_ref[0, :, OFF_KA + h * NA_DIM:OFF_KA + (h + 1) * NA_DIM].astype(BF16)
        v = p_ref[0, :, OFF_VA + h * NA_DIM:OFF_VA + (h + 1) * NA_DIM].astype(BF16)
        p = _softmax_rows(_dot_nt(q, k) * scale)
        oa_ref[0, :, sl] = _dot(p.astype(BF16), v).astype(BF16)
    lam = _diff_lambda(lam_ref, lam_init)
    m1 = _map1_mask()
    dscale = DF_QK ** -0.5
    for h in range(DF_HEADS):
        sl = slice(h * DF_V, (h + 1) * DF_V)
        q = p_ref[0, :, OFF_DQ + h * LANES:OFF_DQ + (h + 1) * LANES] * dscale
        k = p_ref[0, :, OFF_DK + h * LANES:OFF_DK + (h + 1) * LANES].astype(BF16)
        v = p_ref[0, :, OFF_DV + h * DF_V:OFF_DV + (h + 1) * DF_V].astype(BF16)
        qa = jnp.where(m1, q, 0.0).astype(BF16)
        qb = jnp.where(m1, 0.0, q).astype(BF16)
        p = _softmax_rows(_dot_nt(qa, k)) - lam * _softmax_rows(_dot_nt(qb, k))
        o = _dot(p.astype(BF16), v)
        oc_ref[0, :, sl] = _subln(o, g_ref[...], lam_init).astype(BF16)


def _ctx_attn(proj3, lam_p, subln_g, lam_init):
    b, t, nw = proj3.shape
    return pl.pallas_call(
        functools.partial(_ctx_attn_kernel, lam_init=lam_init),
        grid=(b,),
        in_specs=[pl.BlockSpec((1, t, nw), lambda i: (i, 0, 0)),
                  pl.BlockSpec((4, DF_QK), lambda i: (0, 0)),
                  pl.BlockSpec((1, DF_V), lambda i: (0, 0))],
        out_specs=[pl.BlockSpec((1, t, NA_WIDTH), lambda i: (i, 0, 0)),
                   pl.BlockSpec((1, t, DF_WIDTH), lambda i: (i, 0, 0))],
        out_shape=[jax.ShapeDtypeStruct((b, t, NA_WIDTH), BF16),
                   jax.ShapeDtypeStruct((b, t, DF_WIDTH), BF16)],
        compiler_params=_cp(("parallel",), 32),
        name="ctx_attn",
    )(proj3, lam_p, subln_g.reshape(1, DF_V))


NA_QROWS = 4
NA_BAND = NA_WIN_R + NA_QROWS


def _na_group_layout(rows):
    groups = []
    for g in range(rows // NA_QROWS):
        bs = int(np.clip(g * NA_QROWS - NA_WIN_R // 2, 0, rows - NA_BAND))
        pat = []
        for a in range(NA_QROWS):
            r = g * NA_QROWS + a
            r0 = int(np.clip(r - NA_WIN_R // 2, 0, rows - NA_WIN_R))
            pat.append((r0 - bs, r - r0))
        groups.append((bs, tuple(pat)))
    cases = [groups[0][1], groups[1][1], groups[-1][1]]
    assert all(p == cases[1] for _, p in groups[1:-1]) and rows >= 2 * NA_BAND
    assert all(0 <= off and off + NA_WIN_R <= NA_BAND for c in cases for off, _ in c)
    return cases


def _lat_na_kernel(q_ref, k_ref, v_ref, kc_ref, vc_ref, bias_ref, o_ref, kb_ref, vb_ref, *, rows):
    kb_ref[...] = k_ref[0].astype(BF16)
    vb_ref[...] = v_ref[0].astype(BF16)
    kc = kc_ref[0, 0, 0].astype(BF16)
    vc = vc_ref[0, 0, 0].astype(BF16)
    scale = NA_DIM ** -0.5
    nq = NA_QROWS * GRID_W
    band = NA_BAND * GRID_W
    ngroups = rows // NA_QROWS

    def body(g, carry):
        bs = jnp.clip(g * NA_QROWS - NA_WIN_R // 2, 0, rows - NA_BAND)
        case = jnp.where(g == 0, 0, jnp.where(g == ngroups - 1, 2, 1))
        q0 = pl.multiple_of(g * nq, nq)
        q = q_ref[0, pl.ds(q0, nq), :].astype(BF16)
        k0 = pl.multiple_of(bs * GRID_W, GRID_W)
        kw = kb_ref[pl.ds(k0, band), :]
        vw = vb_ref[pl.ds(k0, band), :]
        s_loc = _dot_nt(q, kw) * scale + bias_ref[0, case]
        s_ctx = _dot_nt(q, kc) * scale
        m = jnp.maximum(jnp.max(s_loc, -1, keepdims=True), jnp.max(s_ctx, -1, keepdims=True))
        e_loc = jnp.exp(s_loc - m)
        e_ctx = jnp.exp(s_ctx - m)
        inv = 1.0 / (jnp.sum(e_loc, -1, keepdims=True) + jnp.sum(e_ctx, -1, keepdims=True))
        o = _dot((e_loc * inv).astype(BF16), vw) + _dot((e_ctx * inv).astype(BF16), vc)
        o_ref[0, pl.ds(q0, nq), :] = o.astype(BF16)
        return carry

    lax.fori_loop(0, ngroups, body, 0)


def _lat_na(proj3, cache_k, cache_v, bias, layer):
    b, t, _ = proj3.shape
    past = cache_k.shape[3]
    rows = t // GRID_W
    qb, kb, vb = OFF_QA // NA_DIM, OFF_KA // NA_DIM, OFF_VA // NA_DIM
    return pl.pallas_call(
        functools.partial(_lat_na_kernel, rows=rows),
        grid=(b, NA_HEADS),
        in_specs=[pl.BlockSpec((1, t, NA_DIM), lambda i, h: (i, 0, qb + h)),
                  pl.BlockSpec((1, t, NA_DIM), lambda i, h: (i, 0, kb + h)),
                  pl.BlockSpec((1, t, NA_DIM), lambda i, h: (i, 0, vb + h)),
                  pl.BlockSpec((1, 1, 1, past, NA_DIM), lambda i, h: (i, layer, h, 0, 0)),
                  pl.BlockSpec((1, 1, 1, past, NA_DIM), lambda i, h: (i, layer, h, 0, 0)),
                  pl.BlockSpec((1, 3, NA_QROWS * GRID_W, NA_BAND * GRID_W), lambda i, h: (h, 0, 0, 0))],
        out_specs=pl.BlockSpec((1, t, NA_DIM), lambda i, h: (i, 0, h)),
        out_shape=jax.ShapeDtypeStruct((b, t, NA_WIDTH), BF16),
        scratch_shapes=[pltpu.VMEM((t, NA_DIM), BF16), pltpu.VMEM((t, NA_DIM), BF16)],
        compiler_params=_cp(("parallel", "parallel"), 40),
        name="lat_na",
    )(proj3, proj3, proj3, cache_k, cache_v, bias)


def _lat_df_kernel(q_ref, k_ref, v_ref, kc_ref, vc_ref, cos_ref, sin_ref, lam_ref, g_ref, o_ref,
                   kall_ref, vall_ref, *, t, lam_init):
    qi = pl.program_id(2)

    @pl.when(qi == 0)
    def _():
        k = k_ref[0]
        kall_ref[0:t, :] = (k * cos_ref[...] + pltpu.roll(k, LANES // 2, 1) * sin_ref[...]).astype(BF16)
        kall_ref[t:, :] = kc_ref[0, 0, 0].astype(BF16)
        vall_ref[0:t, :] = v_ref[0].astype(BF16)
        vall_ref[t:, :] = vc_ref[0, 0, 0].astype(BF16)

    t0 = pl.multiple_of(qi * TQ_DF, TQ_DF)
    q = q_ref[0]
    q = q * cos_ref[pl.ds(t0, TQ_DF), :] + pltpu.roll(q, LANES // 2, 1) * sin_ref[pl.ds(t0, TQ_DF), :]
    q = q * DF_QK ** -0.5
    m1 = _map1_mask()
    qa = jnp.where(m1, q, 0.0).astype(BF16)
    qb = jnp.where(m1, 0.0, q).astype(BF16)
    lam = _diff_lambda(lam_ref, lam_init)
    kall = kall_ref[...]
    p = _softmax_rows(_dot_nt(qa, kall)) - lam * _softmax_rows(_dot_nt(qb, kall))
    o = _dot(p.astype(BF16), vall_ref[...])
    o_ref[0] = _subln(o, g_ref[...], lam_init).astype(BF16)


def _lat_df(proj3, cache_kp, cache_v, cos_t, sin_t, lam_p, subln_g, lam_init, layer):
    b, t, _ = proj3.shape
    past = cache_v.shape[3]
    qb, kb, vb = OFF_DQ // LANES, OFF_DK // LANES, OFF_DV // LANES
    return pl.pallas_call(
        functools.partial(_lat_df_kernel, t=t, lam_init=lam_init),
        grid=(b, DF_HEADS, t // TQ_DF),
        in_specs=[pl.BlockSpec((1, TQ_DF, LANES), lambda i, h, j: (i, j, qb + h)),
                  pl.BlockSpec((1, t, LANES), lambda i, h, j: (i, 0, kb + h)),
                  pl.BlockSpec((1, t, LANES), lambda i, h, j: (i, 0, vb + h)),
                  pl.BlockSpec((1, 1, 1, past, LANES), lambda i, h, j: (i, layer, h, 0, 0)),
                  pl.BlockSpec((1, 1, 1, past, DF_V), lambda i, h, j: (i, layer, h, 0, 0)),
                  pl.BlockSpec((t, LANES), lambda i, h, j: (0, 0)),
                  pl.BlockSpec((t, LANES), lambda i, h, j: (0, 0)),
                  pl.BlockSpec((4, DF_QK), lambda i, h, j: (0, 0)),
                  pl.BlockSpec((1, DF_V), lambda i, h, j: (0, 0))],
        out_specs=pl.BlockSpec((1, TQ_DF, DF_V), lambda i, h, j: (i, j, h)),
        out_shape=jax.ShapeDtypeStruct((b, t, DF_WIDTH), BF16),
        scratch_shapes=[pltpu.VMEM((t + past, LANES), BF16), pltpu.VMEM((t + past, DF_V), BF16)],
        compiler_params=_cp(("parallel", "parallel", "arbitrary"), 48),
        name="lat_df",
    )(proj3, proj3, proj3, cache_kp, cache_v, cos_t, sin_t, lam_p, subln_g.reshape(1, DF_V))


def _sg_kernel(u_ref, v_ref, g_ref, b_ref, ws_ref, bs_ref, o_ref, *, nch):
    for c in range(nch):
        rs = slice(c * SG_CHUNK, (c + 1) * SG_CHUNK)
        vn = _layernorm(v_ref[0, rs, :], g_ref[...], b_ref[...]).astype(BF16)
        for g in range(SG_GROUPS):
            cs = slice(g * SG_DIM, (g + 1) * SG_DIM)
            mixed = _dot(ws_ref[g], vn[:, cs]) + bs_ref[g]
            o_ref[0, rs, cs] = (u_ref[0, rs, cs] * mixed).astype(BF16)


def _spatial_gate(proj3, ln_g, ln_b, ws, bs):
    b, t, _ = proj3.shape
    nch = min(4, t // SG_CHUNK)
    tt = nch * SG_CHUNK
    ub, vb = OFF_U // SG_WIDTH, OFF_V // SG_WIDTH
    return pl.pallas_call(
        functools.partial(_sg_kernel, nch=nch),
        grid=(b, t // tt),
        in_specs=[pl.BlockSpec((1, tt, SG_WIDTH), lambda i, j: (i, j, ub)),
                  pl.BlockSpec((1, tt, SG_WIDTH), lambda i, j: (i, j, vb)),
                  pl.BlockSpec((1, SG_WIDTH), lambda i, j: (0, 0)),
                  pl.BlockSpec((1, SG_WIDTH), lambda i, j: (0, 0)),
                  pl.BlockSpec((SG_GROUPS, SG_CHUNK, SG_CHUNK), lambda i, j: (0, 0, 0)),
                  pl.BlockSpec((SG_GROUPS, SG_CHUNK, 1), lambda i, j: (0, 0, 0))],
        out_specs=pl.BlockSpec((1, tt, SG_WIDTH), lambda i, j: (i, j, 0)),
        out_shape=jax.ShapeDtypeStruct((b, t, SG_WIDTH), BF16),
        compiler_params=_cp(("parallel", "parallel"), 16),
        name="spatial_gate",
    )(proj3, proj3, ln_g.reshape(1, SG_WIDTH), ln_b.reshape(1, SG_WIDTH), ws, bs)


def _out_proj_kernel(oa_ref, ob_ref, oc_ref, x_ref, mod_ref, w_ref, lg_ref, lb_ref, x1_ref, h2t_ref):
    y = (_dot(oa_ref[...], w_ref[0:NA_WIDTH, :])
         + _dot(ob_ref[...], w_ref[NA_WIDTH:NA_WIDTH + SG_WIDTH, :])
         + _dot(oc_ref[...], w_ref[NA_WIDTH + SG_WIDTH:, :]))
    g1 = mod_ref[0, 2:3, :]
    sh2 = mod_ref[0, 3:4, :]
    sc2 = mod_ref[0, 4:5, :]
    x1 = _layernorm(DN_ALPHA * x_ref[...] + g1 * y, lg_ref[...], lb_ref[...])
    x1_ref[...] = x1
    h2t_ref[...] = (x1 * (1 + sc2) + sh2).T.astype(BF16)


def _out_proj(oa, ob, oc, x, mod, w, ln_g, ln_b, tiles_per_mod):
    n, d = x.shape
    return pl.pallas_call(
        _out_proj_kernel,
        grid=(n // TM_OUT,),
        in_specs=[pl.BlockSpec((TM_OUT, NA_WIDTH), lambda i: (i, 0)),
                  pl.BlockSpec((TM_OUT, SG_WIDTH), lambda i: (i, 0)),
                  pl.BlockSpec((TM_OUT, DF_WIDTH), lambda i: (i, 0)),
                  pl.BlockSpec((TM_OUT, d), lambda i: (i, 0)),
                  pl.BlockSpec((1, 6, d), lambda i: (i // tiles_per_mod, 0, 0)),
                  pl.BlockSpec((MIX_WIDTH, d), lambda i: (0, 0)),
                  pl.BlockSpec((1, d), lambda i: (0, 0)),
                  pl.BlockSpec((1, d), lambda i: (0, 0))],
        out_specs=[pl.BlockSpec((TM_OUT, d), lambda i: (i, 0)),
                   pl.BlockSpec((d, TM_OUT), lambda i: (0, i))],
        out_shape=[jax.ShapeDtypeStruct((n, d), F32),
                   jax.ShapeDtypeStruct((d, n), BF16)],
        compiler_params=_cp(("parallel",), 48),
        name="out_proj",
    )(oa, ob, oc, x, mod, w, ln_g.reshape(1, d), ln_b.reshape(1, d))


_PK_PAIRS = [(a, b) for a in range(PK_TOPK) for b in range(PK_TOPK) if (a + 1) * (b + 1) <= PK_TOPK]


def _merge_exchange_network(n):
    t = int(math.ceil(math.log2(n)))
    p = 2 ** (t - 1)
    pairs = []
    while p > 0:
        q, r, d = 2 ** (t - 1), 0, p
        while d > 0:
            pairs.extend((i, i + d) for i in range(n - d) if i & p == r)
            d, q, r = q - p, q // 2, p
        p //= 2
    return pairs


def _bitonic_merge_network(n):
    pairs, d = [], n // 2
    while d >= 1:
        pairs.extend((i, i + d) for i in range(n) if i & d == 0)
        d //= 2
    return pairs


_SORT16 = _merge_exchange_network(PK_TOPK)
_MERGE16 = _bitonic_merge_network(PK_TOPK)


def _compare_exchange(xs, network):
    xs = list(xs)
    for i, j in network:
        xs[i], xs[j] = jnp.maximum(xs[i], xs[j]), jnp.minimum(xs[i], xs[j])
    return xs


def _peer_score_kernel(h2t_ref, wqt_ref, keys_ref, s2_out, e2_out, tau_out, q1_out,
                       qt_ref, s_ref, tt_ref, thr_ref, z_ref):
    tm = h2t_ref.shape[1]
    qt_ref[...] = _dot(wqt_ref[...], h2t_ref[...]).astype(BF16)
    neg_inf = -jnp.inf
    sub = 8

    for c in range(2 * PK_HEADS):
        s = _dot(keys_ref[c], qt_ref[c * PK_NKEYS:(c + 1) * PK_NKEYS, :])
        s_ref[c] = s
        head, half = c // 2, c % 2
        xs = _compare_exchange([s[r * sub:(r + 1) * sub, :] for r in range(PK_TOPK)], _SORT16)
        shift = 1
        while shift < sub:
            ys = [pltpu.roll(xs[PK_TOPK - 1 - r], shift, 0) for r in range(PK_TOPK)]
            xs = _compare_exchange([jnp.maximum(x, y) for x, y in zip(xs, ys)], _MERGE16)
            shift *= 2
        for r in range(PK_TOPK):
            tt_ref[half, r, head:head + 1, :] = xs[r][0:1, :]

    def per_lane_chunk(ch, carry):
        l0 = pl.multiple_of(ch * LANES, LANES)
        t1 = [tt_ref[0, a, :, pl.ds(l0, LANES)] for a in range(PK_TOPK)]
        t2 = [tt_ref[1, b, :, pl.ds(l0, LANES)] for b in range(PK_TOPK)]
        cands = [t1[a] + t2[b] for a, b in _PK_PAIRS]
        cur = list(cands)
        for _ in range(PK_TOPK - 1):
            m = functools.reduce(jnp.maximum, cur)
            found = jnp.zeros(m.shape, jnp.bool_)
            nxt = []
            for cnd in cur:
                is_m = cnd == m
                nxt.append(jnp.where(jnp.logical_and(is_m, jnp.logical_not(found)), neg_inf, cnd))
                found = jnp.logical_or(found, is_m)
            cur = nxt
        thr = functools.reduce(jnp.maximum, cur)
        e1 = [jnp.exp(t1[a] - t1[0]) for a in range(PK_TOPK)]
        e2 = [jnp.exp(t2[b] - t2[0]) for b in range(PK_TOPK)]
        z = jnp.zeros(thr.shape, F32)
        for (a, b), cnd in zip(_PK_PAIRS, cands):
            z = z + jnp.where(cnd >= thr, e1[a] * e2[b], 0.0)
        thr_ref[:, pl.ds(l0, LANES)] = thr
        z_ref[:, pl.ds(l0, LANES)] = z
        return carry

    lax.fori_loop(0, tm // LANES, per_lane_chunk, 0)

    for h in range(PK_HEADS):
        s1 = s_ref[2 * h]
        s2 = s_ref[2 * h + 1]
        thr = thr_ref[h:h + 1, :]
        tau = jnp.full(s1.shape, jnp.inf, F32)
        for b in range(PK_TOPK):
            t2b = tt_ref[1, b, h:h + 1, :]
            tau = jnp.minimum(tau, jnp.where(s1 + t2b >= thr, t2b, jnp.inf))
        tau_out[h] = tau
        q1_out[h] = jnp.exp(s1 - tt_ref[0, 0, h:h + 1, :]) * (0.5 / z_ref[h:h + 1, :])
        e2_out[h] = jnp.exp(s2 - tt_ref[1, 0, h:h + 1, :])
        s2_out[h] = s2


def _peer_score(h2t, wqt, keys):
    d, n = h2t.shape
    spec = pl.BlockSpec((PK_HEADS, PK_NKEYS, TM_SCORE), lambda i: (0, 0, i))
    shape = jax.ShapeDtypeStruct((PK_HEADS, PK_NKEYS, n), F32)
    return pl.pallas_call(
        _peer_score_kernel,
        grid=(n // TM_SCORE,),
        in_specs=[pl.BlockSpec((d, TM_SCORE), lambda i: (0, i)),
                  pl.BlockSpec(wqt.shape, lambda i: (0, 0)),
                  pl.BlockSpec(keys.shape, lambda i: (0, 0, 0))],
        out_specs=[spec, spec, spec, spec],
        out_shape=[shape, shape, shape, shape],
        scratch_shapes=[pltpu.VMEM((PK_HEADS * PK_QDIM, TM_SCORE), BF16),
                        pltpu.VMEM((2 * PK_HEADS, PK_NKEYS, TM_SCORE), F32),
                        pltpu.VMEM((2, PK_TOPK, PK_HEADS, TM_SCORE), F32),
                        pltpu.VMEM((PK_HEADS, TM_SCORE), F32),
                        pltpu.VMEM((PK_HEADS, TM_SCORE), F32)],
        compiler_params=_cp(("parallel",), 48),
        name="peer_score",
    )(h2t, wqt, keys)


def _peer_dense_kernel(h2t_ref, u_ref, vt_ref, s2_ref, e2_ref, tau_ref, q1_ref, o_ref, acc_ref, a_ref, w_ref):
    sqrt_half = math.sqrt(0.5)

    @pl.when(pl.program_id(1) == 0)
    def _():
        acc_ref[...] = jnp.zeros_like(acc_ref)

    a_ref[...] = _dot(u_ref[...], h2t_ref[...])
    for ii in range(TE_PEER // PK_NKEYS):
        taus = [tau_ref[h, ii:ii + 1, :] for h in range(PK_HEADS)]
        q1s = [q1_ref[h, ii:ii + 1, :] for h in range(PK_HEADS)]
        for jc in range(PK_NKEYS // RJ_PEER):
            js = slice(jc * RJ_PEER, (jc + 1) * RJ_PEER)
            rs = slice(ii * PK_NKEYS + jc * RJ_PEER, ii * PK_NKEYS + (jc + 1) * RJ_PEER)
            g = None
            for h in range(PK_HEADS):
                term = jnp.where(s2_ref[h, js, :] >= taus[h], e2_ref[h, js, :], 0.0) * q1s[h]
                g = term if g is None else g + term
            a = a_ref[rs, :]
            w_ref[rs, :] = (g * (a * (1.0 + lax.erf(a * sqrt_half)))).astype(BF16)
    acc_ref[...] += _dot(vt_ref[...], w_ref[...])

    @pl.when(pl.program_id(1) == pl.num_programs(1) - 1)
    def _():
        o_ref[...] = acc_ref[...]


def _peer_dense(h2t, u_tab, vt_tab, s2, e2, tau, q1):
    d, n = h2t.shape
    ne = u_tab.shape[0]
    ri = TE_PEER // PK_NKEYS
    gspec = pl.BlockSpec((PK_HEADS, PK_NKEYS, TM_PEER), lambda i, e: (0, 0, i))
    rspec = pl.BlockSpec((PK_HEADS, ri, TM_PEER), lambda i, e: (0, e, i))
    return pl.pallas_call(
        _peer_dense_kernel,
        grid=(n // TM_PEER, ne // TE_PEER),
        in_specs=[pl.BlockSpec((d, TM_PEER), lambda i, e: (0, i)),
                  pl.BlockSpec((TE_PEER, d), lambda i, e: (e, 0)),
                  pl.BlockSpec((d, TE_PEER), lambda i, e: (0, e)),
                  gspec, gspec, rspec, rspec],
        out_specs=pl.BlockSpec((d, TM_PEER), lambda i, e: (0, i)),
        out_shape=jax.ShapeDtypeStruct((d, n), F32),
        scratch_shapes=[pltpu.VMEM((d, TM_PEER), F32),
                        pltpu.VMEM((TE_PEER, TM_PEER), F32),
                        pltpu.VMEM((TE_PEER, TM_PEER), BF16)],
        compiler_params=_cp(("parallel", "arbitrary"), 56),
        name="peer_dense",
    )(h2t, u_tab, vt_tab, s2, e2, tau, q1)


def _peer_ln_kernel(yt_ref, x1_ref, mod_ref, lg_ref, lb_ref, o_ref):
    g2 = mod_ref[0, 5:6, :]
    o_ref[...] = _layernorm(DN_ALPHA * x1_ref[...] + g2 * yt_ref[...].T, lg_ref[...], lb_ref[...])


def _peer_ln(yt, x1, mod, ln_g, ln_b, tiles_per_mod):
    n, d = x1.shape
    return pl.pallas_call(
        _peer_ln_kernel,
        grid=(n // TM_OUT,),
        in_specs=[pl.BlockSpec((d, TM_OUT), lambda i: (0, i)),
                  pl.BlockSpec((TM_OUT, d), lambda i: (i, 0)),
                  pl.BlockSpec((1, 6, d), lambda i: (i // tiles_per_mod, 0, 0)),
                  pl.BlockSpec((1, d), lambda i: (0, 0)),
                  pl.BlockSpec((1, d), lambda i: (0, 0))],
        out_specs=pl.BlockSpec((TM_OUT, d), lambda i: (i, 0)),
        out_shape=jax.ShapeDtypeStruct((n, d), F32),
        compiler_params=_cp(("parallel",), 40),
        name="peer_ln",
    )(yt, x1, mod, ln_g.reshape(1, d), ln_b.reshape(1, d))


def _permute_in_proj(w):
    d = w.shape[0]

    def interleave(cols):
        c = cols.reshape(d, 2, DF_HEADS, 2, DF_QK // 2)
        return c.transpose(0, 2, 3, 1, 4).reshape(d, 2 * DF_HEADS * DF_QK)

    return jnp.concatenate([w[:, :OFF_DQ], interleave(w[:, OFF_DQ:OFF_DK]),
                            interleave(w[:, OFF_DK:OFF_DV]), w[:, OFF_DV:]], axis=1).astype(BF16)


def _pair_lanes(a, b):
    half = DF_QK // 2
    return jnp.concatenate([a[..., :half], b[..., :half], a[..., half:], b[..., half:]], -1)


def _unpair_lanes(kk):
    half = DF_QK // 2
    a = jnp.concatenate([kk[..., 0:half], kk[..., 2 * half:3 * half]], -1)
    b = jnp.concatenate([kk[..., half:2 * half], kk[..., 3 * half:]], -1)
    return a, b


def _rope_tables(t):
    tok = jnp.arange(t)
    row = (tok // GRID_W).astype(F32)
    col = (tok % GRID_W).astype(F32)
    n_freq = DF_QK // 4
    inv = 1.0 / (ROPE_THETA ** (jnp.arange(n_freq, dtype=F32) / n_freq))
    ang = jnp.concatenate([row[:, None] * inv, col[:, None] * inv], -1)
    cos, sin = jnp.cos(ang), jnp.sin(ang)
    return jnp.concatenate([cos] * 4, -1), jnp.concatenate([-sin, -sin, sin, sin], -1)


def _na_bias_table(rpb, rows):
    cases = _na_group_layout(rows)
    cols = np.arange(GRID_W)
    start = np.clip(cols - NA_WIN_C // 2, 0, GRID_W - NA_WIN_C)
    inwin = (cols[None, :] >= start[:, None]) & (cols[None, :] < start[:, None] + NA_WIN_C)
    nh = rpb.shape[0]
    dr = np.zeros((len(cases), NA_QROWS, NA_BAND), np.int32)
    valid = np.zeros((len(cases), NA_QROWS, NA_BAND), bool)
    for c, pat in enumerate(cases):
        for a, (off, cs) in enumerate(pat):
            for kr in range(off, off + NA_WIN_R):
                dr[c, a, kr] = kr - off - cs + NA_WIN_R - 1
                valid[c, a, kr] = True
    sel = rpb[:, dr.reshape(-1), :].reshape(nh, len(cases), NA_QROWS, NA_BAND, 2 * NA_WIN_C - 1)
    lead = GRID_W - NA_WIN_C
    p = jnp.pad(sel, ((0, 0),) * 4 + ((lead, 2 * GRID_W - lead - (2 * NA_WIN_C - 1)),))
    m = jnp.tile(p, (1, 1, 1, 1, GRID_W))[..., :GRID_W * (2 * GRID_W - 1)]
    m = m.reshape(nh, len(cases), NA_QROWS, NA_BAND, GRID_W, 2 * GRID_W - 1)
    tab = m[..., GRID_W - 1:].transpose(0, 1, 2, 4, 3, 5)
    keep = valid[None, :, :, None, :, None] & inwin[None, None, None, :, None, :]
    tab = jnp.where(keep, tab, NEG_BIG)
    return tab.reshape(nh, len(cases), NA_QROWS * GRID_W, NA_BAND * GRID_W)


def _heads_out(a, b, t, h):
    return a.reshape(b, t, h, -1).transpose(0, 2, 1, 3)


def _peer_block(oa, ob, oc, x, mod, tiles_out, w_out, ln_g, ln_b, wqt, keys, u_tab, vt_tab):
    x1, h2t = _out_proj(oa, ob, oc, x, mod, w_out, ln_g[0], ln_b[0], tiles_out)
    s2, e2, tau, q1 = _peer_score(h2t, wqt, keys)
    yt = _peer_dense(h2t, u_tab, vt_tab, s2, e2, tau, q1)
    return _peer_ln(yt, x1, mod, ln_g[1], ln_b[1], tiles_out)


def kernel(x_prompt, x_sample, cache_na_k, cache_na_v, cache_df_k1, cache_df_k2, cache_df_v, c, c_ctx,
           w_mod, b_mod, w_in, na_rpb, sg_ln_g, sg_ln_b, sg_w, sg_b, df_lambda, df_subln_g, w_out,
           pk_wq, pk_keys, pk_u, pk_v, ln_g, ln_b):
    bp, tp, d = x_prompt.shape
    bs, ts, _ = x_sample.shape
    n_p, n_s = bp * tp, bs * ts

    cond8 = jnp.zeros((8, d), F32).at[0].set(c_ctx).at[1:1 + bs].set(c)
    mods = _modulation(cond8, w_mod, b_mod).reshape(DEPTH, 8, 6, d)

    cos_t, sin_t = _rope_tables(ts)
    cache_dk = _pair_lanes(cache_df_k1, cache_df_k2)

    xp = x_prompt.reshape(n_p, d)
    xs = x_sample.reshape(n_s, d)
    st = [[] for _ in range(5)]
    for l in range(DEPTH):
        lam_init = 0.8 - 0.6 * math.exp(-0.3 * l)
        w_in_l = _permute_in_proj(w_in[l])
        w_out_l = w_out[l].astype(BF16)
        wqt = pk_wq[l].T.astype(BF16)
        keys = pk_keys[l].reshape(2 * PK_HEADS, PK_NKEYS, PK_QDIM // 2).astype(BF16)
        u_tab = pk_u[l].astype(BF16)
        vt_tab = pk_v[l].T.astype(BF16)
        ws = sg_w[l].astype(BF16)
        bs_col = sg_b[l].reshape(SG_GROUPS, SG_CHUNK, 1)
        bias = _na_bias_table(na_rpb[l], ts // GRID_W)
        mod_p = mods[l, 0:1]
        mod_s = mods[l, 1:1 + bs]

        proj = _in_proj(xp, mod_p, w_in_l, n_p // TM_IN)
        proj3 = proj.reshape(bp, tp, IN_WIDTH)
        oa, oc = _ctx_attn(proj3, df_lambda[l], df_subln_g[l], lam_init)
        ob = _spatial_gate(proj3, sg_ln_g[l], sg_ln_b[l], ws, bs_col)
        st[0].append(_heads_out(proj[:, OFF_KA:OFF_KA + NA_WIDTH], bp, tp, NA_HEADS))
        st[1].append(_heads_out(proj[:, OFF_VA:OFF_VA + NA_WIDTH], bp, tp, NA_HEADS))
        k1, k2 = _unpair_lanes(_heads_out(proj[:, OFF_DK:OFF_DV], bp, tp, DF_HEADS))
        st[2].append(k1)
        st[3].append(k2)
        st[4].append(_heads_out(proj[:, OFF_DV:], bp, tp, DF_HEADS))
        xp = _peer_block(oa.reshape(n_p, NA_WIDTH), ob.reshape(n_p, SG_WIDTH), oc.reshape(n_p, DF_WIDTH),
                         xp, mod_p, n_p // TM_OUT, w_out_l, ln_g[l], ln_b[l], wqt, keys, u_tab, vt_tab)

        proj = _in_proj(xs, mod_s, w_in_l, ts // TM_IN)
        proj3 = proj.reshape(bs, ts, IN_WIDTH)
        oa = _lat_na(proj3, cache_na_k, cache_na_v, bias, l)
        ob = _spatial_gate(proj3, sg_ln_g[l], sg_ln_b[l], ws, bs_col)
        oc = _lat_df(proj3, cache_dk, cache_df_v, cos_t, sin_t, df_lambda[l], df_subln_g[l], lam_init, l)
        xs = _peer_block(oa.reshape(n_s, NA_WIDTH), ob.reshape(n_s, SG_WIDTH), oc.reshape(n_s, DF_WIDTH),
                         xs, mod_s, ts // TM_OUT, w_out_l, ln_g[l], ln_b[l], wqt, keys, u_tab, vt_tab)

    outs = [jnp.stack(s, 1) for s in st]
    return (xp.reshape(bp, tp, d), xs.reshape(bs, ts, d), *outs)
```

```python
import functools
import math

import numpy as np
import jax
import jax.numpy as jnp
from jax import lax
from jax.experimental import pallas as pl
from jax.experimental.pallas import tpu as pltpu

F32 = jnp.float32
BF16 = jnp.bfloat16

D_MODEL = 2048
DEPTH = 4
GRID_W = 64
NA_HEADS = 8
NA_DIM = 128
NA_WIN_R = 8
NA_WIN_C = 16
SG_GROUPS = 4
SG_DIM = 128
SG_CHUNK = 128
DF_HEADS = 4
DF_QK = 64
DF_V = 128
ROPE_THETA = 10000.0
NA_WIDTH = NA_HEADS * NA_DIM
SG_WIDTH = SG_GROUPS * SG_DIM
DF_WIDTH = DF_HEADS * DF_V
MIX_WIDTH = NA_WIDTH + SG_WIDTH + DF_WIDTH
IN_WIDTH = 3 * NA_WIDTH + 2 * SG_WIDTH + 4 * DF_HEADS * DF_QK + DF_WIDTH
PK_HEADS = 8
PK_QDIM = 256
PK_NKEYS = 128
PK_TOPK = 16
PK_EXPERTS = PK_NKEYS * PK_NKEYS
DN_ALPHA = (2 * DEPTH) ** 0.25
LN_EPS = 1e-5
RMS_EPS = 1e-6

OFF_QA = 0
OFF_KA = NA_WIDTH
OFF_VA = 2 * NA_WIDTH
OFF_U = 3 * NA_WIDTH
OFF_V = OFF_U + SG_WIDTH
OFF_DQ = OFF_V + SG_WIDTH
OFF_DK = OFF_DQ + 2 * DF_HEADS * DF_QK
OFF_DV = OFF_DK + 2 * DF_HEADS * DF_QK

LANES = 128
MIB = 1024 * 1024

TM_IN = 512
TN_IN = IN_WIDTH // 4
TM_OUT = 512
TM_SCORE = 256
TM_PEER = 512
TE_PEER = 1024
RJ_PEER = 16
TQ_DF = 256
TN_MOD = 1024
NEG_BIG = -1e30


def _cp(sem, vmem_mib):
    return pltpu.CompilerParams(dimension_semantics=sem, vmem_limit_bytes=vmem_mib * MIB)


def _dot(a, b):
    return jnp.dot(a, b, preferred_element_type=F32)


def _dot_nt(a, b):
    return lax.dot_general(a, b, (((1,), (1,)), ((), ())), preferred_element_type=F32)


def _layernorm(z, g, b):
    mu = jnp.mean(z, -1, keepdims=True)
    d = z - mu
    var = jnp.mean(d * d, -1, keepdims=True)
    return d * lax.rsqrt(var + LN_EPS) * g + b


def _softmax_rows(s):
    m = jnp.max(s, -1, keepdims=True)
    e = jnp.exp(s - m)
    return e * (1.0 / jnp.sum(e, -1, keepdims=True))


def _diff_lambda(lam_ref, lam_init):
    lf = lam_ref[...]
    a = jnp.sum(lf[0:1, :] * lf[1:2, :], axis=1, keepdims=True)
    b = jnp.sum(lf[2:3, :] * lf[3:4, :], axis=1, keepdims=True)
    return jnp.exp(a) - jnp.exp(b) + lam_init


def _map1_mask():
    lane = lax.broadcasted_iota(jnp.int32, (1, LANES), 1)
    return (lane // (DF_QK // 2)) % 2 == 0


def _subln(o, g, lam_init):
    return o * lax.rsqrt(jnp.mean(o * o, -1, keepdims=True) + RMS_EPS) * g * (1.0 - lam_init)


def _mod_kernel(c_ref, w_ref, b_ref, o_ref):
    c = c_ref[...]
    a = (c * jax.nn.sigmoid(c)).astype(BF16)
    o_ref[0] = _dot(a, w_ref[0].astype(BF16)) + b_ref[0]


def _modulation(cond8, w_mod, b_mod):
    depth, d, n6 = w_mod.shape
    return pl.pallas_call(
        _mod_kernel,
        grid=(depth, n6 // TN_MOD),
        in_specs=[pl.BlockSpec((8, d), lambda l, j: (0, 0)),
                  pl.BlockSpec((1, d, TN_MOD), lambda l, j: (l, 0, j)),
                  pl.BlockSpec((1, 1, TN_MOD), lambda l, j: (l, 0, j))],
        out_specs=pl.BlockSpec((1, 8, TN_MOD), lambda l, j: (l, 0, j)),
        out_shape=jax.ShapeDtypeStruct((depth, 8, n6), F32),
        compiler_params=_cp(("parallel", "parallel"), 32),
        name="modulation",
    )(cond8, w_mod, b_mod.reshape(depth, 1, n6))


def _in_proj_kernel(x_ref, mod_ref, w_ref, o_ref, h_ref):
    @pl.when(pl.program_id(1) == 0)
    def _():
        sh = mod_ref[0, 0:1, :]
        sc = mod_ref[0, 1:2, :]
        h_ref[...] = (x_ref[...] * (1 + sc) + sh).astype(BF16)

    o_ref[...] = _dot(h_ref[...], w_ref[...])


def _in_proj(x, mod, w, tiles_per_mod):
    n, d = x.shape
    nw = w.shape[1]
    return pl.pallas_call(
        _in_proj_kernel,
        grid=(n // TM_IN, nw // TN_IN),
        in_specs=[pl.BlockSpec((TM_IN, d), lambda i, j: (i, 0)),
                  pl.BlockSpec((1, 6, d), lambda i, j: (i // tiles_per_mod, 0, 0)),
                  pl.BlockSpec((d, TN_IN), lambda i, j: (0, j))],
        out_specs=pl.BlockSpec((TM_IN, TN_IN), lambda i, j: (i, j)),
        out_shape=jax.ShapeDtypeStruct((n, nw), F32),
        scratch_shapes=[pltpu.VMEM((TM_IN, d), BF16)],
        compiler_params=_cp(("parallel", "arbitrary"), 40),
        name="in_proj",
    )(x, mod, w)


def _ctx_attn_kernel(p_ref, lam_ref, g_ref, oa_ref, oc_ref, *, lam_init):
    scale = NA_DIM ** -0.5
    for h in range(NA_HEADS):
        sl = slice(h * NA_DIM, (h + 1) * NA_DIM)
        q = p_ref[0, :, OFF_QA + h * NA_DIM:OFF_QA + (h + 1) * NA_DIM].astype(BF16)
        k = p_ref[0, :, OFF_KA + h * NA_DIM:OFF_KA + (h + 1) * NA_DIM].astype(BF16)
        v = p_ref[0, :, OFF_VA + h * NA_DIM:OFF_VA + (h + 1) * NA_DIM].astype(BF16)
        p = _softmax_rows(_dot_nt(q, k) * scale)
        oa_ref[0, :, sl] = _dot(p.astype(BF16), v).astype(BF16)
    lam = _diff_lambda(lam_ref, lam_init)
    m1 = _map1_mask()
    dscale = DF_QK ** -0.5
    for h in range(DF_HEADS):
        sl = slice(h * DF_V, (h + 1) * DF_V)
        q = p_ref[0, :, OFF_DQ + h * LANES:OFF_DQ + (h + 1) * LANES] * dscale
        k = p_ref[0, :, OFF_DK + h * LANES:OFF_DK + (h + 1) * LANES].astype(BF16)
        v = p_ref[0, :, OFF_DV + h * DF_V:OFF_DV + (h + 1) * DF_V].astype(BF16)
        qa = jnp.where(m1, q, 0.0).astype(BF16)
        qb = jnp.where(m1, 0.0, q).astype(BF16)
        p = _softmax_rows(_dot_nt(qa, k)) - lam * _softmax_rows(_dot_nt(qb, k))
        o = _dot(p.astype(BF16), v)
        oc_ref[0, :, sl] = _subln(o, g_ref[...], lam_init).astype(BF16)


def _ctx_attn(proj3, lam_p, subln_g, lam_init):
    b, t, nw = proj3.shape
    return pl.pallas_call(
        functools.partial(_ctx_attn_kernel, lam_init=lam_init),
        grid=(b,),
        in_specs=[pl.BlockSpec((1, t, nw), lambda i: (i, 0, 0)),
                  pl.BlockSpec((4, DF_QK), lambda i: (0, 0)),
                  pl.BlockSpec((1, DF_V), lambda i: (0, 0))],
        out_specs=[pl.BlockSpec((1, t, NA_WIDTH), lambda i: (i, 0, 0)),
                   pl.BlockSpec((1, t, DF_WIDTH), lambda i: (i, 0, 0))],
        out_shape=[jax.ShapeDtypeStruct((b, t, NA_WIDTH), BF16),
                   jax.ShapeDtypeStruct((b, t, DF_WIDTH), BF16)],
        compiler_params=_cp(("parallel",), 32),
        name="ctx_attn",
    )(proj3, lam_p, subln_g.reshape(1, DF_V))


NA_QROWS = 4
NA_BAND = NA_WIN_R + NA_QROWS


def _na_group_layout(rows):
    groups = []
    for g in range(rows // NA_QROWS):
        bs = int(np.clip(g * NA_QROWS - NA_WIN_R // 2, 0, rows - NA_BAND))
        pat = []
        for a in range(NA_QROWS):
            r = g * NA_QROWS + a
            r0 = int(np.clip(r - NA_WIN_R // 2, 0, rows - NA_WIN_R))
            pat.append((r0 - bs, r - r0))
        groups.append((bs, tuple(pat)))
    cases = [groups[0][1], groups[1][1], groups[-1][1]]
    assert all(p == cases[1] for _, p in groups[1:-1]) and rows >= 2 * NA_BAND
    assert all(0 <= off and off + NA_WIN_R <= NA_BAND for c in cases for off, _ in c)
    return cases


def _lat_na_kernel(q_ref, k_ref, v_ref, kc_ref, vc_ref, bias_ref, o_ref, kb_ref, vb_ref, *, rows):
    kb_ref[...] = k_ref[0].astype(BF16)
    vb_ref[...] = v_ref[0].astype(BF16)
    kc = kc_ref[0, 0, 0].astype(BF16)
    vc = vc_ref[0, 0, 0].astype(BF16)
    scale = NA_DIM ** -0.5
    nq = NA_QROWS * GRID_W
    band = NA_BAND * GRID_W
    ngroups = rows // NA_QROWS

    def body(g, carry):
        bs = jnp.clip(g * NA_QROWS - NA_WIN_R // 2, 0, rows - NA_BAND)
        case = jnp.where(g == 0, 0, jnp.where(g == ngroups - 1, 2, 1))
        q0 = pl.multiple_of(g * nq, nq)
        q = q_ref[0, pl.ds(q0, nq), :].astype(BF16)
        k0 = pl.multiple_of(bs * GRID_W, GRID_W)
        kw = kb_ref[pl.ds(k0, band), :]
        vw = vb_ref[pl.ds(k0, band), :]
        s_loc = _dot_nt(q, kw) * scale + bias_ref[0, case]
        s_ctx = _dot_nt(q, kc) * scale
        m = jnp.maximum(jnp.max(s_loc, -1, keepdims=True), jnp.max(s_ctx, -1, keepdims=True))
        e_loc = jnp.exp(s_loc - m)
        e_ctx = jnp.exp(s_ctx - m)
        inv = 1.0 / (jnp.sum(e_loc, -1, keepdims=True) + jnp.sum(e_ctx, -1, keepdims=True))
        o = _dot((e_loc * inv).astype(BF16), vw) + _dot((e_ctx * inv).astype(BF16), vc)
        o_ref[0, pl.ds(q0, nq), :] = o.astype(BF16)
        return carry

    lax.fori_loop(0, ngroups, body, 0)


def _lat_na(proj3, cache_k, cache_v, bias, layer):
    b, t, _ = proj3.shape
    past = cache_k.shape[3]
    rows = t // GRID_W
    qb, kb, vb = OFF_QA // NA_DIM, OFF_KA // NA_DIM, OFF_VA // NA_DIM
    return pl.pallas_call(
        functools.partial(_lat_na_kernel, rows=rows),
        grid=(b, NA_HEADS),
        in_specs=[pl.BlockSpec((1, t, NA_DIM), lambda i, h: (i, 0, qb + h)),
                  pl.BlockSpec((1, t, NA_DIM), lambda i, h: (i, 0, kb + h)),
                  pl.BlockSpec((1, t, NA_DIM), lambda i, h: (i, 0, vb + h)),
                  pl.BlockSpec((1, 1, 1, past, NA_DIM), lambda i, h: (i, layer, h, 0, 0)),
                  pl.BlockSpec((1, 1, 1, past, NA_DIM), lambda i, h: (i, layer, h, 0, 0)),
                  pl.BlockSpec((1, 3, NA_QROWS * GRID_W, NA_BAND * GRID_W), lambda i, h: (h, 0, 0, 0))],
        out_specs=pl.BlockSpec((1, t, NA_DIM), lambda i, h: (i, 0, h)),
        out_shape=jax.ShapeDtypeStruct((b, t, NA_WIDTH), BF16),
        scratch_shapes=[pltpu.VMEM((t, NA_DIM), BF16), pltpu.VMEM((t, NA_DIM), BF16)],
        compiler_params=_cp(("parallel", "parallel"), 40),
        name="lat_na",
    )(proj3, proj3, proj3, cache_k, cache_v, bias)


def _lat_df_kernel(q_ref, k_ref, v_ref, kc_ref, vc_ref, cos_ref, sin_ref, lam_ref, g_ref, o_ref,
                   kall_ref, vall_ref, *, t, lam_init):
    qi = pl.program_id(2)

    @pl.when(qi == 0)
    def _():
        k = k_ref[0]
        kall_ref[0:t, :] = (k * cos_ref[...] + pltpu.roll(k, LANES // 2, 1) * sin_ref[...]).astype(BF16)
        kall_ref[t:, :] = kc_ref[0, 0, 0].astype(BF16)
        vall_ref[0:t, :] = v_ref[0].astype(BF16)
        vall_ref[t:, :] = vc_ref[0, 0, 0].astype(BF16)

    t0 = pl.multiple_of(qi * TQ_DF, TQ_DF)
    q = q_ref[0]
    q = q * cos_ref[pl.ds(t0, TQ_DF), :] + pltpu.roll(q, LANES // 2, 1) * sin_ref[pl.ds(t0, TQ_DF), :]
    q = q * DF_QK ** -0.5
    m1 = _map1_mask()
    qa = jnp.where(m1, q, 0.0).astype(BF16)
    qb = jnp.where(m1, 0.0, q).astype(BF16)
    lam = _diff_lambda(lam_ref, lam_init)
    kall = kall_ref[...]
    p = _softmax_rows(_dot_nt(qa, kall)) - lam * _softmax_rows(_dot_nt(qb, kall))
    o = _dot(p.astype(BF16), vall_ref[...])
    o_ref[0] = _subln(o, g_ref[...], lam_init).astype(BF16)


def _lat_df(proj3, cache_kp, cache_v, cos_t, sin_t, lam_p, subln_g, lam_init, layer):
    b, t, _ = proj3.shape
    past = cache_v.shape[3]
    qb, kb, vb = OFF_DQ // LANES, OFF_DK // LANES, OFF_DV // LANES
    return pl.pallas_call(
        functools.partial(_lat_df_kernel, t=t, lam_init=lam_init),
        grid=(b, DF_HEADS, t // TQ_DF),
        in_specs=[pl.BlockSpec((1, TQ_DF, LANES), lambda i, h, j: (i, j, qb + h)),
                  pl.BlockSpec((1, t, LANES), lambda i, h, j: (i, 0, kb + h)),
                  pl.BlockSpec((1, t, LANES), lambda i, h, j: (i, 0, vb + h)),
                  pl.BlockSpec((1, 1, 1, past, LANES), lambda i, h, j: (i, layer, h, 0, 0)),
                  pl.BlockSpec((1, 1, 1, past, DF_V), lambda i, h, j: (i, layer, h, 0, 0)),
                  pl.BlockSpec((t, LANES), lambda i, h, j: (0, 0)),
                  pl.BlockSpec((t, LANES), lambda i, h, j: (0, 0)),
                  pl.BlockSpec((4, DF_QK), lambda i, h, j: (0, 0)),
                  pl.BlockSpec((1, DF_V), lambda i, h, j: (0, 0))],
        out_specs=pl.BlockSpec((1, TQ_DF, DF_V), lambda i, h, j: (i, j, h)),
        out_shape=jax.ShapeDtypeStruct((b, t, DF_WIDTH), BF16),
        scratch_shapes=[pltpu.VMEM((t + past, LANES), BF16), pltpu.VMEM((t + past, DF_V), BF16)],
        compiler_params=_cp(("parallel", "parallel", "arbitrary"), 48),
        name="lat_df",
    )(proj3, proj3, proj3, cache_kp, cache_v, cos_t, sin_t, lam_p, subln_g.reshape(1, DF_V))


def _sg_kernel(u_ref, v_ref, g_ref, b_ref, ws_ref, bs_ref, o_ref, *, nch):
    for c in range(nch):
        rs = slice(c * SG_CHUNK, (c + 1) * SG_CHUNK)
        vn = _layernorm(v_ref[0, rs, :], g_ref[...], b_ref[...]).astype(BF16)
        for g in range(SG_GROUPS):
            cs = slice(g * SG_DIM, (g + 1) * SG_DIM)
            mixed = _dot(ws_ref[g], vn[:, cs]) + bs_ref[g]
            o_ref[0, rs, cs] = (u_ref[0, rs, cs] * mixed).astype(BF16)


def _spatial_gate(proj3, ln_g, ln_b, ws, bs):
    b, t, _ = proj3.shape
    nch = min(4, t // SG_CHUNK)
    tt = nch * SG_CHUNK
    ub, vb = OFF_U // SG_WIDTH, OFF_V // SG_WIDTH
    return pl.pallas_call(
        functools.partial(_sg_kernel, nch=nch),
        grid=(b, t // tt),
        in_specs=[pl.BlockSpec((1, tt, SG_WIDTH), lambda i, j: (i, j, ub)),
                  pl.BlockSpec((1, tt, SG_WIDTH), lambda i, j: (i, j, vb)),
                  pl.BlockSpec((1, SG_WIDTH), lambda i, j: (0, 0)),
                  pl.BlockSpec((1, SG_WIDTH), lambda i, j: (0, 0)),
                  pl.BlockSpec((SG_GROUPS, SG_CHUNK, SG_CHUNK), lambda i, j: (0, 0, 0)),
                  pl.BlockSpec((SG_GROUPS, SG_CHUNK, 1), lambda i, j: (0, 0, 0))],
        out_specs=pl.BlockSpec((1, tt, SG_WIDTH), lambda i, j: (i, j, 0)),
        out_shape=jax.ShapeDtypeStruct((b, t, SG_WIDTH), BF16),
        compiler_params=_cp(("parallel", "parallel"), 16),
        name="spatial_gate",
    )(proj3, proj3, ln_g.reshape(1, SG_WIDTH), ln_b.reshape(1, SG_WIDTH), ws, bs)


def _out_proj_kernel(oa_ref, ob_ref, oc_ref, x_ref, mod_ref, w_ref, lg_ref, lb_ref, x1_ref, h2t_ref):
    y = (_dot(oa_ref[...], w_ref[0:NA_WIDTH, :])
         + _dot(ob_ref[...], w_ref[NA_WIDTH:NA_WIDTH + SG_WIDTH, :])
         + _dot(oc_ref[...], w_ref[NA_WIDTH + SG_WIDTH:, :]))
    g1 = mod_ref[0, 2:3, :]
    sh2 = mod_ref[0, 3:4, :]
    sc2 = mod_ref[0, 4:5, :]
    x1 = _layernorm(DN_ALPHA * x_ref[...] + g1 * y, lg_ref[...], lb_ref[...])
    x1_ref[...] = x1
    h2t_ref[...] = (x1 * (1 + sc2) + sh2).T.astype(BF16)


def _out_proj(oa, ob, oc, x, mod, w, ln_g, ln_b, tiles_per_mod):
    n, d = x.shape
    return pl.pallas_call(
        _out_proj_kernel,
        grid=(n // TM_OUT,),
        in_specs=[pl.BlockSpec((TM_OUT, NA_WIDTH), lambda i: (i, 0)),
                  pl.BlockSpec((TM_OUT, SG_WIDTH), lambda i: (i, 0)),
                  pl.BlockSpec((TM_OUT, DF_WIDTH), lambda i: (i, 0)),
                  pl.BlockSpec((TM_OUT, d), lambda i: (i, 0)),
                  pl.BlockSpec((1, 6, d), lambda i: (i // tiles_per_mod, 0, 0)),
                  pl.BlockSpec((MIX_WIDTH, d), lambda i: (0, 0)),
                  pl.BlockSpec((1, d), lambda i: (0, 0)),
                  pl.BlockSpec((1, d), lambda i: (0, 0))],
        out_specs=[pl.BlockSpec((TM_OUT, d), lambda i: (i, 0)),
                   pl.BlockSpec((d, TM_OUT), lambda i: (0, i))],
        out_shape=[jax.ShapeDtypeStruct((n, d), F32),
                   jax.ShapeDtypeStruct((d, n), BF16)],
        compiler_params=_cp(("parallel",), 48),
        name="out_proj",
    )(oa, ob, oc, x, mod, w, ln_g.reshape(1, d), ln_b.reshape(1, d))


_PK_PAIRS = [(a, b) for a in range(PK_TOPK) for b in range(PK_TOPK) if (a + 1) * (b + 1) <= PK_TOPK]


def _merge_exchange_network(n):
    t = int(math.ceil(math.log2(n)))
    p = 2 ** (t - 1)
    pairs = []
    while p > 0:
        q, r, d = 2 ** (t - 1), 0, p
        while d > 0:
            pairs.extend((i, i + d) for i in range(n - d) if i & p == r)
            d, q, r = q - p, q // 2, p
        p //= 2
    return pairs


def _bitonic_merge_network(n):
    pairs, d = [], n // 2
    while d >= 1:
        pairs.extend((i, i + d) for i in range(n) if i & d == 0)
        d //= 2
    return pairs


_SORT16 = _merge_exchange_network(PK_TOPK)
_MERGE16 = _bitonic_merge_network(PK_TOPK)


def _compare_exchange(xs, network):
    xs = list(xs)
    for i, j in network:
        xs[i], xs[j] = jnp.maximum(xs[i], xs[j]), jnp.minimum(xs[i], xs[j])
    return xs


def _peer_score_kernel(h2t_ref, wqt_ref, keys_ref, rank2_out, e2_out, cnt_out, q1_out,
                       qt_ref, s_ref, tt_ref, thr_ref, z_ref):
    tm = h2t_ref.shape[1]
    qt_ref[...] = _dot(wqt_ref[...], h2t_ref[...]).astype(BF16)
    neg_inf = -jnp.inf
    sub = 8

    for c in range(2 * PK_HEADS):
        s = _dot(keys_ref[c], qt_ref[c * PK_NKEYS:(c + 1) * PK_NKEYS, :])
        s_ref[c] = s
        head, half = c // 2, c % 2
        xs = _compare_exchange([s[r * sub:(r + 1) * sub, :] for r in range(PK_TOPK)], _SORT16)
        shift = 1
        while shift < sub:
            ys = [pltpu.roll(xs[PK_TOPK - 1 - r], shift, 0) for r in range(PK_TOPK)]
            xs = _compare_exchange([jnp.maximum(x, y) for x, y in zip(xs, ys)], _MERGE16)
            shift *= 2
        for r in range(PK_TOPK):
            tt_ref[half, r, head:head + 1, :] = xs[r][0:1, :]

    def per_lane_chunk(ch, carry):
        l0 = pl.multiple_of(ch * LANES, LANES)
        t1 = [tt_ref[0, a, :, pl.ds(l0, LANES)] for a in range(PK_TOPK)]
        t2 = [tt_ref[1, b, :, pl.ds(l0, LANES)] for b in range(PK_TOPK)]
        cands = [t1[a] + t2[b] for a, b in _PK_PAIRS]
        cur = list(cands)
        for _ in range(PK_TOPK - 1):
            m = functools.reduce(jnp.maximum, cur)
            found = jnp.zeros(m.shape, jnp.bool_)
            nxt = []
            for cnd in cur:
                is_m = cnd == m
                nxt.append(jnp.where(jnp.logical_and(is_m, jnp.logical_not(found)), neg_inf, cnd))
                found = jnp.logical_or(found, is_m)
            cur = nxt
        thr = functools.reduce(jnp.maximum, cur)
        e1 = [jnp.exp(t1[a] - t1[0]) for a in range(PK_TOPK)]
        e2 = [jnp.exp(t2[b] - t2[0]) for b in range(PK_TOPK)]
        z = jnp.zeros(thr.shape, F32)
        for (a, b), cnd in zip(_PK_PAIRS, cands):
            z = z + jnp.where(cnd >= thr, e1[a] * e2[b], 0.0)
        thr_ref[:, pl.ds(l0, LANES)] = thr
        z_ref[:, pl.ds(l0, LANES)] = z
        return carry

    lax.fori_loop(0, tm // LANES, per_lane_chunk, 0)

    for h in range(PK_HEADS):
        s1 = s_ref[2 * h]
        s2 = s_ref[2 * h + 1]
        thr = thr_ref[h:h + 1, :]
        cnt = jnp.zeros(s1.shape, F32)
        rank2 = jnp.zeros(s2.shape, F32)
        for b in range(PK_TOPK):
            t2b = tt_ref[1, b, h:h + 1, :]
            cnt = jnp.where(s1 + t2b >= thr, float(b + 1), cnt)
            rank2 = jnp.where(t2b > s2, float(b + 1), rank2)
        cnt_out[h] = cnt
        rank2_out[h] = rank2.astype(BF16)
        q1_out[h] = jnp.exp(s1 - tt_ref[0, 0, h:h + 1, :]) * (0.5 / z_ref[h:h + 1, :])
        e2_out[h] = jnp.exp(s2 - tt_ref[1, 0, h:h + 1, :]).astype(BF16)


def _peer_score(h2t, wqt, keys):
    d, n = h2t.shape
    spec = pl.BlockSpec((PK_HEADS, PK_NKEYS, TM_SCORE), lambda i: (0, 0, i))
    shape = jax.ShapeDtypeStruct((PK_HEADS, PK_NKEYS, n), F32)
    shape16 = jax.ShapeDtypeStruct((PK_HEADS, PK_NKEYS, n), BF16)
    return pl.pallas_call(
        _peer_score_kernel,
        grid=(n // TM_SCORE,),
        in_specs=[pl.BlockSpec((d, TM_SCORE), lambda i: (0, i)),
                  pl.BlockSpec(wqt.shape, lambda i: (0, 0)),
                  pl.BlockSpec(keys.shape, lambda i: (0, 0, 0))],
        out_specs=[spec, spec, spec, spec],
        out_shape=[shape16, shape16, shape, shape],
        scratch_shapes=[pltpu.VMEM((PK_HEADS * PK_QDIM, TM_SCORE), BF16),
                        pltpu.VMEM((2 * PK_HEADS, PK_NKEYS, TM_SCORE), F32),
                        pltpu.VMEM((2, PK_TOPK, PK_HEADS, TM_SCORE), F32),
                        pltpu.VMEM((PK_HEADS, TM_SCORE), F32),
                        pltpu.VMEM((PK_HEADS, TM_SCORE), F32)],
        compiler_params=_cp(("parallel",), 48),
        name="peer_score",
    )(h2t, wqt, keys)


def _peer_dense_kernel(h2t_ref, u_ref, vt_ref, rank2_ref, e2_ref, cnt_ref, q1_ref, o_ref,
                       acc_ref, a_ref, w_ref, row_ref):
    sqrt_half = math.sqrt(0.5)

    @pl.when(pl.program_id(1) == 0)
    def _():
        acc_ref[...] = jnp.zeros_like(acc_ref)

    tm = h2t_ref.shape[1]
    zero = jnp.zeros((RJ_PEER, tm), BF16)
    a_ref[...] = _dot(u_ref[...], h2t_ref[...])
    for ii in range(TE_PEER // PK_NKEYS):
        for h in range(PK_HEADS):
            row_ref[0, h] = jnp.broadcast_to(cnt_ref[h, ii:ii + 1, :], (RJ_PEER, tm)).astype(BF16)
            row_ref[1, h] = jnp.broadcast_to(q1_ref[h, ii:ii + 1, :], (RJ_PEER, tm)).astype(BF16)
        for jc in range(PK_NKEYS // RJ_PEER):
            js = slice(jc * RJ_PEER, (jc + 1) * RJ_PEER)
            rs = slice(ii * PK_NKEYS + jc * RJ_PEER, ii * PK_NKEYS + (jc + 1) * RJ_PEER)
            g = None
            for h in range(PK_HEADS):
                term = jnp.where(rank2_ref[h, js, :] < row_ref[0, h], e2_ref[h, js, :], zero) * row_ref[1, h]
                g = term if g is None else g + term
            a = a_ref[rs, :]
            w_ref[rs, :] = g * (a * (1.0 + lax.erf(a * sqrt_half))).astype(BF16)
    acc_ref[...] += _dot(vt_ref[...], w_ref[...])

    @pl.when(pl.program_id(1) == pl.num_programs(1) - 1)
    def _():
        o_ref[...] = acc_ref[...]


def _peer_dense(h2t, u_tab, vt_tab, rank2, e2, cnt, q1):
    d, n = h2t.shape
    ne = u_tab.shape[0]
    ri = TE_PEER // PK_NKEYS
    gspec = pl.BlockSpec((PK_HEADS, PK_NKEYS, TM_PEER), lambda i, e: (0, 0, i))
    rspec = pl.BlockSpec((PK_HEADS, ri, TM_PEER), lambda i, e: (0, e, i))
    return pl.pallas_call(
        _peer_dense_kernel,
        grid=(n // TM_PEER, ne // TE_PEER),
        in_specs=[pl.BlockSpec((d, TM_PEER), lambda i, e: (0, i)),
                  pl.BlockSpec((TE_PEER, d), lambda i, e: (e, 0)),
                  pl.BlockSpec((d, TE_PEER), lambda i, e: (0, e)),
                  gspec, gspec, rspec, rspec],
        out_specs=pl.BlockSpec((d, TM_PEER), lambda i, e: (0, i)),
        out_shape=jax.ShapeDtypeStruct((d, n), F32),
        scratch_shapes=[pltpu.VMEM((d, TM_PEER), F32),
                        pltpu.VMEM((TE_PEER, TM_PEER), F32),
                        pltpu.VMEM((TE_PEER, TM_PEER), BF16),
                        pltpu.VMEM((2, PK_HEADS, RJ_PEER, TM_PEER), BF16)],
        compiler_params=_cp(("parallel", "arbitrary"), 56),
        name="peer_dense",
    )(h2t, u_tab, vt_tab, rank2, e2, cnt, q1)


def _peer_ln_kernel(yt_ref, x1_ref, mod_ref, lg_ref, lb_ref, o_ref):
    g2 = mod_ref[0, 5:6, :]
    o_ref[...] = _layernorm(DN_ALPHA * x1_ref[...] + g2 * yt_ref[...].T, lg_ref[...], lb_ref[...])


def _peer_ln(yt, x1, mod, ln_g, ln_b, tiles_per_mod):
    n, d = x1.shape
    return pl.pallas_call(
        _peer_ln_kernel,
        grid=(n // TM_OUT,),
        in_specs=[pl.BlockSpec((d, TM_OUT), lambda i: (0, i)),
                  pl.BlockSpec((TM_OUT, d), lambda i: (i, 0)),
                  pl.BlockSpec((1, 6, d), lambda i: (i // tiles_per_mod, 0, 0)),
                  pl.BlockSpec((1, d), lambda i: (0, 0)),
                  pl.BlockSpec((1, d), lambda i: (0, 0))],
        out_specs=pl.BlockSpec((TM_OUT, d), lambda i: (i, 0)),
        out_shape=jax.ShapeDtypeStruct((n, d), F32),
        compiler_params=_cp(("parallel",), 40),
        name="peer_ln",
    )(yt, x1, mod, ln_g.reshape(1, d), ln_b.reshape(1, d))


def _permute_in_proj(w):
    d = w.shape[0]

    def interleave(cols):
        c = cols.reshape(d, 2, DF_HEADS, 2, DF_QK // 2)
        return c.transpose(0, 2, 3, 1, 4).reshape(d, 2 * DF_HEADS * DF_QK)

    return jnp.concatenate([w[:, :OFF_DQ], interleave(w[:, OFF_DQ:OFF_DK]),
                            interleave(w[:, OFF_DK:OFF_DV]), w[:, OFF_DV:]], axis=1).astype(BF16)


def _pair_lanes(a, b):
    half = DF_QK // 2
    return jnp.concatenate([a[..., :half], b[..., :half], a[..., half:], b[..., half:]], -1)


def _unpair_lanes(kk):
    half = DF_QK // 2
    a = jnp.concatenate([kk[..., 0:half], kk[..., 2 * half:3 * half]], -1)
    b = jnp.concatenate([kk[..., half:2 * half], kk[..., 3 * half:]], -1)
    return a, b


def _rope_tables(t):
    tok = jnp.arange(t)
    row = (tok // GRID_W).astype(F32)
    col = (tok % GRID_W).astype(F32)
    n_freq = DF_QK // 4
    inv = 1.0 / (ROPE_THETA ** (jnp.arange(n_freq, dtype=F32) / n_freq))
    ang = jnp.concatenate([row[:, None] * inv, col[:, None] * inv], -1)
    cos, sin = jnp.cos(ang), jnp.sin(ang)
    return jnp.concatenate([cos] * 4, -1), jnp.concatenate([-sin, -sin, sin, sin], -1)


def _na_bias_table(rpb, rows):
    cases = _na_group_layout(rows)
    cols = np.arange(GRID_W)
    start = np.clip(cols - NA_WIN_C // 2, 0, GRID_W - NA_WIN_C)
    inwin = (cols[None, :] >= start[:, None]) & (cols[None, :] < start[:, None] + NA_WIN_C)
    nh = rpb.shape[0]
    dr = np.zeros((len(cases), NA_QROWS, NA_BAND), np.int32)
    valid = np.zeros((len(cases), NA_QROWS, NA_BAND), bool)
    for c, pat in enumerate(cases):
        for a, (off, cs) in enumerate(pat):
            for kr in range(off, off + NA_WIN_R):
                dr[c, a, kr] = kr - off - cs + NA_WIN_R - 1
                valid[c, a, kr] = True
    sel = rpb[:, dr.reshape(-1), :].reshape(nh, len(cases), NA_QROWS, NA_BAND, 2 * NA_WIN_C - 1)
    lead = GRID_W - NA_WIN_C
    p = jnp.pad(sel, ((0, 0),) * 4 + ((lead, 2 * GRID_W - lead - (2 * NA_WIN_C - 1)),))
    m = jnp.tile(p, (1, 1, 1, 1, GRID_W))[..., :GRID_W * (2 * GRID_W - 1)]
    m = m.reshape(nh, len(cases), NA_QROWS, NA_BAND, GRID_W, 2 * GRID_W - 1)
    tab = m[..., GRID_W - 1:].transpose(0, 1, 2, 4, 3, 5)
    keep = valid[None, :, :, None, :, None] & inwin[None, None, None, :, None, :]
    tab = jnp.where(keep, tab, NEG_BIG)
    return tab.reshape(nh, len(cases), NA_QROWS * GRID_W, NA_BAND * GRID_W)


def _heads_out(a, b, t, h):
    return a.reshape(b, t, h, -1).transpose(0, 2, 1, 3)


def _peer_block(oa, ob, oc, x, mod, tiles_out, w_out, ln_g, ln_b, wqt, keys, u_tab, vt_tab):
    x1, h2t = _out_proj(oa, ob, oc, x, mod, w_out, ln_g[0], ln_b[0], tiles_out)
    rank2, e2, cnt, q1 = _peer_score(h2t, wqt, keys)
    yt = _peer_dense(h2t, u_tab, vt_tab, rank2, e2, cnt, q1)
    return _peer_ln(yt, x1, mod, ln_g[1], ln_b[1], tiles_out)


def kernel(x_prompt, x_sample, cache_na_k, cache_na_v, cache_df_k1, cache_df_k2, cache_df_v, c, c_ctx,
           w_mod, b_mod, w_in, na_rpb, sg_ln_g, sg_ln_b, sg_w, sg_b, df_lambda, df_subln_g, w_out,
           pk_wq, pk_keys, pk_u, pk_v, ln_g, ln_b):
    bp, tp, d = x_prompt.shape
    bs, ts, _ = x_sample.shape
    n_p, n_s = bp * tp, bs * ts

    cond8 = jnp.zeros((8, d), F32).at[0].set(c_ctx).at[1:1 + bs].set(c)
    mods = _modulation(cond8, w_mod, b_mod).reshape(DEPTH, 8, 6, d)

    cos_t, sin_t = _rope_tables(ts)
    cache_dk = _pair_lanes(cache_df_k1, cache_df_k2)

    xp = x_prompt.reshape(n_p, d)
    xs = x_sample.reshape(n_s, d)
    st = [[] for _ in range(5)]
    for l in range(DEPTH):
        lam_init = 0.8 - 0.6 * math.exp(-0.3 * l)
        w_in_l = _permute_in_proj(w_in[l])
        w_out_l = w_out[l].astype(BF16)
        wqt = pk_wq[l].T.astype(BF16)
        keys = pk_keys[l].reshape(2 * PK_HEADS, PK_NKEYS, PK_QDIM // 2).astype(BF16)
        u_tab = pk_u[l].astype(BF16)
        vt_tab = pk_v[l].T.astype(BF16)
        ws = sg_w[l].astype(BF16)
        bs_col = sg_b[l].reshape(SG_GROUPS, SG_CHUNK, 1)
        bias = _na_bias_table(na_rpb[l], ts // GRID_W)
        mod_p = mods[l, 0:1]
        mod_s = mods[l, 1:1 + bs]

        proj = _in_proj(xp, mod_p, w_in_l, n_p // TM_IN)
        proj3 = proj.reshape(bp, tp, IN_WIDTH)
        oa, oc = _ctx_attn(proj3, df_lambda[l], df_subln_g[l], lam_init)
        ob = _spatial_gate(proj3, sg_ln_g[l], sg_ln_b[l], ws, bs_col)
        st[0].append(_heads_out(proj[:, OFF_KA:OFF_KA + NA_WIDTH], bp, tp, NA_HEADS))
        st[1].append(_heads_out(proj[:, OFF_VA:OFF_VA + NA_WIDTH], bp, tp, NA_HEADS))
        k1, k2 = _unpair_lanes(_heads_out(proj[:, OFF_DK:OFF_DV], bp, tp, DF_HEADS))
        st[2].append(k1)
        st[3].append(k2)
        st[4].append(_heads_out(proj[:, OFF_DV:], bp, tp, DF_HEADS))
        xp = _peer_block(oa.reshape(n_p, NA_WIDTH), ob.reshape(n_p, SG_WIDTH), oc.reshape(n_p, DF_WIDTH),
                         xp, mod_p, n_p // TM_OUT, w_out_l, ln_g[l], ln_b[l], wqt, keys, u_tab, vt_tab)

        proj = _in_proj(xs, mod_s, w_in_l, ts // TM_IN)
        proj3 = proj.reshape(bs, ts, IN_WIDTH)
        oa = _lat_na(proj3, cache_na_k, cache_na_v, bias, l)
        ob = _spatial_gate(proj3, sg_ln_g[l], sg_ln_b[l], ws, bs_col)
        oc = _lat_df(proj3, cache_dk, cache_df_v, cos_t, sin_t, df_lambda[l], df_subln_g[l], lam_init, l)
        xs = _peer_block(oa.reshape(n_s, NA_WIDTH), ob.reshape(n_s, SG_WIDTH), oc.reshape(n_s, DF_WIDTH),
                         xs, mod_s, ts // TM_OUT, w_out_l, ln_g[l], ln_b[l], wqt, keys, u_tab, vt_tab)

    outs = [jnp.stack(s, 1) for s in st]
    return (xp.reshape(bp, tp, d), xs.reshape(bs, ts, d), *outs)
```

```python
import functools
import math

import numpy as np
import jax
import jax.numpy as jnp
from jax import lax
from jax.experimental import pallas as pl
from jax.experimental.pallas import tpu as pltpu

F32 = jnp.float32
BF16 = jnp.bfloat16

D_MODEL = 2048
DEPTH = 4
GRID_W = 64
NA_HEADS = 8
NA_DIM = 128
NA_WIN_R = 8
NA_WIN_C = 16
SG_GROUPS = 4
SG_DIM = 128
SG_CHUNK = 128
DF_HEADS = 4
DF_QK = 64
DF_V = 128
ROPE_THETA = 10000.0
NA_WIDTH = NA_HEADS * NA_DIM
SG_WIDTH = SG_GROUPS * SG_DIM
DF_WIDTH = DF_HEADS * DF_V
MIX_WIDTH = NA_WIDTH + SG_WIDTH + DF_WIDTH
IN_WIDTH = 3 * NA_WIDTH + 2 * SG_WIDTH + 4 * DF_HEADS * DF_QK + DF_WIDTH
PK_HEADS = 8
PK_QDIM = 256
PK_NKEYS = 128
PK_TOPK = 16
PK_EXPERTS = PK_NKEYS * PK_NKEYS
DN_ALPHA = (2 * DEPTH) ** 0.25
LN_EPS = 1e-5
RMS_EPS = 1e-6

OFF_QA = 0
OFF_KA = NA_WIDTH
OFF_VA = 2 * NA_WIDTH
OFF_U = 3 * NA_WIDTH
OFF_V = OFF_U + SG_WIDTH
OFF_DQ = OFF_V + SG_WIDTH
OFF_DK = OFF_DQ + 2 * DF_HEADS * DF_QK
OFF_DV = OFF_DK + 2 * DF_HEADS * DF_QK

LANES = 128
MIB = 1024 * 1024

TM_IN = 512
TN_IN = IN_WIDTH // 4
TM_OUT = 512
TM_SCORE = 256
TM_PEER = 512
TE_PEER = 1024
RJ_PEER = 16
TQ_DF = 256
TN_MOD = 1024
NEG_BIG = -1e30


def _cp(sem, vmem_mib):
    return pltpu.CompilerParams(dimension_semantics=sem, vmem_limit_bytes=vmem_mib * MIB)


def _dot(a, b):
    return jnp.dot(a, b, preferred_element_type=F32)


def _dot_nt(a, b):
    return lax.dot_general(a, b, (((1,), (1,)), ((), ())), preferred_element_type=F32)


def _layernorm(z, g, b):
    mu = jnp.mean(z, -1, keepdims=True)
    d = z - mu
    var = jnp.mean(d * d, -1, keepdims=True)
    return d * lax.rsqrt(var + LN_EPS) * g + b


def _diff_probs(s1, s2, lam):
    e1 = jnp.exp(s1 - jnp.max(s1, -1, keepdims=True))
    e2 = jnp.exp(s2 - jnp.max(s2, -1, keepdims=True))
    c1 = 1.0 / jnp.sum(e1, -1, keepdims=True)
    c2 = lam / jnp.sum(e2, -1, keepdims=True)
    return e1 * c1 - e2 * c2


def _softmax_rows(s):
    m = jnp.max(s, -1, keepdims=True)
    e = jnp.exp(s - m)
    return e * (1.0 / jnp.sum(e, -1, keepdims=True))


def _diff_lambda(lam_ref, lam_init):
    lf = lam_ref[...]
    a = jnp.sum(lf[0:1, :] * lf[1:2, :], axis=1, keepdims=True)
    b = jnp.sum(lf[2:3, :] * lf[3:4, :], axis=1, keepdims=True)
    return jnp.exp(a) - jnp.exp(b) + lam_init


def _map1_mask():
    lane = lax.broadcasted_iota(jnp.int32, (1, LANES), 1)
    return (lane // (DF_QK // 2)) % 2 == 0


def _subln(o, g, lam_init):
    return o * lax.rsqrt(jnp.mean(o * o, -1, keepdims=True) + RMS_EPS) * g * (1.0 - lam_init)


def _mod_kernel(c_ref, w_ref, b_ref, o_ref):
    c = c_ref[...]
    a = (c * jax.nn.sigmoid(c)).astype(BF16)
    o_ref[0] = _dot(a, w_ref[0].astype(BF16)) + b_ref[0]


def _modulation(cond8, w_mod, b_mod):
    depth, d, n6 = w_mod.shape
    return pl.pallas_call(
        _mod_kernel,
        grid=(depth, n6 // TN_MOD),
        in_specs=[pl.BlockSpec((8, d), lambda l, j: (0, 0)),
                  pl.BlockSpec((1, d, TN_MOD), lambda l, j: (l, 0, j)),
                  pl.BlockSpec((1, 1, TN_MOD), lambda l, j: (l, 0, j))],
        out_specs=pl.BlockSpec((1, 8, TN_MOD), lambda l, j: (l, 0, j)),
        out_shape=jax.ShapeDtypeStruct((depth, 8, n6), F32),
        compiler_params=_cp(("parallel", "parallel"), 32),
        name="modulation",
    )(cond8, w_mod, b_mod.reshape(depth, 1, n6))


def _in_proj_kernel(x_ref, mod_ref, w_ref, o_ref, h_ref):
    @pl.when(pl.program_id(1) == 0)
    def _():
        sh = mod_ref[0, 0:1, :]
        sc = mod_ref[0, 1:2, :]
        h_ref[...] = (x_ref[...] * (1 + sc) + sh).astype(BF16)

    o_ref[...] = _dot(h_ref[...], w_ref[...])


def _in_proj(x, mod, w, tiles_per_mod):
    n, d = x.shape
    nw = w.shape[1]
    return pl.pallas_call(
        _in_proj_kernel,
        grid=(n // TM_IN, nw // TN_IN),
        in_specs=[pl.BlockSpec((TM_IN, d), lambda i, j: (i, 0)),
                  pl.BlockSpec((1, 6, d), lambda i, j: (i // tiles_per_mod, 0, 0)),
                  pl.BlockSpec((d, TN_IN), lambda i, j: (0, j))],
        out_specs=pl.BlockSpec((TM_IN, TN_IN), lambda i, j: (i, j)),
        out_shape=jax.ShapeDtypeStruct((n, nw), F32),
        scratch_shapes=[pltpu.VMEM((TM_IN, d), BF16)],
        compiler_params=_cp(("parallel", "arbitrary"), 40),
        name="in_proj",
    )(x, mod, w)


def _ctx_attn_kernel(p_ref, lam_ref, g_ref, oa_ref, oc_ref, *, lam_init):
    scale = NA_DIM ** -0.5
    for h in range(NA_HEADS):
        sl = slice(h * NA_DIM, (h + 1) * NA_DIM)
        q = p_ref[0, :, OFF_QA + h * NA_DIM:OFF_QA + (h + 1) * NA_DIM].astype(BF16)
        k = p_ref[0, :, OFF_KA + h * NA_DIM:OFF_KA + (h + 1) * NA_DIM].astype(BF16)
        v = p_ref[0, :, OFF_VA + h * NA_DIM:OFF_VA + (h + 1) * NA_DIM].astype(BF16)
        p = _softmax_rows(_dot_nt(q, k) * scale)
        oa_ref[0, :, sl] = _dot(p.astype(BF16), v).astype(BF16)
    lam = _diff_lambda(lam_ref, lam_init)
    m1 = _map1_mask()
    dscale = DF_QK ** -0.5
    for h in range(DF_HEADS):
        sl = slice(h * DF_V, (h + 1) * DF_V)
        q = p_ref[0, :, OFF_DQ + h * LANES:OFF_DQ + (h + 1) * LANES] * dscale
        k = p_ref[0, :, OFF_DK + h * LANES:OFF_DK + (h + 1) * LANES].astype(BF16)
        v = p_ref[0, :, OFF_DV + h * DF_V:OFF_DV + (h + 1) * DF_V].astype(BF16)
        qa = jnp.where(m1, q, 0.0).astype(BF16)
        qb = jnp.where(m1, 0.0, q).astype(BF16)
        p = _diff_probs(_dot_nt(qa, k), _dot_nt(qb, k), lam)
        o = _dot(p.astype(BF16), v)
        oc_ref[0, :, sl] = _subln(o, g_ref[...], lam_init).astype(BF16)


def _ctx_attn(proj3, lam_p, subln_g, lam_init):
    b, t, nw = proj3.shape
    return pl.pallas_call(
        functools.partial(_ctx_attn_kernel, lam_init=lam_init),
        grid=(b,),
        in_specs=[pl.BlockSpec((1, t, nw), lambda i: (i, 0, 0)),
                  pl.BlockSpec((4, DF_QK), lambda i: (0, 0)),
                  pl.BlockSpec((1, DF_V), lambda i: (0, 0))],
        out_specs=[pl.BlockSpec((1, t, NA_WIDTH), lambda i: (i, 0, 0)),
                   pl.BlockSpec((1, t, DF_WIDTH), lambda i: (i, 0, 0))],
        out_shape=[jax.ShapeDtypeStruct((b, t, NA_WIDTH), BF16),
                   jax.ShapeDtypeStruct((b, t, DF_WIDTH), BF16)],
        compiler_params=_cp(("parallel",), 32),
        name="ctx_attn",
    )(proj3, lam_p, subln_g.reshape(1, DF_V))


NA_QROWS = 4
NA_BAND = NA_WIN_R + NA_QROWS


def _na_group_layout(rows):
    groups = []
    for g in range(rows // NA_QROWS):
        bs = int(np.clip(g * NA_QROWS - NA_WIN_R // 2, 0, rows - NA_BAND))
        pat = []
        for a in range(NA_QROWS):
            r = g * NA_QROWS + a
            r0 = int(np.clip(r - NA_WIN_R // 2, 0, rows - NA_WIN_R))
            pat.append((r0 - bs, r - r0))
        groups.append((bs, tuple(pat)))
    cases = [groups[0][1], groups[1][1], groups[-1][1]]
    assert all(p == cases[1] for _, p in groups[1:-1]) and rows >= 2 * NA_BAND
    assert all(0 <= off and off + NA_WIN_R <= NA_BAND for c in cases for off, _ in c)
    return cases


def _lat_na_kernel(q_ref, k_ref, v_ref, kc_ref, vc_ref, bias_ref, o_ref, kb_ref, vb_ref, *, rows):
    kb_ref[...] = k_ref[0].astype(BF16)
    vb_ref[...] = v_ref[0].astype(BF16)
    kc = kc_ref[0, 0, 0].astype(BF16)
    vc = vc_ref[0, 0, 0].astype(BF16)
    scale = NA_DIM ** -0.5
    nq = NA_QROWS * GRID_W
    band = NA_BAND * GRID_W
    ngroups = rows // NA_QROWS

    def body(g, carry):
        bs = jnp.clip(g * NA_QROWS - NA_WIN_R // 2, 0, rows - NA_BAND)
        case = jnp.where(g == 0, 0, jnp.where(g == ngroups - 1, 2, 1))
        q0 = pl.multiple_of(g * nq, nq)
        q = q_ref[0, pl.ds(q0, nq), :].astype(BF16)
        k0 = pl.multiple_of(bs * GRID_W, GRID_W)
        kw = kb_ref[pl.ds(k0, band), :]
        vw = vb_ref[pl.ds(k0, band), :]
        s_loc = _dot_nt(q, kw) * scale + bias_ref[0, case]
        s_ctx = _dot_nt(q, kc) * scale
        m = jnp.maximum(jnp.max(s_loc, -1, keepdims=True), jnp.max(s_ctx, -1, keepdims=True))
        e_loc = jnp.exp(s_loc - m)
        e_ctx = jnp.exp(s_ctx - m)
        inv = 1.0 / (jnp.sum(e_loc, -1, keepdims=True) + jnp.sum(e_ctx, -1, keepdims=True))
        o = _dot((e_loc * inv).astype(BF16), vw) + _dot((e_ctx * inv).astype(BF16), vc)
        o_ref[0, pl.ds(q0, nq), :] = o.astype(BF16)
        return carry

    lax.fori_loop(0, ngroups, body, 0)


def _lat_na(proj3, cache_k, cache_v, bias, layer):
    b, t, _ = proj3.shape
    past = cache_k.shape[3]
    rows = t // GRID_W
    qb, kb, vb = OFF_QA // NA_DIM, OFF_KA // NA_DIM, OFF_VA // NA_DIM
    return pl.pallas_call(
        functools.partial(_lat_na_kernel, rows=rows),
        grid=(b, NA_HEADS),
        in_specs=[pl.BlockSpec((1, t, NA_DIM), lambda i, h: (i, 0, qb + h)),
                  pl.BlockSpec((1, t, NA_DIM), lambda i, h: (i, 0, kb + h)),
                  pl.BlockSpec((1, t, NA_DIM), lambda i, h: (i, 0, vb + h)),
                  pl.BlockSpec((1, 1, 1, past, NA_DIM), lambda i, h: (i, layer, h, 0, 0)),
                  pl.BlockSpec((1, 1, 1, past, NA_DIM), lambda i, h: (i, layer, h, 0, 0)),
                  pl.BlockSpec((1, 3, NA_QROWS * GRID_W, NA_BAND * GRID_W), lambda i, h: (h, 0, 0, 0))],
        out_specs=pl.BlockSpec((1, t, NA_DIM), lambda i, h: (i, 0, h)),
        out_shape=jax.ShapeDtypeStruct((b, t, NA_WIDTH), BF16),
        scratch_shapes=[pltpu.VMEM((t, NA_DIM), BF16), pltpu.VMEM((t, NA_DIM), BF16)],
        compiler_params=_cp(("parallel", "parallel"), 40),
        name="lat_na",
    )(proj3, proj3, proj3, cache_k, cache_v, bias)


def _lat_df_kernel(q_ref, k_ref, v_ref, kc_ref, vc_ref, cos_ref, sin_ref, lam_ref, g_ref, o_ref,
                   kall_ref, vall_ref, *, t, lam_init):
    qi = pl.program_id(2)

    @pl.when(qi == 0)
    def _():
        k = k_ref[0]
        kall_ref[0:t, :] = (k * cos_ref[...] + pltpu.roll(k, LANES // 2, 1) * sin_ref[...]).astype(BF16)
        kall_ref[t:, :] = kc_ref[0, 0, 0].astype(BF16)
        vall_ref[0:t, :] = v_ref[0].astype(BF16)
        vall_ref[t:, :] = vc_ref[0, 0, 0].astype(BF16)

    t0 = pl.multiple_of(qi * TQ_DF, TQ_DF)
    q = q_ref[0]
    q = q * cos_ref[pl.ds(t0, TQ_DF), :] + pltpu.roll(q, LANES // 2, 1) * sin_ref[pl.ds(t0, TQ_DF), :]
    q = q * DF_QK ** -0.5
    m1 = _map1_mask()
    qa = jnp.where(m1, q, 0.0).astype(BF16)
    qb = jnp.where(m1, 0.0, q).astype(BF16)
    lam = _diff_lambda(lam_ref, lam_init)
    kall = kall_ref[...]
    p = _diff_probs(_dot_nt(qa, kall), _dot_nt(qb, kall), lam)
    o = _dot(p.astype(BF16), vall_ref[...])
    o_ref[0] = _subln(o, g_ref[...], lam_init).astype(BF16)


def _lat_df(proj3, cache_kp, cache_v, cos_t, sin_t, lam_p, subln_g, lam_init, layer):
    b, t, _ = proj3.shape
    past = cache_v.shape[3]
    qb, kb, vb = OFF_DQ // LANES, OFF_DK // LANES, OFF_DV // LANES
    return pl.pallas_call(
        functools.partial(_lat_df_kernel, t=t, lam_init=lam_init),
        grid=(b, DF_HEADS, t // TQ_DF),
        in_specs=[pl.BlockSpec((1, TQ_DF, LANES), lambda i, h, j: (i, j, qb + h)),
                  pl.BlockSpec((1, t, LANES), lambda i, h, j: (i, 0, kb + h)),
                  pl.BlockSpec((1, t, LANES), lambda i, h, j: (i, 0, vb + h)),
                  pl.BlockSpec((1, 1, 1, past, LANES), lambda i, h, j: (i, layer, h, 0, 0)),
                  pl.BlockSpec((1, 1, 1, past, DF_V), lambda i, h, j: (i, layer, h, 0, 0)),
                  pl.BlockSpec((t, LANES), lambda i, h, j: (0, 0)),
                  pl.BlockSpec((t, LANES), lambda i, h, j: (0, 0)),
                  pl.BlockSpec((4, DF_QK), lambda i, h, j: (0, 0)),
                  pl.BlockSpec((1, DF_V), lambda i, h, j: (0, 0))],
        out_specs=pl.BlockSpec((1, TQ_DF, DF_V), lambda i, h, j: (i, j, h)),
        out_shape=jax.ShapeDtypeStruct((b, t, DF_WIDTH), BF16),
        scratch_shapes=[pltpu.VMEM((t + past, LANES), BF16), pltpu.VMEM((t + past, DF_V), BF16)],
        compiler_params=_cp(("parallel", "parallel", "arbitrary"), 48),
        name="lat_df",
    )(proj3, proj3, proj3, cache_kp, cache_v, cos_t, sin_t, lam_p, subln_g.reshape(1, DF_V))


def _sg_kernel(u_ref, v_ref, g_ref, b_ref, ws_ref, bs_ref, o_ref, *, nch):
    for c in range(nch):
        rs = slice(c * SG_CHUNK, (c + 1) * SG_CHUNK)
        vn = _layernorm(v_ref[0, rs, :], g_ref[...], b_ref[...]).astype(BF16)
        for g in range(SG_GROUPS):
            cs = slice(g * SG_DIM, (g + 1) * SG_DIM)
            mixed = _dot(ws_ref[g], vn[:, cs]) + bs_ref[g]
            o_ref[0, rs, cs] = (u_ref[0, rs, cs] * mixed).astype(BF16)


def _spatial_gate(proj3, ln_g, ln_b, ws, bs):
    b, t, _ = proj3.shape
    nch = min(4, t // SG_CHUNK)
    tt = nch * SG_CHUNK
    ub, vb = OFF_U // SG_WIDTH, OFF_V // SG_WIDTH
    return pl.pallas_call(
        functools.partial(_sg_kernel, nch=nch),
        grid=(b, t // tt),
        in_specs=[pl.BlockSpec((1, tt, SG_WIDTH), lambda i, j: (i, j, ub)),
                  pl.BlockSpec((1, tt, SG_WIDTH), lambda i, j: (i, j, vb)),
                  pl.BlockSpec((1, SG_WIDTH), lambda i, j: (0, 0)),
                  pl.BlockSpec((1, SG_WIDTH), lambda i, j: (0, 0)),
                  pl.BlockSpec((SG_GROUPS, SG_CHUNK, SG_CHUNK), lambda i, j: (0, 0, 0)),
                  pl.BlockSpec((SG_GROUPS, SG_CHUNK, 1), lambda i, j: (0, 0, 0))],
        out_specs=pl.BlockSpec((1, tt, SG_WIDTH), lambda i, j: (i, j, 0)),
        out_shape=jax.ShapeDtypeStruct((b, t, SG_WIDTH), BF16),
        compiler_params=_cp(("parallel", "parallel"), 16),
        name="spatial_gate",
    )(proj3, proj3, ln_g.reshape(1, SG_WIDTH), ln_b.reshape(1, SG_WIDTH), ws, bs)


def _out_proj_kernel(oa_ref, ob_ref, oc_ref, x_ref, mod_ref, w_ref, lg_ref, lb_ref, x1_ref, h2t_ref):
    y = (_dot(oa_ref[...], w_ref[0:NA_WIDTH, :])
         + _dot(ob_ref[...], w_ref[NA_WIDTH:NA_WIDTH + SG_WIDTH, :])
         + _dot(oc_ref[...], w_ref[NA_WIDTH + SG_WIDTH:, :]))
    g1 = mod_ref[0, 2:3, :]
    sh2 = mod_ref[0, 3:4, :]
    sc2 = mod_ref[0, 4:5, :]
    x1 = _layernorm(DN_ALPHA * x_ref[...] + g1 * y, lg_ref[...], lb_ref[...])
    x1_ref[...] = x1
    h2t_ref[...] = (x1 * (1 + sc2) + sh2).T.astype(BF16)


def _out_proj(oa, ob, oc, x, mod, w, ln_g, ln_b, tiles_per_mod):
    n, d = x.shape
    return pl.pallas_call(
        _out_proj_kernel,
        grid=(n // TM_OUT,),
        in_specs=[pl.BlockSpec((TM_OUT, NA_WIDTH), lambda i: (i, 0)),
                  pl.BlockSpec((TM_OUT, SG_WIDTH), lambda i: (i, 0)),
                  pl.BlockSpec((TM_OUT, DF_WIDTH), lambda i: (i, 0)),
                  pl.BlockSpec((TM_OUT, d), lambda i: (i, 0)),
                  pl.BlockSpec((1, 6, d), lambda i: (i // tiles_per_mod, 0, 0)),
                  pl.BlockSpec((MIX_WIDTH, d), lambda i: (0, 0)),
                  pl.BlockSpec((1, d), lambda i: (0, 0)),
                  pl.BlockSpec((1, d), lambda i: (0, 0))],
        out_specs=[pl.BlockSpec((TM_OUT, d), lambda i: (i, 0)),
                   pl.BlockSpec((d, TM_OUT), lambda i: (0, i))],
        out_shape=[jax.ShapeDtypeStruct((n, d), F32),
                   jax.ShapeDtypeStruct((d, n), BF16)],
        compiler_params=_cp(("parallel",), 48),
        name="out_proj",
    )(oa, ob, oc, x, mod, w, ln_g.reshape(1, d), ln_b.reshape(1, d))


_PK_PAIRS = [(a, b) for a in range(PK_TOPK) for b in range(PK_TOPK) if (a + 1) * (b + 1) <= PK_TOPK]


def _merge_exchange_network(n):
    t = int(math.ceil(math.log2(n)))
    p = 2 ** (t - 1)
    pairs = []
    while p > 0:
        q, r, d = 2 ** (t - 1), 0, p
        while d > 0:
            pairs.extend((i, i + d) for i in range(n - d) if i & p == r)
            d, q, r = q - p, q // 2, p
        p //= 2
    return pairs


def _bitonic_merge_network(n):
    pairs, d = [], n // 2
    while d >= 1:
        pairs.extend((i, i + d) for i in range(n) if i & d == 0)
        d //= 2
    return pairs


_SORT16 = _merge_exchange_network(PK_TOPK)
_MERGE16 = _bitonic_merge_network(PK_TOPK)


def _compare_exchange(xs, network):
    xs = list(xs)
    for i, j in network:
        xs[i], xs[j] = jnp.maximum(xs[i], xs[j]), jnp.minimum(xs[i], xs[j])
    return xs


def _peer_score_kernel(h2t_ref, wqt_ref, keys_ref, rank2_out, e2_out, cnt_out, q1_out,
                       qt_ref, s_ref, tt_ref, thr_ref, z_ref):
    tm = h2t_ref.shape[1]
    qt_ref[...] = _dot(wqt_ref[...], h2t_ref[...]).astype(BF16)
    neg_inf = -jnp.inf
    sub = 8

    for c in range(2 * PK_HEADS):
        s = _dot(keys_ref[c], qt_ref[c * PK_NKEYS:(c + 1) * PK_NKEYS, :])
        s_ref[c] = s
        head, half = c // 2, c % 2
        xs = _compare_exchange([s[r * sub:(r + 1) * sub, :] for r in range(PK_TOPK)], _SORT16)
        shift = 1
        while shift < sub:
            ys = [pltpu.roll(xs[PK_TOPK - 1 - r], shift, 0) for r in range(PK_TOPK)]
            xs = _compare_exchange([jnp.maximum(x, y) for x, y in zip(xs, ys)], _MERGE16)
            shift *= 2
        for r in range(PK_TOPK):
            tt_ref[half, r, head:head + 1, :] = xs[r][0:1, :]

    def per_lane_chunk(ch, carry):
        l0 = ch * LANES
        t1 = [tt_ref[0, a, :, pl.ds(l0, LANES)] for a in range(PK_TOPK)]
        t2 = [tt_ref[1, b, :, pl.ds(l0, LANES)] for b in range(PK_TOPK)]
        cands = [t1[a] + t2[b] for a, b in _PK_PAIRS]
        cur = list(cands)
        for _ in range(PK_TOPK - 1):
            m = functools.reduce(jnp.maximum, cur)
            found = jnp.zeros(m.shape, jnp.bool_)
            nxt = []
            for cnd in cur:
                is_m = cnd == m
                nxt.append(jnp.where(jnp.logical_and(is_m, jnp.logical_not(found)), neg_inf, cnd))
                found = jnp.logical_or(found, is_m)
            cur = nxt
        thr = functools.reduce(jnp.maximum, cur)
        e1 = [jnp.exp(t1[a] - t1[0]) for a in range(PK_TOPK)]
        e2 = [jnp.exp(t2[b] - t2[0]) for b in range(PK_TOPK)]
        z = jnp.zeros(thr.shape, F32)
        for (a, b), cnd in zip(_PK_PAIRS, cands):
            z = z + jnp.where(cnd >= thr, e1[a] * e2[b], 0.0)
        thr_ref[:, pl.ds(l0, LANES)] = thr
        z_ref[:, pl.ds(l0, LANES)] = z
        return carry

    for ch in range(tm // LANES):
        per_lane_chunk(ch, 0)

    for h in range(PK_HEADS):
        s1 = s_ref[2 * h]
        s2 = s_ref[2 * h + 1]
        thr = thr_ref[h:h + 1, :]
        cnt = jnp.zeros(s1.shape, F32)
        rank2 = jnp.zeros(s2.shape, F32)
        for b in range(PK_TOPK):
            t2b = tt_ref[1, b, h:h + 1, :]
            cnt = jnp.where(s1 + t2b >= thr, float(b + 1), cnt)
            rank2 = jnp.where(t2b > s2, float(b + 1), rank2)
        cnt_out[h] = cnt
        rank2_out[h] = rank2.astype(BF16)
        q1_out[h] = jnp.exp(s1 - tt_ref[0, 0, h:h + 1, :]) * (0.5 / z_ref[h:h + 1, :])
        e2_out[h] = jnp.exp(s2 - tt_ref[1, 0, h:h + 1, :]).astype(BF16)


def _peer_score(h2t, wqt, keys):
    d, n = h2t.shape
    spec = pl.BlockSpec((PK_HEADS, PK_NKEYS, TM_SCORE), lambda i: (0, 0, i))
    shape = jax.ShapeDtypeStruct((PK_HEADS, PK_NKEYS, n), F32)
    shape16 = jax.ShapeDtypeStruct((PK_HEADS, PK_NKEYS, n), BF16)
    return pl.pallas_call(
        _peer_score_kernel,
        grid=(n // TM_SCORE,),
        in_specs=[pl.BlockSpec((d, TM_SCORE), lambda i: (0, i)),
                  pl.BlockSpec(wqt.shape, lambda i: (0, 0)),
                  pl.BlockSpec(keys.shape, lambda i: (0, 0, 0))],
        out_specs=[spec, spec, spec, spec],
        out_shape=[shape16, shape16, shape, shape],
        scratch_shapes=[pltpu.VMEM((PK_HEADS * PK_QDIM, TM_SCORE), BF16),
                        pltpu.VMEM((2 * PK_HEADS, PK_NKEYS, TM_SCORE), F32),
                        pltpu.VMEM((2, PK_TOPK, PK_HEADS, TM_SCORE), F32),
                        pltpu.VMEM((PK_HEADS, TM_SCORE), F32),
                        pltpu.VMEM((PK_HEADS, TM_SCORE), F32)],
        compiler_params=_cp(("parallel",), 48),
        name="peer_score",
    )(h2t, wqt, keys)


def _peer_dense_kernel(h2t_ref, u_ref, vt_ref, rank2_ref, e2_ref, cnt_ref, q1_ref, o_ref,
                       acc_ref, a_ref, w_ref, row_ref):
    sqrt_half = math.sqrt(0.5)

    @pl.when(pl.program_id(1) == 0)
    def _():
        acc_ref[...] = jnp.zeros_like(acc_ref)

    tm = h2t_ref.shape[1]
    zero = jnp.zeros((RJ_PEER, tm), BF16)
    a_ref[...] = _dot(u_ref[...], h2t_ref[...])
    for ii in range(TE_PEER // PK_NKEYS):
        for h in range(PK_HEADS):
            row_ref[0, h] = jnp.broadcast_to(cnt_ref[h, ii:ii + 1, :], (RJ_PEER, tm)).astype(BF16)
            row_ref[1, h] = jnp.broadcast_to(q1_ref[h, ii:ii + 1, :], (RJ_PEER, tm)).astype(BF16)
        for jc in range(PK_NKEYS // RJ_PEER):
            js = slice(jc * RJ_PEER, (jc + 1) * RJ_PEER)
            rs = slice(ii * PK_NKEYS + jc * RJ_PEER, ii * PK_NKEYS + (jc + 1) * RJ_PEER)
            g = None
            for h in range(PK_HEADS):
                term = jnp.where(rank2_ref[h, js, :] < row_ref[0, h], e2_ref[h, js, :], zero) * row_ref[1, h]
                g = term if g is None else g + term
            a = a_ref[rs, :]
            w_ref[rs, :] = g * (a * (1.0 + lax.erf(a * sqrt_half))).astype(BF16)
    acc_ref[...] += _dot(vt_ref[...], w_ref[...])

    @pl.when(pl.program_id(1) == pl.num_programs(1) - 1)
    def _():
        o_ref[...] = acc_ref[...]


def _peer_dense(h2t, u_tabs, vt_tabs, layer, rank2, e2, cnt, q1):
    d, n = h2t.shape
    ne = u_tabs.shape[1]
    ri = TE_PEER // PK_NKEYS
    gspec = pl.BlockSpec((PK_HEADS, PK_NKEYS, TM_PEER), lambda i, e: (0, 0, i))
    rspec = pl.BlockSpec((PK_HEADS, ri, TM_PEER), lambda i, e: (0, e, i))
    return pl.pallas_call(
        _peer_dense_kernel,
        grid=(n // TM_PEER, ne // TE_PEER),
        in_specs=[pl.BlockSpec((d, TM_PEER), lambda i, e: (0, i)),
                  pl.BlockSpec((None, TE_PEER, d), lambda i, e: (layer, e, 0)),
                  pl.BlockSpec((None, d, TE_PEER), lambda i, e: (layer, 0, e)),
                  gspec, gspec, rspec, rspec],
        out_specs=pl.BlockSpec((d, TM_PEER), lambda i, e: (0, i)),
        out_shape=jax.ShapeDtypeStruct((d, n), F32),
        scratch_shapes=[pltpu.VMEM((d, TM_PEER), F32),
                        pltpu.VMEM((TE_PEER, TM_PEER), F32),
                        pltpu.VMEM((TE_PEER, TM_PEER), BF16),
                        pltpu.VMEM((2, PK_HEADS, RJ_PEER, TM_PEER), BF16)],
        compiler_params=_cp(("parallel", "arbitrary"), 56),
        name="peer_dense",
    )(h2t, u_tabs, vt_tabs, rank2, e2, cnt, q1)


def _peer_ln_kernel(yt_ref, x1_ref, mod_ref, lg_ref, lb_ref, o_ref):
    g2 = mod_ref[0, 5:6, :]
    o_ref[...] = _layernorm(DN_ALPHA * x1_ref[...] + g2 * yt_ref[...].T, lg_ref[...], lb_ref[...])


def _peer_ln(yt, x1, mod, ln_g, ln_b, tiles_per_mod):
    n, d = x1.shape
    return pl.pallas_call(
        _peer_ln_kernel,
        grid=(n // TM_OUT,),
        in_specs=[pl.BlockSpec((d, TM_OUT), lambda i: (0, i)),
                  pl.BlockSpec((TM_OUT, d), lambda i: (i, 0)),
                  pl.BlockSpec((1, 6, d), lambda i: (i // tiles_per_mod, 0, 0)),
                  pl.BlockSpec((1, d), lambda i: (0, 0)),
                  pl.BlockSpec((1, d), lambda i: (0, 0))],
        out_specs=pl.BlockSpec((TM_OUT, d), lambda i: (i, 0)),
        out_shape=jax.ShapeDtypeStruct((n, d), F32),
        compiler_params=_cp(("parallel",), 40),
        name="peer_ln",
    )(yt, x1, mod, ln_g.reshape(1, d), ln_b.reshape(1, d))


def _permute_in_proj(w):
    d = w.shape[0]

    def interleave(cols):
        c = cols.reshape(d, 2, DF_HEADS, 2, DF_QK // 2)
        return c.transpose(0, 2, 3, 1, 4).reshape(d, 2 * DF_HEADS * DF_QK)

    return jnp.concatenate([w[:, :OFF_DQ], interleave(w[:, OFF_DQ:OFF_DK]),
                            interleave(w[:, OFF_DK:OFF_DV]), w[:, OFF_DV:]], axis=1).astype(BF16)


def _pair_lanes(a, b):
    half = DF_QK // 2
    return jnp.concatenate([a[..., :half], b[..., :half], a[..., half:], b[..., half:]], -1)


def _unpair_lanes(kk):
    half = DF_QK // 2
    a = jnp.concatenate([kk[..., 0:half], kk[..., 2 * half:3 * half]], -1)
    b = jnp.concatenate([kk[..., half:2 * half], kk[..., 3 * half:]], -1)
    return a, b


def _rope_tables(t):
    tok = jnp.arange(t)
    row = (tok // GRID_W).astype(F32)
    col = (tok % GRID_W).astype(F32)
    n_freq = DF_QK // 4
    inv = 1.0 / (ROPE_THETA ** (jnp.arange(n_freq, dtype=F32) / n_freq))
    ang = jnp.concatenate([row[:, None] * inv, col[:, None] * inv], -1)
    cos, sin = jnp.cos(ang), jnp.sin(ang)
    return jnp.concatenate([cos] * 4, -1), jnp.concatenate([-sin, -sin, sin, sin], -1)


def _na_bias_table(rpb, rows):
    cases = _na_group_layout(rows)
    cols = np.arange(GRID_W)
    start = np.clip(cols - NA_WIN_C // 2, 0, GRID_W - NA_WIN_C)
    inwin = (cols[None, :] >= start[:, None]) & (cols[None, :] < start[:, None] + NA_WIN_C)
    nh = rpb.shape[0]
    dr = np.zeros((len(cases), NA_QROWS, NA_BAND), np.int32)
    valid = np.zeros((len(cases), NA_QROWS, NA_BAND), bool)
    for c, pat in enumerate(cases):
        for a, (off, cs) in enumerate(pat):
            for kr in range(off, off + NA_WIN_R):
                dr[c, a, kr] = kr - off - cs + NA_WIN_R - 1
                valid[c, a, kr] = True
    sel = rpb[:, dr.reshape(-1), :].reshape(nh, len(cases), NA_QROWS, NA_BAND, 2 * NA_WIN_C - 1)
    lead = GRID_W - NA_WIN_C
    p = jnp.pad(sel, ((0, 0),) * 4 + ((lead, 2 * GRID_W - lead - (2 * NA_WIN_C - 1)),))
    m = jnp.tile(p, (1, 1, 1, 1, GRID_W))[..., :GRID_W * (2 * GRID_W - 1)]
    m = m.reshape(nh, len(cases), NA_QROWS, NA_BAND, GRID_W, 2 * GRID_W - 1)
    tab = m[..., GRID_W - 1:].transpose(0, 1, 2, 4, 3, 5)
    keep = valid[None, :, :, None, :, None] & inwin[None, None, None, :, None, :]
    tab = jnp.where(keep, tab, NEG_BIG)
    return tab.reshape(nh, len(cases), NA_QROWS * GRID_W, NA_BAND * GRID_W)


def _heads_out(a, b, t, h):
    return a.reshape(b, t, h, -1).transpose(0, 2, 1, 3)


def _peer_block(oa, ob, oc, x, mod, tiles_out, w_out, ln_g, ln_b, wqt, keys, u_tabs, vt_tabs, layer):
    x1, h2t = _out_proj(oa, ob, oc, x, mod, w_out, ln_g[0], ln_b[0], tiles_out)
    rank2, e2, cnt, q1 = _peer_score(h2t, wqt, keys)
    yt = _peer_dense(h2t, u_tabs, vt_tabs, layer, rank2, e2, cnt, q1)
    return _peer_ln(yt, x1, mod, ln_g[1], ln_b[1], tiles_out)


def kernel(x_prompt, x_sample, cache_na_k, cache_na_v, cache_df_k1, cache_df_k2, cache_df_v, c, c_ctx,
           w_mod, b_mod, w_in, na_rpb, sg_ln_g, sg_ln_b, sg_w, sg_b, df_lambda, df_subln_g, w_out,
           pk_wq, pk_keys, pk_u, pk_v, ln_g, ln_b):
    bp, tp, d = x_prompt.shape
    bs, ts, _ = x_sample.shape
    n_p, n_s = bp * tp, bs * ts

    cond8 = jnp.zeros((8, d), F32).at[0].set(c_ctx).at[1:1 + bs].set(c)
    mods = _modulation(cond8, w_mod, b_mod).reshape(DEPTH, 8, 6, d)

    cos_t, sin_t = _rope_tables(ts)
    cache_dk = _pair_lanes(cache_df_k1, cache_df_k2)
    u_tabs = pk_u.astype(BF16)
    vt_tabs = pk_v.transpose(0, 2, 1).astype(BF16)

    xp = x_prompt.reshape(n_p, d)
    xs = x_sample.reshape(n_s, d)
    st = [[] for _ in range(5)]
    for l in range(DEPTH):
        lam_init = 0.8 - 0.6 * math.exp(-0.3 * l)
        w_in_l = _permute_in_proj(w_in[l])
        w_out_l = w_out[l].astype(BF16)
        wqt = pk_wq[l].T.astype(BF16)
        keys = pk_keys[l].reshape(2 * PK_HEADS, PK_NKEYS, PK_QDIM // 2).astype(BF16)
        ws = sg_w[l].astype(BF16)
        bs_col = sg_b[l].reshape(SG_GROUPS, SG_CHUNK, 1)
        bias = _na_bias_table(na_rpb[l], ts // GRID_W)
        mod_p = mods[l, 0:1]
        mod_s = mods[l, 1:1 + bs]

        proj = _in_proj(xp, mod_p, w_in_l, n_p // TM_IN)
        proj3 = proj.reshape(bp, tp, IN_WIDTH)
        oa, oc = _ctx_attn(proj3, df_lambda[l], df_subln_g[l], lam_init)
        ob = _spatial_gate(proj3, sg_ln_g[l], sg_ln_b[l], ws, bs_col)
        st[0].append(_heads_out(proj[:, OFF_KA:OFF_KA + NA_WIDTH], bp, tp, NA_HEADS))
        st[1].append(_heads_out(proj[:, OFF_VA:OFF_VA + NA_WIDTH], bp, tp, NA_HEADS))
        k1, k2 = _unpair_lanes(_heads_out(proj[:, OFF_DK:OFF_DV], bp, tp, DF_HEADS))
        st[2].append(k1)
        st[3].append(k2)
        st[4].append(_heads_out(proj[:, OFF_DV:], bp, tp, DF_HEADS))
        xp = _peer_block(oa.reshape(n_p, NA_WIDTH), ob.reshape(n_p, SG_WIDTH), oc.reshape(n_p, DF_WIDTH),
                         xp, mod_p, n_p // TM_OUT, w_out_l, ln_g[l], ln_b[l], wqt, keys, u_tabs, vt_tabs, l)

        proj = _in_proj(xs, mod_s, w_in_l, ts // TM_IN)
        proj3 = proj.reshape(bs, ts, IN_WIDTH)
        oa = _lat_na(proj3, cache_na_k, cache_na_v, bias, l)
        ob = _spatial_gate(proj3, sg_ln_g[l], sg_ln_b[l], ws, bs_col)
        oc = _lat_df(proj3, cache_dk, cache_df_v, cos_t, sin_t, df_lambda[l], df_subln_g[l], lam_init, l)
        xs = _peer_block(oa.reshape(n_s, NA_WIDTH), ob.reshape(n_s, SG_WIDTH), oc.reshape(n_s, DF_WIDTH),
                         xs, mod_s, ts // TM_OUT, w_out_l, ln_g[l], ln_b[l], wqt, keys, u_tabs, vt_tabs, l)

    outs = [jnp.stack(s, 1) for s in st]
    return (xp.reshape(bp, tp, d), xs.reshape(bs, ts, d), *outs)
```

```python
import functools
import math

import numpy as np
import jax
import jax.numpy as jnp
from jax import lax
from jax.experimental import pallas as pl
from jax.experimental.pallas import tpu as pltpu

F32 = jnp.float32
BF16 = jnp.bfloat16

D_MODEL = 2048
DEPTH = 4
GRID_W = 64
NA_HEADS = 8
NA_DIM = 128
NA_WIN_R = 8
NA_WIN_C = 16
SG_GROUPS = 4
SG_DIM = 128
SG_CHUNK = 128
DF_HEADS = 4
DF_QK = 64
DF_V = 128
ROPE_THETA = 10000.0
NA_WIDTH = NA_HEADS * NA_DIM
SG_WIDTH = SG_GROUPS * SG_DIM
DF_WIDTH = DF_HEADS * DF_V
MIX_WIDTH = NA_WIDTH + SG_WIDTH + DF_WIDTH
IN_WIDTH = 3 * NA_WIDTH + 2 * SG_WIDTH + 4 * DF_HEADS * DF_QK + DF_WIDTH
PK_HEADS = 8
PK_QDIM = 256
PK_NKEYS = 128
PK_TOPK = 16
PK_EXPERTS = PK_NKEYS * PK_NKEYS
DN_ALPHA = (2 * DEPTH) ** 0.25
LN_EPS = 1e-5
RMS_EPS = 1e-6

OFF_QA = 0
OFF_KA = NA_WIDTH
OFF_VA = 2 * NA_WIDTH
OFF_U = 3 * NA_WIDTH
OFF_V = OFF_U + SG_WIDTH
OFF_DQ = OFF_V + SG_WIDTH
OFF_DK = OFF_DQ + 2 * DF_HEADS * DF_QK
OFF_DV = OFF_DK + 2 * DF_HEADS * DF_QK

LANES = 128
MIB = 1024 * 1024

TM_IN = 512
TN_IN = IN_WIDTH // 4
TM_OUT = 512
TM_SCORE = 256
TM_PEER = 512
TE_PEER = 1024
RJ_PEER = 16
TQ_DF = 256
TN_MOD = 1024
NEG_BIG = -1e30


def _cp(sem, vmem_mib):
    return pltpu.CompilerParams(dimension_semantics=sem, vmem_limit_bytes=vmem_mib * MIB)


def _dot(a, b):
    return jnp.dot(a, b, preferred_element_type=F32)


def _dot_nt(a, b):
    return lax.dot_general(a, b, (((1,), (1,)), ((), ())), preferred_element_type=F32)


def _layernorm(z, g, b):
    mu = jnp.mean(z, -1, keepdims=True)
    d = z - mu
    var = jnp.mean(d * d, -1, keepdims=True)
    return d * lax.rsqrt(var + LN_EPS) * g + b


def _diff_probs(s1, s2, lam):
    e1 = jnp.exp(s1 - jnp.max(s1, -1, keepdims=True))
    e2 = jnp.exp(s2 - jnp.max(s2, -1, keepdims=True))
    c1 = 1.0 / jnp.sum(e1, -1, keepdims=True)
    c2 = lam / jnp.sum(e2, -1, keepdims=True)
    return e1 * c1 - e2 * c2


def _softmax_rows(s):
    m = jnp.max(s, -1, keepdims=True)
    e = jnp.exp(s - m)
    return e * (1.0 / jnp.sum(e, -1, keepdims=True))


def _diff_lambda(lam_ref, lam_init):
    lf = lam_ref[...]
    a = jnp.sum(lf[0:1, :] * lf[1:2, :], axis=1, keepdims=True)
    b = jnp.sum(lf[2:3, :] * lf[3:4, :], axis=1, keepdims=True)
    return jnp.exp(a) - jnp.exp(b) + lam_init


def _map1_mask():
    lane = lax.broadcasted_iota(jnp.int32, (1, LANES), 1)
    return (lane // (DF_QK // 2)) % 2 == 0


def _subln(o, g, lam_init):
    return o * lax.rsqrt(jnp.mean(o * o, -1, keepdims=True) + RMS_EPS) * g * (1.0 - lam_init)


def _mod_kernel(c_ref, w_ref, b_ref, o_ref):
    c = c_ref[...]
    a = (c * jax.nn.sigmoid(c)).astype(BF16)
    o_ref[0] = _dot(a, w_ref[0].astype(BF16)) + b_ref[0]


def _modulation(cond8, w_mod, b_mod):
    depth, d, n6 = w_mod.shape
    return pl.pallas_call(
        _mod_kernel,
        grid=(depth, n6 // TN_MOD),
        in_specs=[pl.BlockSpec((8, d), lambda l, j: (0, 0)),
                  pl.BlockSpec((1, d, TN_MOD), lambda l, j: (l, 0, j)),
                  pl.BlockSpec((1, 1, TN_MOD), lambda l, j: (l, 0, j))],
        out_specs=pl.BlockSpec((1, 8, TN_MOD), lambda l, j: (l, 0, j)),
        out_shape=jax.ShapeDtypeStruct((depth, 8, n6), F32),
        compiler_params=_cp(("parallel", "parallel"), 32),
        name="modulation",
    )(cond8, w_mod, b_mod.reshape(depth, 1, n6))


def _in_proj_kernel(x_ref, mod_ref, w_ref, o_ref, h_ref):
    @pl.when(pl.program_id(1) == 0)
    def _():
        sh = mod_ref[0, 0:1, :]
        sc = mod_ref[0, 1:2, :]
        h_ref[...] = (x_ref[...] * (1 + sc) + sh).astype(BF16)

    o_ref[...] = _dot(h_ref[...], w_ref[...])


def _in_proj(x, mod, w, tiles_per_mod):
    n, d = x.shape
    nw = w.shape[1]
    return pl.pallas_call(
        _in_proj_kernel,
        grid=(n // TM_IN, nw // TN_IN),
        in_specs=[pl.BlockSpec((TM_IN, d), lambda i, j: (i, 0)),
                  pl.BlockSpec((1, 6, d), lambda i, j: (i // tiles_per_mod, 0, 0)),
                  pl.BlockSpec((d, TN_IN), lambda i, j: (0, j))],
        out_specs=pl.BlockSpec((TM_IN, TN_IN), lambda i, j: (i, j)),
        out_shape=jax.ShapeDtypeStruct((n, nw), F32),
        scratch_shapes=[pltpu.VMEM((TM_IN, d), BF16)],
        compiler_params=_cp(("parallel", "arbitrary"), 40),
        name="in_proj",
    )(x, mod, w)


def _ctx_attn_kernel(p_ref, lam_ref, g_ref, oa_ref, oc_ref, *, lam_init):
    scale = NA_DIM ** -0.5
    for h in range(NA_HEADS):
        sl = slice(h * NA_DIM, (h + 1) * NA_DIM)
        q = p_ref[0, :, OFF_QA + h * NA_DIM:OFF_QA + (h + 1) * NA_DIM].astype(BF16)
        k = p_ref[0, :, OFF_KA + h * NA_DIM:OFF_KA + (h + 1) * NA_DIM].astype(BF16)
        v = p_ref[0, :, OFF_VA + h * NA_DIM:OFF_VA + (h + 1) * NA_DIM].astype(BF16)
        p = _softmax_rows(_dot_nt(q, k) * scale)
        oa_ref[0, :, sl] = _dot(p.astype(BF16), v).astype(BF16)
    lam = _diff_lambda(lam_ref, lam_init)
    m1 = _map1_mask()
    dscale = DF_QK ** -0.5
    for h in range(DF_HEADS):
        sl = slice(h * DF_V, (h + 1) * DF_V)
        q = p_ref[0, :, OFF_DQ + h * LANES:OFF_DQ + (h + 1) * LANES] * dscale
        k = p_ref[0, :, OFF_DK + h * LANES:OFF_DK + (h + 1) * LANES].astype(BF16)
        v = p_ref[0, :, OFF_DV + h * DF_V:OFF_DV + (h + 1) * DF_V].astype(BF16)
        qa = jnp.where(m1, q, 0.0).astype(BF16)
        qb = jnp.where(m1, 0.0, q).astype(BF16)
        p = _diff_probs(_dot_nt(qa, k), _dot_nt(qb, k), lam)
        o = _dot(p.astype(BF16), v)
        oc_ref[0, :, sl] = _subln(o, g_ref[...], lam_init).astype(BF16)


def _ctx_attn(proj3, lam_p, subln_g, lam_init):
    b, t, nw = proj3.shape
    return pl.pallas_call(
        functools.partial(_ctx_attn_kernel, lam_init=lam_init),
        grid=(b,),
        in_specs=[pl.BlockSpec((1, t, nw), lambda i: (i, 0, 0)),
                  pl.BlockSpec((4, DF_QK), lambda i: (0, 0)),
                  pl.BlockSpec((1, DF_V), lambda i: (0, 0))],
        out_specs=[pl.BlockSpec((1, t, NA_WIDTH), lambda i: (i, 0, 0)),
                   pl.BlockSpec((1, t, DF_WIDTH), lambda i: (i, 0, 0))],
        out_shape=[jax.ShapeDtypeStruct((b, t, NA_WIDTH), BF16),
                   jax.ShapeDtypeStruct((b, t, DF_WIDTH), BF16)],
        compiler_params=_cp(("parallel",), 32),
        name="ctx_attn",
    )(proj3, lam_p, subln_g.reshape(1, DF_V))


NA_QROWS = 4
NA_BAND = NA_WIN_R + NA_QROWS


def _na_group_layout(rows):
    groups = []
    for g in range(rows // NA_QROWS):
        bs = int(np.clip(g * NA_QROWS - NA_WIN_R // 2, 0, rows - NA_BAND))
        pat = []
        for a in range(NA_QROWS):
            r = g * NA_QROWS + a
            r0 = int(np.clip(r - NA_WIN_R // 2, 0, rows - NA_WIN_R))
            pat.append((r0 - bs, r - r0))
        groups.append((bs, tuple(pat)))
    cases = [groups[0][1], groups[1][1], groups[-1][1]]
    assert all(p == cases[1] for _, p in groups[1:-1]) and rows >= 2 * NA_BAND
    assert all(0 <= off and off + NA_WIN_R <= NA_BAND for c in cases for off, _ in c)
    return cases


def _lat_na_kernel(q_ref, k_ref, v_ref, kc_ref, vc_ref, bias_ref, o_ref, kb_ref, vb_ref, *, rows):
    kb_ref[...] = k_ref[0].astype(BF16)
    vb_ref[...] = v_ref[0].astype(BF16)
    kc = kc_ref[0, 0, 0].astype(BF16)
    vc = vc_ref[0, 0, 0].astype(BF16)
    scale = NA_DIM ** -0.5
    nq = NA_QROWS * GRID_W
    band = NA_BAND * GRID_W
    ngroups = rows // NA_QROWS

    def body(g, carry):
        bs = jnp.clip(g * NA_QROWS - NA_WIN_R // 2, 0, rows - NA_BAND)
        case = jnp.where(g == 0, 0, jnp.where(g == ngroups - 1, 2, 1))
        q0 = pl.multiple_of(g * nq, nq)
        q = q_ref[0, pl.ds(q0, nq), :].astype(BF16)
        k0 = pl.multiple_of(bs * GRID_W, GRID_W)
        kw = kb_ref[pl.ds(k0, band), :]
        vw = vb_ref[pl.ds(k0, band), :]
        s_loc = _dot_nt(q, kw) * scale + bias_ref[0, case]
        s_ctx = _dot_nt(q, kc) * scale
        m = jnp.maximum(jnp.max(s_loc, -1, keepdims=True), jnp.max(s_ctx, -1, keepdims=True))
        e_loc = jnp.exp(s_loc - m)
        e_ctx = jnp.exp(s_ctx - m)
        inv = 1.0 / (jnp.sum(e_loc, -1, keepdims=True) + jnp.sum(e_ctx, -1, keepdims=True))
        o = _dot((e_loc * inv).astype(BF16), vw) + _dot((e_ctx * inv).astype(BF16), vc)
        o_ref[0, pl.ds(q0, nq), :] = o.astype(BF16)
        return carry

    lax.fori_loop(0, ngroups, body, 0)


def _lat_na(proj3, cache_k, cache_v, bias, layer):
    b, t, _ = proj3.shape
    past = cache_k.shape[3]
    rows = t // GRID_W
    qb, kb, vb = OFF_QA // NA_DIM, OFF_KA // NA_DIM, OFF_VA // NA_DIM
    return pl.pallas_call(
        functools.partial(_lat_na_kernel, rows=rows),
        grid=(b, NA_HEADS),
        in_specs=[pl.BlockSpec((1, t, NA_DIM), lambda i, h: (i, 0, qb + h)),
                  pl.BlockSpec((1, t, NA_DIM), lambda i, h: (i, 0, kb + h)),
                  pl.BlockSpec((1, t, NA_DIM), lambda i, h: (i, 0, vb + h)),
                  pl.BlockSpec((1, 1, 1, past, NA_DIM), lambda i, h: (i, layer, h, 0, 0)),
                  pl.BlockSpec((1, 1, 1, past, NA_DIM), lambda i, h: (i, layer, h, 0, 0)),
                  pl.BlockSpec((1, 3, NA_QROWS * GRID_W, NA_BAND * GRID_W), lambda i, h: (h, 0, 0, 0))],
        out_specs=pl.BlockSpec((1, t, NA_DIM), lambda i, h: (i, 0, h)),
        out_shape=jax.ShapeDtypeStruct((b, t, NA_WIDTH), BF16),
        scratch_shapes=[pltpu.VMEM((t, NA_DIM), BF16), pltpu.VMEM((t, NA_DIM), BF16)],
        compiler_params=_cp(("parallel", "parallel"), 40),
        name="lat_na",
    )(proj3, proj3, proj3, cache_k, cache_v, bias)


def _lat_df_kernel(q_ref, k_ref, v_ref, kc_ref, vc_ref, cos_ref, sin_ref, lam_ref, g_ref, o_ref,
                   kall_ref, vall_ref, *, t, lam_init):
    qi = pl.program_id(2)

    @pl.when(qi == 0)
    def _():
        k = k_ref[0]
        kall_ref[0:t, :] = (k * cos_ref[...] + pltpu.roll(k, LANES // 2, 1) * sin_ref[...]).astype(BF16)
        kall_ref[t:, :] = kc_ref[0, 0, 0].astype(BF16)
        vall_ref[0:t, :] = v_ref[0].astype(BF16)
        vall_ref[t:, :] = vc_ref[0, 0, 0].astype(BF16)

    t0 = pl.multiple_of(qi * TQ_DF, TQ_DF)
    q = q_ref[0]
    q = q * cos_ref[pl.ds(t0, TQ_DF), :] + pltpu.roll(q, LANES // 2, 1) * sin_ref[pl.ds(t0, TQ_DF), :]
    q = q * DF_QK ** -0.5
    m1 = _map1_mask()
    qa = jnp.where(m1, q, 0.0).astype(BF16)
    qb = jnp.where(m1, 0.0, q).astype(BF16)
    lam = _diff_lambda(lam_ref, lam_init)
    kall = kall_ref[...]
    p = _diff_probs(_dot_nt(qa, kall), _dot_nt(qb, kall), lam)
    o = _dot(p.astype(BF16), vall_ref[...])
    o_ref[0] = _subln(o, g_ref[...], lam_init).astype(BF16)


def _lat_df(proj3, cache_kp, cache_v, cos_t, sin_t, lam_p, subln_g, lam_init, layer):
    b, t, _ = proj3.shape
    past = cache_v.shape[3]
    qb, kb, vb = OFF_DQ // LANES, OFF_DK // LANES, OFF_DV // LANES
    return pl.pallas_call(
        functools.partial(_lat_df_kernel, t=t, lam_init=lam_init),
        grid=(b, DF_HEADS, t // TQ_DF),
        in_specs=[pl.BlockSpec((1, TQ_DF, LANES), lambda i, h, j: (i, j, qb + h)),
                  pl.BlockSpec((1, t, LANES), lambda i, h, j: (i, 0, kb + h)),
                  pl.BlockSpec((1, t, LANES), lambda i, h, j: (i, 0, vb + h)),
                  pl.BlockSpec((1, 1, 1, past, LANES), lambda i, h, j: (i, layer, h, 0, 0)),
                  pl.BlockSpec((1, 1, 1, past, DF_V), lambda i, h, j: (i, layer, h, 0, 0)),
                  pl.BlockSpec((t, LANES), lambda i, h, j: (0, 0)),
                  pl.BlockSpec((t, LANES), lambda i, h, j: (0, 0)),
                  pl.BlockSpec((4, DF_QK), lambda i, h, j: (0, 0)),
                  pl.BlockSpec((1, DF_V), lambda i, h, j: (0, 0))],
        out_specs=pl.BlockSpec((1, TQ_DF, DF_V), lambda i, h, j: (i, j, h)),
        out_shape=jax.ShapeDtypeStruct((b, t, DF_WIDTH), BF16),
        scratch_shapes=[pltpu.VMEM((t + past, LANES), BF16), pltpu.VMEM((t + past, DF_V), BF16)],
        compiler_params=_cp(("parallel", "parallel", "arbitrary"), 48),
        name="lat_df",
    )(proj3, proj3, proj3, cache_kp, cache_v, cos_t, sin_t, lam_p, subln_g.reshape(1, DF_V))


def _sg_kernel(u_ref, v_ref, g_ref, b_ref, ws_ref, bs_ref, o_ref, *, nch):
    for c in range(nch):
        rs = slice(c * SG_CHUNK, (c + 1) * SG_CHUNK)
        vn = _layernorm(v_ref[0, rs, :], g_ref[...], b_ref[...]).astype(BF16)
        for g in range(SG_GROUPS):
            cs = slice(g * SG_DIM, (g + 1) * SG_DIM)
            mixed = _dot(ws_ref[g], vn[:, cs]) + bs_ref[g]
            o_ref[0, rs, cs] = (u_ref[0, rs, cs] * mixed).astype(BF16)


def _spatial_gate(proj3, ln_g, ln_b, ws, bs):
    b, t, _ = proj3.shape
    nch = min(4, t // SG_CHUNK)
    tt = nch * SG_CHUNK
    ub, vb = OFF_U // SG_WIDTH, OFF_V // SG_WIDTH
    return pl.pallas_call(
        functools.partial(_sg_kernel, nch=nch),
        grid=(b, t // tt),
        in_specs=[pl.BlockSpec((1, tt, SG_WIDTH), lambda i, j: (i, j, ub)),
                  pl.BlockSpec((1, tt, SG_WIDTH), lambda i, j: (i, j, vb)),
                  pl.BlockSpec((1, SG_WIDTH), lambda i, j: (0, 0)),
                  pl.BlockSpec((1, SG_WIDTH), lambda i, j: (0, 0)),
                  pl.BlockSpec((SG_GROUPS, SG_CHUNK, SG_CHUNK), lambda i, j: (0, 0, 0)),
                  pl.BlockSpec((SG_GROUPS, SG_CHUNK, 1), lambda i, j: (0, 0, 0))],
        out_specs=pl.BlockSpec((1, tt, SG_WIDTH), lambda i, j: (i, j, 0)),
        out_shape=jax.ShapeDtypeStruct((b, t, SG_WIDTH), BF16),
        compiler_params=_cp(("parallel", "parallel"), 16),
        name="spatial_gate",
    )(proj3, proj3, ln_g.reshape(1, SG_WIDTH), ln_b.reshape(1, SG_WIDTH), ws, bs)


def _cache_out_kernel(ka_ref, va_ref, dk_ref, dv_ref, *refs):
    ok, ov, ok1, ok2, odv = refs[5:]
    half = DF_QK // 2
    for h in range(NA_HEADS):
        ok[0, 0, h] = ka_ref[0, :, h * NA_DIM:(h + 1) * NA_DIM]
        ov[0, 0, h] = va_ref[0, :, h * NA_DIM:(h + 1) * NA_DIM]
    for h in range(DF_HEADS):
        kk = dk_ref[0, :, h * LANES:(h + 1) * LANES]
        ok1[0, 0, h] = jnp.concatenate([kk[:, 0:half], kk[:, 2 * half:3 * half]], -1)
        ok2[0, 0, h] = jnp.concatenate([kk[:, half:2 * half], kk[:, 3 * half:]], -1)
        odv[0, 0, h] = dv_ref[0, :, h * DF_V:(h + 1) * DF_V]


def _cache_out(proj3, caches, layer):
    b, t, _ = proj3.shape
    any_spec = pl.BlockSpec(memory_space=pl.ANY)

    def head_spec(c):
        return pl.BlockSpec((1, 1) + c.shape[2:], lambda i: (i, layer, 0, 0, 0))

    return pl.pallas_call(
        _cache_out_kernel,
        grid=(b,),
        in_specs=[pl.BlockSpec((1, t, NA_WIDTH), lambda i: (i, 0, OFF_KA // NA_WIDTH)),
                  pl.BlockSpec((1, t, NA_WIDTH), lambda i: (i, 0, OFF_VA // NA_WIDTH)),
                  pl.BlockSpec((1, t, DF_WIDTH), lambda i: (i, 0, OFF_DK // DF_WIDTH)),
                  pl.BlockSpec((1, t, DF_WIDTH), lambda i: (i, 0, OFF_DV // DF_WIDTH))] + [any_spec] * 5,
        out_specs=[head_spec(c) for c in caches],
        out_shape=[jax.ShapeDtypeStruct(c.shape, c.dtype) for c in caches],
        input_output_aliases={4 + k: k for k in range(5)},
        compiler_params=_cp(("parallel",), 32),
        name="cache_out",
    )(proj3, proj3, proj3, proj3, *caches)


def _out_proj_kernel(oa_ref, ob_ref, oc_ref, x_ref, mod_ref, w_ref, lg_ref, lb_ref, x1_ref, h2t_ref):
    y = (_dot(oa_ref[...], w_ref[0:NA_WIDTH, :])
         + _dot(ob_ref[...], w_ref[NA_WIDTH:NA_WIDTH + SG_WIDTH, :])
         + _dot(oc_ref[...], w_ref[NA_WIDTH + SG_WIDTH:, :]))
    g1 = mod_ref[0, 2:3, :]
    sh2 = mod_ref[0, 3:4, :]
    sc2 = mod_ref[0, 4:5, :]
    x1 = _layernorm(DN_ALPHA * x_ref[...] + g1 * y, lg_ref[...], lb_ref[...])
    x1_ref[...] = x1
    h2t_ref[...] = (x1 * (1 + sc2) + sh2).T.astype(BF16)


def _out_proj(oa, ob, oc, x, mod, w, ln_g, ln_b, tiles_per_mod):
    n, d = x.shape
    return pl.pallas_call(
        _out_proj_kernel,
        grid=(n // TM_OUT,),
        in_specs=[pl.BlockSpec((TM_OUT, NA_WIDTH), lambda i: (i, 0)),
                  pl.BlockSpec((TM_OUT, SG_WIDTH), lambda i: (i, 0)),
                  pl.BlockSpec((TM_OUT, DF_WIDTH), lambda i: (i, 0)),
                  pl.BlockSpec((TM_OUT, d), lambda i: (i, 0)),
                  pl.BlockSpec((1, 6, d), lambda i: (i // tiles_per_mod, 0, 0)),
                  pl.BlockSpec((MIX_WIDTH, d), lambda i: (0, 0)),
                  pl.BlockSpec((1, d), lambda i: (0, 0)),
                  pl.BlockSpec((1, d), lambda i: (0, 0))],
        out_specs=[pl.BlockSpec((TM_OUT, d), lambda i: (i, 0)),
                   pl.BlockSpec((d, TM_OUT), lambda i: (0, i))],
        out_shape=[jax.ShapeDtypeStruct((n, d), F32),
                   jax.ShapeDtypeStruct((d, n), BF16)],
        compiler_params=_cp(("parallel",), 48),
        name="out_proj",
    )(oa, ob, oc, x, mod, w, ln_g.reshape(1, d), ln_b.reshape(1, d))


_PK_PAIRS = [(a, b) for a in range(PK_TOPK) for b in range(PK_TOPK) if (a + 1) * (b + 1) <= PK_TOPK]


def _merge_exchange_network(n):
    t = int(math.ceil(math.log2(n)))
    p = 2 ** (t - 1)
    pairs = []
    while p > 0:
        q, r, d = 2 ** (t - 1), 0, p
        while d > 0:
            pairs.extend((i, i + d) for i in range(n - d) if i & p == r)
            d, q, r = q - p, q // 2, p
        p //= 2
    return pairs


def _bitonic_merge_network(n):
    pairs, d = [], n // 2
    while d >= 1:
        pairs.extend((i, i + d) for i in range(n) if i & d == 0)
        d //= 2
    return pairs


_SORT16 = _merge_exchange_network(PK_TOPK)
_MERGE16 = _bitonic_merge_network(PK_TOPK)


def _compare_exchange(xs, network):
    xs = list(xs)
    for i, j in network:
        xs[i], xs[j] = jnp.maximum(xs[i], xs[j]), jnp.minimum(xs[i], xs[j])
    return xs


def _peer_score_kernel(h2t_ref, wqt_ref, keys_ref, rank2_out, e2_out, cnt_out, q1_out,
                       qt_ref, s_ref, tt_ref, thr_ref, z_ref):
    tm = h2t_ref.shape[1]
    qt_ref[...] = _dot(wqt_ref[...], h2t_ref[...]).astype(BF16)
    neg_inf = -jnp.inf
    sub = 8

    for c in range(2 * PK_HEADS):
        s = _dot(keys_ref[c], qt_ref[c * PK_NKEYS:(c + 1) * PK_NKEYS, :])
        s_ref[c] = s
        head, half = c // 2, c % 2
        xs = _compare_exchange([s[r * sub:(r + 1) * sub, :] for r in range(PK_TOPK)], _SORT16)
        shift = 1
        while shift < sub:
            ys = [pltpu.roll(xs[PK_TOPK - 1 - r], shift, 0) for r in range(PK_TOPK)]
            xs = _compare_exchange([jnp.maximum(x, y) for x, y in zip(xs, ys)], _MERGE16)
            shift *= 2
        for r in range(PK_TOPK):
            tt_ref[half, r, head:head + 1, :] = xs[r][0:1, :]

    def per_lane_chunk(ch, carry):
        l0 = ch * LANES
        t1 = [tt_ref[0, a, :, pl.ds(l0, LANES)] for a in range(PK_TOPK)]
        t2 = [tt_ref[1, b, :, pl.ds(l0, LANES)] for b in range(PK_TOPK)]
        cands = [t1[a] + t2[b] for a, b in _PK_PAIRS]
        cur = list(cands)
        for _ in range(PK_TOPK - 1):
            m = functools.reduce(jnp.maximum, cur)
            found = jnp.zeros(m.shape, jnp.bool_)
            nxt = []
            for cnd in cur:
                is_m = cnd == m
                nxt.append(jnp.where(jnp.logical_and(is_m, jnp.logical_not(found)), neg_inf, cnd))
                found = jnp.logical_or(found, is_m)
            cur = nxt
        thr = functools.reduce(jnp.maximum, cur)
        e1 = [jnp.exp(t1[a] - t1[0]) for a in range(PK_TOPK)]
        e2 = [jnp.exp(t2[b] - t2[0]) for b in range(PK_TOPK)]
        z = jnp.zeros(thr.shape, F32)
        for (a, b), cnd in zip(_PK_PAIRS, cands):
            z = z + jnp.where(cnd >= thr, e1[a] * e2[b], 0.0)
        thr_ref[:, pl.ds(l0, LANES)] = thr
        z_ref[:, pl.ds(l0, LANES)] = z
        return carry

    for ch in range(tm // LANES):
        per_lane_chunk(ch, 0)

    for h in range(PK_HEADS):
        s1 = s_ref[2 * h]
        s2 = s_ref[2 * h + 1]
        thr = thr_ref[h:h + 1, :]
        cnt = jnp.zeros(s1.shape, F32)
        rank2 = jnp.zeros(s2.shape, F32)
        for b in range(PK_TOPK):
            t2b = tt_ref[1, b, h:h + 1, :]
            cnt = jnp.where(s1 + t2b >= thr, float(b + 1), cnt)
            rank2 = jnp.where(t2b > s2, float(b + 1), rank2)
        cnt_out[h] = cnt
        rank2_out[h] = rank2.astype(BF16)
        q1_out[h] = jnp.exp(s1 - tt_ref[0, 0, h:h + 1, :]) * (0.5 / z_ref[h:h + 1, :])
        e2_out[h] = jnp.exp(s2 - tt_ref[1, 0, h:h + 1, :]).astype(BF16)


def _peer_score(h2t, wqt, keys):
    d, n = h2t.shape
    spec = pl.BlockSpec((PK_HEADS, PK_NKEYS, TM_SCORE), lambda i: (0, 0, i))
    shape = jax.ShapeDtypeStruct((PK_HEADS, PK_NKEYS, n), F32)
    shape16 = jax.ShapeDtypeStruct((PK_HEADS, PK_NKEYS, n), BF16)
    return pl.pallas_call(
        _peer_score_kernel,
        grid=(n // TM_SCORE,),
        in_specs=[pl.BlockSpec((d, TM_SCORE), lambda i: (0, i)),
                  pl.BlockSpec(wqt.shape, lambda i: (0, 0)),
                  pl.BlockSpec(keys.shape, lambda i: (0, 0, 0))],
        out_specs=[spec, spec, spec, spec],
        out_shape=[shape16, shape16, shape, shape],
        scratch_shapes=[pltpu.VMEM((PK_HEADS * PK_QDIM, TM_SCORE), BF16),
                        pltpu.VMEM((2 * PK_HEADS, PK_NKEYS, TM_SCORE), F32),
                        pltpu.VMEM((2, PK_TOPK, PK_HEADS, TM_SCORE), F32),
                        pltpu.VMEM((PK_HEADS, TM_SCORE), F32),
                        pltpu.VMEM((PK_HEADS, TM_SCORE), F32)],
        compiler_params=_cp(("parallel",), 48),
        name="peer_score",
    )(h2t, wqt, keys)


def _peer_dense_kernel(h2t_ref, u_ref, vt_ref, rank2_ref, e2_ref, cnt_ref, q1_ref, o_ref,
                       acc_ref, a_ref, w_ref, row_ref):
    sqrt_half = math.sqrt(0.5)

    @pl.when(pl.program_id(1) == 0)
    def _():
        acc_ref[...] = jnp.zeros_like(acc_ref)

    tm = h2t_ref.shape[1]
    zero = jnp.zeros((RJ_PEER, tm), BF16)
    a_ref[...] = _dot(u_ref[...], h2t_ref[...])
    for ii in range(TE_PEER // PK_NKEYS):
        for h in range(PK_HEADS):
            row_ref[0, h] = jnp.broadcast_to(cnt_ref[h, ii:ii + 1, :], (RJ_PEER, tm)).astype(BF16)
            row_ref[1, h] = jnp.broadcast_to(q1_ref[h, ii:ii + 1, :], (RJ_PEER, tm)).astype(BF16)
        for jc in range(PK_NKEYS // RJ_PEER):
            js = slice(jc * RJ_PEER, (jc + 1) * RJ_PEER)
            rs = slice(ii * PK_NKEYS + jc * RJ_PEER, ii * PK_NKEYS + (jc + 1) * RJ_PEER)
            g = None
            for h in range(PK_HEADS):
                term = jnp.where(rank2_ref[h, js, :] < row_ref[0, h], e2_ref[h, js, :], zero) * row_ref[1, h]
                g = term if g is None else g + term
            a = a_ref[rs, :]
            w_ref[rs, :] = g * (a * (1.0 + lax.erf(a * sqrt_half))).astype(BF16)
    acc_ref[...] += _dot(vt_ref[...], w_ref[...])

    @pl.when(pl.program_id(1) == pl.num_programs(1) - 1)
    def _():
        o_ref[...] = acc_ref[...]


def _peer_dense(h2t, u_tabs, vt_tabs, layer, rank2, e2, cnt, q1):
    d, n = h2t.shape
    ne = u_tabs.shape[1]
    ri = TE_PEER // PK_NKEYS
    gspec = pl.BlockSpec((PK_HEADS, PK_NKEYS, TM_PEER), lambda i, e: (0, 0, i))
    rspec = pl.BlockSpec((PK_HEADS, ri, TM_PEER), lambda i, e: (0, e, i))
    return pl.pallas_call(
        _peer_dense_kernel,
        grid=(n // TM_PEER, ne // TE_PEER),
        in_specs=[pl.BlockSpec((d, TM_PEER), lambda i, e: (0, i)),
                  pl.BlockSpec((None, TE_PEER, d), lambda i, e: (layer, e, 0)),
                  pl.BlockSpec((None, d, TE_PEER), lambda i, e: (layer, 0, e)),
                  gspec, gspec, rspec, rspec],
        out_specs=pl.BlockSpec((d, TM_PEER), lambda i, e: (0, i)),
        out_shape=jax.ShapeDtypeStruct((d, n), F32),
        scratch_shapes=[pltpu.VMEM((d, TM_PEER), F32),
                        pltpu.VMEM((TE_PEER, TM_PEER), F32),
                        pltpu.VMEM((TE_PEER, TM_PEER), BF16),
                        pltpu.VMEM((2, PK_HEADS, RJ_PEER, TM_PEER), BF16)],
        compiler_params=_cp(("parallel", "arbitrary"), 56),
        name="peer_dense",
    )(h2t, u_tabs, vt_tabs, rank2, e2, cnt, q1)


def _peer_ln_kernel(yt_ref, x1_ref, mod_ref, lg_ref, lb_ref, o_ref):
    g2 = mod_ref[0, 5:6, :]
    o_ref[...] = _layernorm(DN_ALPHA * x1_ref[...] + g2 * yt_ref[...].T, lg_ref[...], lb_ref[...])


def _peer_ln(yt, x1, mod, ln_g, ln_b, tiles_per_mod):
    n, d = x1.shape
    return pl.pallas_call(
        _peer_ln_kernel,
        grid=(n // TM_OUT,),
        in_specs=[pl.BlockSpec((d, TM_OUT), lambda i: (0, i)),
                  pl.BlockSpec((TM_OUT, d), lambda i: (i, 0)),
                  pl.BlockSpec((1, 6, d), lambda i: (i // tiles_per_mod, 0, 0)),
                  pl.BlockSpec((1, d), lambda i: (0, 0)),
                  pl.BlockSpec((1, d), lambda i: (0, 0))],
        out_specs=pl.BlockSpec((TM_OUT, d), lambda i: (i, 0)),
        out_shape=jax.ShapeDtypeStruct((n, d), F32),
        compiler_params=_cp(("parallel",), 40),
        name="peer_ln",
    )(yt, x1, mod, ln_g.reshape(1, d), ln_b.reshape(1, d))


def _permute_in_proj(w):
    d = w.shape[0]

    def interleave(cols):
        c = cols.reshape(d, 2, DF_HEADS, 2, DF_QK // 2)
        return c.transpose(0, 2, 3, 1, 4).reshape(d, 2 * DF_HEADS * DF_QK)

    return jnp.concatenate([w[:, :OFF_DQ], interleave(w[:, OFF_DQ:OFF_DK]),
                            interleave(w[:, OFF_DK:OFF_DV]), w[:, OFF_DV:]], axis=1).astype(BF16)


def _pair_lanes(a, b):
    half = DF_QK // 2
    return jnp.concatenate([a[..., :half], b[..., :half], a[..., half:], b[..., half:]], -1)


def _rope_tables(t):
    tok = jnp.arange(t)
    row = (tok // GRID_W).astype(F32)
    col = (tok % GRID_W).astype(F32)
    n_freq = DF_QK // 4
    inv = 1.0 / (ROPE_THETA ** (jnp.arange(n_freq, dtype=F32) / n_freq))
    ang = jnp.concatenate([row[:, None] * inv, col[:, None] * inv], -1)
    cos, sin = jnp.cos(ang), jnp.sin(ang)
    return jnp.concatenate([cos] * 4, -1), jnp.concatenate([-sin, -sin, sin, sin], -1)


def _na_bias_table(rpb, rows):
    cases = _na_group_layout(rows)
    cols = np.arange(GRID_W)
    start = np.clip(cols - NA_WIN_C // 2, 0, GRID_W - NA_WIN_C)
    inwin = (cols[None, :] >= start[:, None]) & (cols[None, :] < start[:, None] + NA_WIN_C)
    nh = rpb.shape[0]
    dr = np.zeros((len(cases), NA_QROWS, NA_BAND), np.int32)
    valid = np.zeros((len(cases), NA_QROWS, NA_BAND), bool)
    for c, pat in enumerate(cases):
        for a, (off, cs) in enumerate(pat):
            for kr in range(off, off + NA_WIN_R):
                dr[c, a, kr] = kr - off - cs + NA_WIN_R - 1
                valid[c, a, kr] = True
    sel = rpb[:, dr.reshape(-1), :].reshape(nh, len(cases), NA_QROWS, NA_BAND, 2 * NA_WIN_C - 1)
    lead = GRID_W - NA_WIN_C
    p = jnp.pad(sel, ((0, 0),) * 4 + ((lead, 2 * GRID_W - lead - (2 * NA_WIN_C - 1)),))
    m = jnp.tile(p, (1, 1, 1, 1, GRID_W))[..., :GRID_W * (2 * GRID_W - 1)]
    m = m.reshape(nh, len(cases), NA_QROWS, NA_BAND, GRID_W, 2 * GRID_W - 1)
    tab = m[..., GRID_W - 1:].transpose(0, 1, 2, 4, 3, 5)
    keep = valid[None, :, :, None, :, None] & inwin[None, None, None, :, None, :]
    tab = jnp.where(keep, tab, NEG_BIG)
    return tab.reshape(nh, len(cases), NA_QROWS * GRID_W, NA_BAND * GRID_W)


def _peer_block(oa, ob, oc, x, mod, tiles_out, w_out, ln_g, ln_b, wqt, keys, u_tabs, vt_tabs, layer):
    x1, h2t = _out_proj(oa, ob, oc, x, mod, w_out, ln_g[0], ln_b[0], tiles_out)
    rank2, e2, cnt, q1 = _peer_score(h2t, wqt, keys)
    yt = _peer_dense(h2t, u_tabs, vt_tabs, layer, rank2, e2, cnt, q1)
    return _peer_ln(yt, x1, mod, ln_g[1], ln_b[1], tiles_out)


def kernel(x_prompt, x_sample, cache_na_k, cache_na_v, cache_df_k1, cache_df_k2, cache_df_v, c, c_ctx,
           w_mod, b_mod, w_in, na_rpb, sg_ln_g, sg_ln_b, sg_w, sg_b, df_lambda, df_subln_g, w_out,
           pk_wq, pk_keys, pk_u, pk_v, ln_g, ln_b):
    bp, tp, d = x_prompt.shape
    bs, ts, _ = x_sample.shape
    n_p, n_s = bp * tp, bs * ts

    cond8 = jnp.zeros((8, d), F32).at[0].set(c_ctx).at[1:1 + bs].set(c)
    mods = _modulation(cond8, w_mod, b_mod).reshape(DEPTH, 8, 6, d)

    cos_t, sin_t = _rope_tables(ts)
    cache_dk = _pair_lanes(cache_df_k1, cache_df_k2)
    u_tabs = pk_u.astype(BF16)
    vt_tabs = pk_v.transpose(0, 2, 1).astype(BF16)

    xp = x_prompt.reshape(n_p, d)
    xs = x_sample.reshape(n_s, d)
    caches_out = [jnp.zeros((bp, DEPTH, NA_HEADS, tp, NA_DIM), F32), jnp.zeros((bp, DEPTH, NA_HEADS, tp, NA_DIM), F32),
                  jnp.zeros((bp, DEPTH, DF_HEADS, tp, DF_QK), F32), jnp.zeros((bp, DEPTH, DF_HEADS, tp, DF_QK), F32),
                  jnp.zeros((bp, DEPTH, DF_HEADS, tp, DF_V), F32)]
    for l in range(DEPTH):
        lam_init = 0.8 - 0.6 * math.exp(-0.3 * l)
        w_in_l = _permute_in_proj(w_in[l])
        w_out_l = w_out[l].astype(BF16)
        wqt = pk_wq[l].T.astype(BF16)
        keys = pk_keys[l].reshape(2 * PK_HEADS, PK_NKEYS, PK_QDIM // 2).astype(BF16)
        ws = sg_w[l].astype(BF16)
        bs_col = sg_b[l].reshape(SG_GROUPS, SG_CHUNK, 1)
        bias = _na_bias_table(na_rpb[l], ts // GRID_W)
        mod_p = mods[l, 0:1]
        mod_s = mods[l, 1:1 + bs]

        proj = _in_proj(xp, mod_p, w_in_l, n_p // TM_IN)
        proj3 = proj.reshape(bp, tp, IN_WIDTH)
        oa, oc = _ctx_attn(proj3, df_lambda[l], df_subln_g[l], lam_init)
        ob = _spatial_gate(proj3, sg_ln_g[l], sg_ln_b[l], ws, bs_col)
        caches_out = _cache_out(proj3, caches_out, l)
        xp = _peer_block(oa.reshape(n_p, NA_WIDTH), ob.reshape(n_p, SG_WIDTH), oc.reshape(n_p, DF_WIDTH),
                         xp, mod_p, n_p // TM_OUT, w_out_l, ln_g[l], ln_b[l], wqt, keys, u_tabs, vt_tabs, l)

        proj = _in_proj(xs, mod_s, w_in_l, ts // TM_IN)
        proj3 = proj.reshape(bs, ts, IN_WIDTH)
        oa = _lat_na(proj3, cache_na_k, cache_na_v, bias, l)
        ob = _spatial_gate(proj3, sg_ln_g[l], sg_ln_b[l], ws, bs_col)
        oc = _lat_df(proj3, cache_dk, cache_df_v, cos_t, sin_t, df_lambda[l], df_subln_g[l], lam_init, l)
        xs = _peer_block(oa.reshape(n_s, NA_WIDTH), ob.reshape(n_s, SG_WIDTH), oc.reshape(n_s, DF_WIDTH),
                         xs, mod_s, ts // TM_OUT, w_out_l, ln_g[l], ln_b[l], wqt, keys, u_tabs, vt_tabs, l)

    return (xp.reshape(bp, tp, d), xs.reshape(bs, ts, d), *caches_out)
```

```python
import functools
import math

import numpy as np
import jax
import jax.numpy as jnp
from jax import lax
from jax.experimental import pallas as pl
from jax.experimental.pallas import tpu as pltpu

F32 = jnp.float32
BF16 = jnp.bfloat16

D_MODEL = 2048
DEPTH = 4
GRID_W = 64
NA_HEADS = 8
NA_DIM = 128
NA_WIN_R = 8
NA_WIN_C = 16
SG_GROUPS = 4
SG_DIM = 128
SG_CHUNK = 128
DF_HEADS = 4
DF_QK = 64
DF_V = 128
ROPE_THETA = 10000.0
NA_WIDTH = NA_HEADS * NA_DIM
SG_WIDTH = SG_GROUPS * SG_DIM
DF_WIDTH = DF_HEADS * DF_V
MIX_WIDTH = NA_WIDTH + SG_WIDTH + DF_WIDTH
IN_WIDTH = 3 * NA_WIDTH + 2 * SG_WIDTH + 4 * DF_HEADS * DF_QK + DF_WIDTH
PK_HEADS = 8
PK_QDIM = 256
PK_NKEYS = 128
PK_TOPK = 16
PK_EXPERTS = PK_NKEYS * PK_NKEYS
DN_ALPHA = (2 * DEPTH) ** 0.25
LN_EPS = 1e-5
RMS_EPS = 1e-6

OFF_QA = 0
OFF_KA = NA_WIDTH
OFF_VA = 2 * NA_WIDTH
OFF_U = 3 * NA_WIDTH
OFF_V = OFF_U + SG_WIDTH
OFF_DQ = OFF_V + SG_WIDTH
OFF_DK = OFF_DQ + 2 * DF_HEADS * DF_QK
OFF_DV = OFF_DK + 2 * DF_HEADS * DF_QK

LANES = 128
MIB = 1024 * 1024

TM_IN = 512
TN_IN = IN_WIDTH // 4
TM_OUT = 512
TM_SCORE = 256
TM_PEER = 512
TE_PEER = 1024
RJ_PEER = 16
TQ_DF = 256
TN_MOD = 1024
NEG_BIG = -1e30


def _cp(sem, vmem_mib):
    return pltpu.CompilerParams(dimension_semantics=sem, vmem_limit_bytes=vmem_mib * MIB)


def _dot(a, b):
    return jnp.dot(a, b, preferred_element_type=F32)


def _dot_nt(a, b):
    return lax.dot_general(a, b, (((1,), (1,)), ((), ())), preferred_element_type=F32)


def _layernorm(z, g, b):
    mu = jnp.mean(z, -1, keepdims=True)
    d = z - mu
    var = jnp.mean(d * d, -1, keepdims=True)
    return d * lax.rsqrt(var + LN_EPS) * g + b


def _diff_probs(s1, s2, lam):
    e1 = jnp.exp(s1 - jnp.max(s1, -1, keepdims=True))
    e2 = jnp.exp(s2 - jnp.max(s2, -1, keepdims=True))
    c1 = 1.0 / jnp.sum(e1, -1, keepdims=True)
    c2 = lam / jnp.sum(e2, -1, keepdims=True)
    return e1 * c1 - e2 * c2


def _softmax_rows(s):
    m = jnp.max(s, -1, keepdims=True)
    e = jnp.exp(s - m)
    return e * (1.0 / jnp.sum(e, -1, keepdims=True))


def _diff_lambda(lam_ref, lam_init):
    lf = lam_ref[...]
    a = jnp.sum(lf[0:1, :] * lf[1:2, :], axis=1, keepdims=True)
    b = jnp.sum(lf[2:3, :] * lf[3:4, :], axis=1, keepdims=True)
    return jnp.exp(a) - jnp.exp(b) + lam_init


def _map1_mask():
    lane = lax.broadcasted_iota(jnp.int32, (1, LANES), 1)
    return (lane // (DF_QK // 2)) % 2 == 0


def _subln(o, g, lam_init):
    return o * lax.rsqrt(jnp.mean(o * o, -1, keepdims=True) + RMS_EPS) * g * (1.0 - lam_init)


def _mod_kernel(c_ref, w_ref, b_ref, o_ref):
    c = c_ref[...]
    a = (c * jax.nn.sigmoid(c)).astype(BF16)
    o_ref[0] = _dot(a, w_ref[0].astype(BF16)) + b_ref[0]


def _modulation(cond8, w_mod, b_mod):
    depth, d, n6 = w_mod.shape
    return pl.pallas_call(
        _mod_kernel,
        grid=(depth, n6 // TN_MOD),
        in_specs=[pl.BlockSpec((8, d), lambda l, j: (0, 0)),
                  pl.BlockSpec((1, d, TN_MOD), lambda l, j: (l, 0, j)),
                  pl.BlockSpec((1, 1, TN_MOD), lambda l, j: (l, 0, j))],
        out_specs=pl.BlockSpec((1, 8, TN_MOD), lambda l, j: (l, 0, j)),
        out_shape=jax.ShapeDtypeStruct((depth, 8, n6), F32),
        compiler_params=_cp(("parallel", "parallel"), 32),
        name="modulation",
    )(cond8, w_mod, b_mod.reshape(depth, 1, n6))


def _in_proj_kernel(x_ref, mod_ref, w_ref, o_ref, h_ref):
    @pl.when(pl.program_id(1) == 0)
    def _():
        sh = mod_ref[0, 0:1, :]
        sc = mod_ref[0, 1:2, :]
        h_ref[...] = (x_ref[...] * (1 + sc) + sh).astype(BF16)

    o_ref[...] = _dot(h_ref[...], w_ref[...])


def _in_proj(x, mod, w, tiles_per_mod):
    n, d = x.shape
    nw = w.shape[1]
    return pl.pallas_call(
        _in_proj_kernel,
        grid=(n // TM_IN, nw // TN_IN),
        in_specs=[pl.BlockSpec((TM_IN, d), lambda i, j: (i, 0)),
                  pl.BlockSpec((1, 6, d), lambda i, j: (i // tiles_per_mod, 0, 0)),
                  pl.BlockSpec((d, TN_IN), lambda i, j: (0, j))],
        out_specs=pl.BlockSpec((TM_IN, TN_IN), lambda i, j: (i, j)),
        out_shape=jax.ShapeDtypeStruct((n, nw), F32),
        scratch_shapes=[pltpu.VMEM((TM_IN, d), BF16)],
        compiler_params=_cp(("parallel", "arbitrary"), 40),
        name="in_proj",
    )(x, mod, w)


def _ctx_attn_kernel(p_ref, lam_ref, g_ref, oa_ref, oc_ref, *, lam_init):
    scale = NA_DIM ** -0.5
    for h in range(NA_HEADS):
        sl = slice(h * NA_DIM, (h + 1) * NA_DIM)
        q = p_ref[0, :, OFF_QA + h * NA_DIM:OFF_QA + (h + 1) * NA_DIM].astype(BF16)
        k = p_ref[0, :, OFF_KA + h * NA_DIM:OFF_KA + (h + 1) * NA_DIM].astype(BF16)
        v = p_ref[0, :, OFF_VA + h * NA_DIM:OFF_VA + (h + 1) * NA_DIM].astype(BF16)
        p = _softmax_rows(_dot_nt(q, k) * scale)
        oa_ref[0, :, sl] = _dot(p.astype(BF16), v).astype(BF16)
    lam = _diff_lambda(lam_ref, lam_init)
    m1 = _map1_mask()
    dscale = DF_QK ** -0.5
    for h in range(DF_HEADS):
        sl = slice(h * DF_V, (h + 1) * DF_V)
        q = p_ref[0, :, OFF_DQ + h * LANES:OFF_DQ + (h + 1) * LANES] * dscale
        k = p_ref[0, :, OFF_DK + h * LANES:OFF_DK + (h + 1) * LANES].astype(BF16)
        v = p_ref[0, :, OFF_DV + h * DF_V:OFF_DV + (h + 1) * DF_V].astype(BF16)
        qa = jnp.where(m1, q, 0.0).astype(BF16)
        qb = jnp.where(m1, 0.0, q).astype(BF16)
        p = _diff_probs(_dot_nt(qa, k), _dot_nt(qb, k), lam)
        o = _dot(p.astype(BF16), v)
        oc_ref[0, :, sl] = _subln(o, g_ref[...], lam_init).astype(BF16)


def _ctx_attn(proj3, lam_p, subln_g, lam_init):
    b, t, nw = proj3.shape
    return pl.pallas_call(
        functools.partial(_ctx_attn_kernel, lam_init=lam_init),
        grid=(b,),
        in_specs=[pl.BlockSpec((1, t, nw), lambda i: (i, 0, 0)),
                  pl.BlockSpec((4, DF_QK), lambda i: (0, 0)),
                  pl.BlockSpec((1, DF_V), lambda i: (0, 0))],
        out_specs=[pl.BlockSpec((1, t, NA_WIDTH), lambda i: (i, 0, 0)),
                   pl.BlockSpec((1, t, DF_WIDTH), lambda i: (i, 0, 0))],
        out_shape=[jax.ShapeDtypeStruct((b, t, NA_WIDTH), BF16),
                   jax.ShapeDtypeStruct((b, t, DF_WIDTH), BF16)],
        compiler_params=_cp(("parallel",), 32),
        name="ctx_attn",
    )(proj3, lam_p, subln_g.reshape(1, DF_V))


NA_QROWS = 4
NA_BAND = NA_WIN_R + NA_QROWS


def _na_group_layout(rows):
    groups = []
    for g in range(rows // NA_QROWS):
        bs = int(np.clip(g * NA_QROWS - NA_WIN_R // 2, 0, rows - NA_BAND))
        pat = []
        for a in range(NA_QROWS):
            r = g * NA_QROWS + a
            r0 = int(np.clip(r - NA_WIN_R // 2, 0, rows - NA_WIN_R))
            pat.append((r0 - bs, r - r0))
        groups.append((bs, tuple(pat)))
    cases = [groups[0][1], groups[1][1], groups[-1][1]]
    assert all(p == cases[1] for _, p in groups[1:-1]) and rows >= 2 * NA_BAND
    assert all(0 <= off and off + NA_WIN_R <= NA_BAND for c in cases for off, _ in c)
    return cases


def _lat_na_kernel(q_ref, k_ref, v_ref, kc_ref, vc_ref, bias_ref, o_ref, kb_ref, vb_ref, *, rows):
    kb_ref[...] = k_ref[0].astype(BF16)
    vb_ref[...] = v_ref[0].astype(BF16)
    kc = kc_ref[0, 0, 0].astype(BF16)
    vc = vc_ref[0, 0, 0].astype(BF16)
    scale = NA_DIM ** -0.5
    nq = NA_QROWS * GRID_W
    band = NA_BAND * GRID_W
    ngroups = rows // NA_QROWS

    def body(g, carry):
        bs = jnp.clip(g * NA_QROWS - NA_WIN_R // 2, 0, rows - NA_BAND)
        case = jnp.where(g == 0, 0, jnp.where(g == ngroups - 1, 2, 1))
        q0 = pl.multiple_of(g * nq, nq)
        q = q_ref[0, pl.ds(q0, nq), :].astype(BF16)
        k0 = pl.multiple_of(bs * GRID_W, GRID_W)
        kw = kb_ref[pl.ds(k0, band), :]
        vw = vb_ref[pl.ds(k0, band), :]
        s_loc = _dot_nt(q, kw) * scale + bias_ref[0, case]
        s_ctx = _dot_nt(q, kc) * scale
        m = jnp.maximum(jnp.max(s_loc, -1, keepdims=True), jnp.max(s_ctx, -1, keepdims=True))
        e_loc = jnp.exp(s_loc - m)
        e_ctx = jnp.exp(s_ctx - m)
        inv = 1.0 / (jnp.sum(e_loc, -1, keepdims=True) + jnp.sum(e_ctx, -1, keepdims=True))
        o = _dot((e_loc * inv).astype(BF16), vw) + _dot((e_ctx * inv).astype(BF16), vc)
        o_ref[0, pl.ds(q0, nq), :] = o.astype(BF16)
        return carry

    lax.fori_loop(0, ngroups, body, 0)


def _lat_na(proj3, cache_k, cache_v, bias, layer):
    b, t, _ = proj3.shape
    past = cache_k.shape[3]
    rows = t // GRID_W
    qb, kb, vb = OFF_QA // NA_DIM, OFF_KA // NA_DIM, OFF_VA // NA_DIM
    return pl.pallas_call(
        functools.partial(_lat_na_kernel, rows=rows),
        grid=(b, NA_HEADS),
        in_specs=[pl.BlockSpec((1, t, NA_DIM), lambda i, h: (i, 0, qb + h)),
                  pl.BlockSpec((1, t, NA_DIM), lambda i, h: (i, 0, kb + h)),
                  pl.BlockSpec((1, t, NA_DIM), lambda i, h: (i, 0, vb + h)),
                  pl.BlockSpec((1, 1, 1, past, NA_DIM), lambda i, h: (i, layer, h, 0, 0)),
                  pl.BlockSpec((1, 1, 1, past, NA_DIM), lambda i, h: (i, layer, h, 0, 0)),
                  pl.BlockSpec((1, 3, NA_QROWS * GRID_W, NA_BAND * GRID_W), lambda i, h: (h, 0, 0, 0))],
        out_specs=pl.BlockSpec((1, t, NA_DIM), lambda i, h: (i, 0, h)),
        out_shape=jax.ShapeDtypeStruct((b, t, NA_WIDTH), BF16),
        scratch_shapes=[pltpu.VMEM((t, NA_DIM), BF16), pltpu.VMEM((t, NA_DIM), BF16)],
        compiler_params=_cp(("parallel", "parallel"), 40),
        name="lat_na",
    )(proj3, proj3, proj3, cache_k, cache_v, bias)


def _lat_df_kernel(q_ref, k_ref, v_ref, kc_ref, vc_ref, cos_ref, sin_ref, lam_ref, g_ref, o_ref,
                   kall_ref, vall_ref, *, t, lam_init):
    qi = pl.program_id(2)

    @pl.when(qi == 0)
    def _():
        k = k_ref[0]
        kall_ref[0:t, :] = (k * cos_ref[...] + pltpu.roll(k, LANES // 2, 1) * sin_ref[...]).astype(BF16)
        kall_ref[t:, :] = kc_ref[0, 0, 0].astype(BF16)
        vall_ref[0:t, :] = v_ref[0].astype(BF16)
        vall_ref[t:, :] = vc_ref[0, 0, 0].astype(BF16)

    t0 = pl.multiple_of(qi * TQ_DF, TQ_DF)
    q = q_ref[0]
    q = q * cos_ref[pl.ds(t0, TQ_DF), :] + pltpu.roll(q, LANES // 2, 1) * sin_ref[pl.ds(t0, TQ_DF), :]
    q = q * DF_QK ** -0.5
    m1 = _map1_mask()
    qa = jnp.where(m1, q, 0.0).astype(BF16)
    qb = jnp.where(m1, 0.0, q).astype(BF16)
    lam = _diff_lambda(lam_ref, lam_init)
    kall = kall_ref[...]
    p = _diff_probs(_dot_nt(qa, kall), _dot_nt(qb, kall), lam)
    o = _dot(p.astype(BF16), vall_ref[...])
    o_ref[0] = _subln(o, g_ref[...], lam_init).astype(BF16)


def _lat_df(proj3, cache_kp, cache_v, cos_t, sin_t, lam_p, subln_g, lam_init, layer):
    b, t, _ = proj3.shape
    past = cache_v.shape[3]
    qb, kb, vb = OFF_DQ // LANES, OFF_DK // LANES, OFF_DV // LANES
    return pl.pallas_call(
        functools.partial(_lat_df_kernel, t=t, lam_init=lam_init),
        grid=(b, DF_HEADS, t // TQ_DF),
        in_specs=[pl.BlockSpec((1, TQ_DF, LANES), lambda i, h, j: (i, j, qb + h)),
                  pl.BlockSpec((1, t, LANES), lambda i, h, j: (i, 0, kb + h)),
                  pl.BlockSpec((1, t, LANES), lambda i, h, j: (i, 0, vb + h)),
                  pl.BlockSpec((1, 1, 1, past, LANES), lambda i, h, j: (i, layer, h, 0, 0)),
                  pl.BlockSpec((1, 1, 1, past, DF_V), lambda i, h, j: (i, layer, h, 0, 0)),
                  pl.BlockSpec((t, LANES), lambda i, h, j: (0, 0)),
                  pl.BlockSpec((t, LANES), lambda i, h, j: (0, 0)),
                  pl.BlockSpec((4, DF_QK), lambda i, h, j: (0, 0)),
                  pl.BlockSpec((1, DF_V), lambda i, h, j: (0, 0))],
        out_specs=pl.BlockSpec((1, TQ_DF, DF_V), lambda i, h, j: (i, j, h)),
        out_shape=jax.ShapeDtypeStruct((b, t, DF_WIDTH), BF16),
        scratch_shapes=[pltpu.VMEM((t + past, LANES), BF16), pltpu.VMEM((t + past, DF_V), BF16)],
        compiler_params=_cp(("parallel", "parallel", "arbitrary"), 48),
        name="lat_df",
    )(proj3, proj3, proj3, cache_kp, cache_v, cos_t, sin_t, lam_p, subln_g.reshape(1, DF_V))


def _sg_kernel(u_ref, v_ref, g_ref, b_ref, ws_ref, bs_ref, o_ref, *, nch):
    for c in range(nch):
        rs = slice(c * SG_CHUNK, (c + 1) * SG_CHUNK)
        vn = _layernorm(v_ref[0, rs, :], g_ref[...], b_ref[...]).astype(BF16)
        for g in range(SG_GROUPS):
            cs = slice(g * SG_DIM, (g + 1) * SG_DIM)
            mixed = _dot(ws_ref[g], vn[:, cs]) + bs_ref[g]
            o_ref[0, rs, cs] = (u_ref[0, rs, cs] * mixed).astype(BF16)


def _spatial_gate(proj3, ln_g, ln_b, ws, bs):
    b, t, _ = proj3.shape
    nch = min(4, t // SG_CHUNK)
    tt = nch * SG_CHUNK
    ub, vb = OFF_U // SG_WIDTH, OFF_V // SG_WIDTH
    return pl.pallas_call(
        functools.partial(_sg_kernel, nch=nch),
        grid=(b, t // tt),
        in_specs=[pl.BlockSpec((1, tt, SG_WIDTH), lambda i, j: (i, j, ub)),
                  pl.BlockSpec((1, tt, SG_WIDTH), lambda i, j: (i, j, vb)),
                  pl.BlockSpec((1, SG_WIDTH), lambda i, j: (0, 0)),
                  pl.BlockSpec((1, SG_WIDTH), lambda i, j: (0, 0)),
                  pl.BlockSpec((SG_GROUPS, SG_CHUNK, SG_CHUNK), lambda i, j: (0, 0, 0)),
                  pl.BlockSpec((SG_GROUPS, SG_CHUNK, 1), lambda i, j: (0, 0, 0))],
        out_specs=pl.BlockSpec((1, tt, SG_WIDTH), lambda i, j: (i, j, 0)),
        out_shape=jax.ShapeDtypeStruct((b, t, SG_WIDTH), BF16),
        compiler_params=_cp(("parallel", "parallel"), 16),
        name="spatial_gate",
    )(proj3, proj3, ln_g.reshape(1, SG_WIDTH), ln_b.reshape(1, SG_WIDTH), ws, bs)


def _cache_out_kernel(ka_ref, va_ref, dk_ref, dv_ref, *refs):
    ok, ov, ok1, ok2, odv = refs[5:]
    half = DF_QK // 2
    for h in range(NA_HEADS):
        ok[0, 0, h] = ka_ref[0, :, h * NA_DIM:(h + 1) * NA_DIM]
        ov[0, 0, h] = va_ref[0, :, h * NA_DIM:(h + 1) * NA_DIM]
    for h in range(DF_HEADS):
        kk = dk_ref[0, :, h * LANES:(h + 1) * LANES]
        ok1[0, 0, h] = jnp.concatenate([kk[:, 0:half], kk[:, 2 * half:3 * half]], -1)
        ok2[0, 0, h] = jnp.concatenate([kk[:, half:2 * half], kk[:, 3 * half:]], -1)
        odv[0, 0, h] = dv_ref[0, :, h * DF_V:(h + 1) * DF_V]


def _cache_out(proj3, caches, layer):
    b, t, _ = proj3.shape
    any_spec = pl.BlockSpec(memory_space=pl.ANY)

    def head_spec(c):
        return pl.BlockSpec((1, 1) + c.shape[2:], lambda i: (i, layer, 0, 0, 0))

    return pl.pallas_call(
        _cache_out_kernel,
        grid=(b,),
        in_specs=[pl.BlockSpec((1, t, NA_WIDTH), lambda i: (i, 0, OFF_KA // NA_WIDTH)),
                  pl.BlockSpec((1, t, NA_WIDTH), lambda i: (i, 0, OFF_VA // NA_WIDTH)),
                  pl.BlockSpec((1, t, DF_WIDTH), lambda i: (i, 0, OFF_DK // DF_WIDTH)),
                  pl.BlockSpec((1, t, DF_WIDTH), lambda i: (i, 0, OFF_DV // DF_WIDTH))] + [any_spec] * 5,
        out_specs=[head_spec(c) for c in caches],
        out_shape=[jax.ShapeDtypeStruct(c.shape, c.dtype) for c in caches],
        input_output_aliases={4 + k: k for k in range(5)},
        compiler_params=_cp(("parallel",), 32),
        name="cache_out",
    )(proj3, proj3, proj3, proj3, *caches)


def _out_proj_kernel(oa_ref, ob_ref, oc_ref, x_ref, mod_ref, w_ref, lg_ref, lb_ref, x1_ref, h2t_ref):
    y = (_dot(oa_ref[...], w_ref[0:NA_WIDTH, :])
         + _dot(ob_ref[...], w_ref[NA_WIDTH:NA_WIDTH + SG_WIDTH, :])
         + _dot(oc_ref[...], w_ref[NA_WIDTH + SG_WIDTH:, :]))
    g1 = mod_ref[0, 2:3, :]
    sh2 = mod_ref[0, 3:4, :]
    sc2 = mod_ref[0, 4:5, :]
    x1 = _layernorm(DN_ALPHA * x_ref[...] + g1 * y, lg_ref[...], lb_ref[...])
    x1_ref[...] = x1
    h2t_ref[...] = (x1 * (1 + sc2) + sh2).T.astype(BF16)


def _out_proj(oa, ob, oc, x, mod, w, ln_g, ln_b, tiles_per_mod):
    n, d = x.shape
    return pl.pallas_call(
        _out_proj_kernel,
        grid=(n // TM_OUT,),
        in_specs=[pl.BlockSpec((TM_OUT, NA_WIDTH), lambda i: (i, 0)),
                  pl.BlockSpec((TM_OUT, SG_WIDTH), lambda i: (i, 0)),
                  pl.BlockSpec((TM_OUT, DF_WIDTH), lambda i: (i, 0)),
                  pl.BlockSpec((TM_OUT, d), lambda i: (i, 0)),
                  pl.BlockSpec((1, 6, d), lambda i: (i // tiles_per_mod, 0, 0)),
                  pl.BlockSpec((MIX_WIDTH, d), lambda i: (0, 0)),
                  pl.BlockSpec((1, d), lambda i: (0, 0)),
                  pl.BlockSpec((1, d), lambda i: (0, 0))],
        out_specs=[pl.BlockSpec((TM_OUT, d), lambda i: (i, 0)),
                   pl.BlockSpec((d, TM_OUT), lambda i: (0, i))],
        out_shape=[jax.ShapeDtypeStruct((n, d), F32),
                   jax.ShapeDtypeStruct((d, n), BF16)],
        compiler_params=_cp(("parallel",), 48),
        name="out_proj",
    )(oa, ob, oc, x, mod, w, ln_g.reshape(1, d), ln_b.reshape(1, d))


_PK_PAIRS = [(a, b) for a in range(PK_TOPK) for b in range(PK_TOPK) if (a + 1) * (b + 1) <= PK_TOPK]


def _merge_exchange_network(n):
    t = int(math.ceil(math.log2(n)))
    p = 2 ** (t - 1)
    pairs = []
    while p > 0:
        q, r, d = 2 ** (t - 1), 0, p
        while d > 0:
            pairs.extend((i, i + d) for i in range(n - d) if i & p == r)
            d, q, r = q - p, q // 2, p
        p //= 2
    return pairs


def _bitonic_merge_network(n):
    pairs, d = [], n // 2
    while d >= 1:
        pairs.extend((i, i + d) for i in range(n) if i & d == 0)
        d //= 2
    return pairs


_SORT16 = _merge_exchange_network(PK_TOPK)
_MERGE16 = _bitonic_merge_network(PK_TOPK)


def _compare_exchange(xs, network):
    xs = list(xs)
    for i, j in network:
        xs[i], xs[j] = jnp.maximum(xs[i], xs[j]), jnp.minimum(xs[i], xs[j])
    return xs


def _peer_score_kernel(h2t_ref, wqt_ref, keys_ref, rank2_out, e2_out, cnt_out, q1_out,
                       qt_ref, s_ref, tt_ref, thr_ref, z_ref):
    tm = h2t_ref.shape[1]
    qt_ref[...] = _dot(wqt_ref[...], h2t_ref[...]).astype(BF16)
    neg_inf = -jnp.inf
    sub = 8

    for c in range(2 * PK_HEADS):
        s = _dot(keys_ref[c], qt_ref[c * PK_NKEYS:(c + 1) * PK_NKEYS, :])
        s_ref[c] = s
        head, half = c // 2, c % 2
        xs = _compare_exchange([s[r * sub:(r + 1) * sub, :] for r in range(PK_TOPK)], _SORT16)
        shift = 1
        while shift < sub:
            ys = [pltpu.roll(xs[PK_TOPK - 1 - r], shift, 0) for r in range(PK_TOPK)]
            xs = _compare_exchange([jnp.maximum(x, y) for x, y in zip(xs, ys)], _MERGE16)
            shift *= 2
        for r in range(PK_TOPK):
            tt_ref[half, r, head:head + 1, :] = xs[r][0:1, :]

    def per_lane_chunk(ch, carry):
        l0 = ch * LANES
        t1 = [tt_ref[0, a, :, pl.ds(l0, LANES)] for a in range(PK_TOPK)]
        t2 = [tt_ref[1, b, :, pl.ds(l0, LANES)] for b in range(PK_TOPK)]
        cands = [t1[a] + t2[b] for a, b in _PK_PAIRS]
        cur = list(cands)
        for _ in range(PK_TOPK - 1):
            m = functools.reduce(jnp.maximum, cur)
            found = jnp.zeros(m.shape, jnp.bool_)
            nxt = []
            for cnd in cur:
                is_m = cnd == m
                nxt.append(jnp.where(jnp.logical_and(is_m, jnp.logical_not(found)), neg_inf, cnd))
                found = jnp.logical_or(found, is_m)
            cur = nxt
        thr = functools.reduce(jnp.maximum, cur)
        e1 = [jnp.exp(t1[a] - t1[0]) for a in range(PK_TOPK)]
        e2 = [jnp.exp(t2[b] - t2[0]) for b in range(PK_TOPK)]
        z = jnp.zeros(thr.shape, F32)
        for (a, b), cnd in zip(_PK_PAIRS, cands):
            z = z + jnp.where(cnd >= thr, e1[a] * e2[b], 0.0)
        thr_ref[:, pl.ds(l0, LANES)] = thr
        z_ref[:, pl.ds(l0, LANES)] = z
        return carry

    for ch in range(tm // LANES):
        per_lane_chunk(ch, 0)

    for h in range(PK_HEADS):
        s1 = s_ref[2 * h]
        s2 = s_ref[2 * h + 1]
        thr = thr_ref[h:h + 1, :]
        cnt = jnp.zeros(s1.shape, F32)
        rank2 = jnp.zeros(s2.shape, F32)
        for b in range(PK_TOPK):
            t2b = tt_ref[1, b, h:h + 1, :]
            cnt = jnp.where(s1 + t2b >= thr, float(b + 1), cnt)
            rank2 = jnp.where(t2b > s2, float(b + 1), rank2)
        cnt_out[h] = cnt
        rank2_out[h] = rank2.astype(BF16)
        q1_out[h] = jnp.exp(s1 - tt_ref[0, 0, h:h + 1, :]) * (0.5 / z_ref[h:h + 1, :])
        e2_out[h] = jnp.exp(s2 - tt_ref[1, 0, h:h + 1, :]).astype(BF16)


def _peer_score(h2t, wqt, keys):
    d, n = h2t.shape
    spec = pl.BlockSpec((PK_HEADS, PK_NKEYS, TM_SCORE), lambda i: (0, 0, i))
    shape = jax.ShapeDtypeStruct((PK_HEADS, PK_NKEYS, n), F32)
    shape16 = jax.ShapeDtypeStruct((PK_HEADS, PK_NKEYS, n), BF16)
    return pl.pallas_call(
        _peer_score_kernel,
        grid=(n // TM_SCORE,),
        in_specs=[pl.BlockSpec((d, TM_SCORE), lambda i: (0, i)),
                  pl.BlockSpec(wqt.shape, lambda i: (0, 0)),
                  pl.BlockSpec(keys.shape, lambda i: (0, 0, 0))],
        out_specs=[spec, spec, spec, spec],
        out_shape=[shape16, shape16, shape, shape],
        scratch_shapes=[pltpu.VMEM((PK_HEADS * PK_QDIM, TM_SCORE), BF16),
                        pltpu.VMEM((2 * PK_HEADS, PK_NKEYS, TM_SCORE), F32),
                        pltpu.VMEM((2, PK_TOPK, PK_HEADS, TM_SCORE), F32),
                        pltpu.VMEM((PK_HEADS, TM_SCORE), F32),
                        pltpu.VMEM((PK_HEADS, TM_SCORE), F32)],
        compiler_params=_cp(("parallel",), 48),
        name="peer_score",
    )(h2t, wqt, keys)


def _peer_dense_kernel(h2t_ref, u_ref, vt_ref, rank2_ref, e2_ref, cnt_ref, q1_ref, o_ref,
                       acc_ref, a_ref, w_ref, row_ref):
    sqrt_half = math.sqrt(0.5)

    @pl.when(pl.program_id(1) == 0)
    def _():
        acc_ref[...] = jnp.zeros_like(acc_ref)

    tm = h2t_ref.shape[1]
    zero = jnp.zeros((RJ_PEER, tm), BF16)
    a_ref[...] = _dot(u_ref[...], h2t_ref[...])
    for ii in range(TE_PEER // PK_NKEYS):
        for h in range(PK_HEADS):
            row_ref[0, h] = jnp.broadcast_to(cnt_ref[h, ii:ii + 1, :], (RJ_PEER, tm)).astype(BF16)
            row_ref[1, h] = jnp.broadcast_to(q1_ref[h, ii:ii + 1, :], (RJ_PEER, tm)).astype(BF16)
        for jc in range(PK_NKEYS // RJ_PEER):
            js = slice(jc * RJ_PEER, (jc + 1) * RJ_PEER)
            rs = slice(ii * PK_NKEYS + jc * RJ_PEER, ii * PK_NKEYS + (jc + 1) * RJ_PEER)
            g = None
            for h in range(PK_HEADS):
                term = jnp.where(rank2_ref[h, js, :] < row_ref[0, h], e2_ref[h, js, :], zero) * row_ref[1, h]
                g = term if g is None else g + term
            a = a_ref[rs, :]
            w_ref[rs, :] = g * (a * (1.0 + lax.erf(a * sqrt_half))).astype(BF16)
    acc_ref[...] += _dot(vt_ref[...], w_ref[...])

    @pl.when(pl.program_id(1) == pl.num_programs(1) - 1)
    def _():
        o_ref[...] = acc_ref[...]


def _peer_dense(h2t, u_tabs, vt_tabs, layer, rank2, e2, cnt, q1):
    d, n = h2t.shape
    ne = u_tabs.shape[1]
    ri = TE_PEER // PK_NKEYS
    gspec = pl.BlockSpec((PK_HEADS, PK_NKEYS, TM_PEER), lambda i, e: (0, 0, i))
    rspec = pl.BlockSpec((PK_HEADS, ri, TM_PEER), lambda i, e: (0, e, i))
    return pl.pallas_call(
        _peer_dense_kernel,
        grid=(n // TM_PEER, ne // TE_PEER),
        in_specs=[pl.BlockSpec((d, TM_PEER), lambda i, e: (0, i)),
                  pl.BlockSpec((None, TE_PEER, d), lambda i, e: (layer, e, 0)),
                  pl.BlockSpec((None, d, TE_PEER), lambda i, e: (layer, 0, e)),
                  gspec, gspec, rspec, rspec],
        out_specs=pl.BlockSpec((d, TM_PEER), lambda i, e: (0, i)),
        out_shape=jax.ShapeDtypeStruct((d, n), F32),
        scratch_shapes=[pltpu.VMEM((d, TM_PEER), F32),
                        pltpu.VMEM((TE_PEER, TM_PEER), F32),
                        pltpu.VMEM((TE_PEER, TM_PEER), BF16),
                        pltpu.VMEM((2, PK_HEADS, RJ_PEER, TM_PEER), BF16)],
        compiler_params=_cp(("parallel", "arbitrary"), 56),
        name="peer_dense",
    )(h2t, u_tabs, vt_tabs, rank2, e2, cnt, q1)


def _peer_ln_kernel(yt_ref, x1_ref, mod_ref, lg_ref, lb_ref, o_ref):
    g2 = mod_ref[0, 5:6, :]
    o_ref[...] = _layernorm(DN_ALPHA * x1_ref[...] + g2 * yt_ref[...].T, lg_ref[...], lb_ref[...])


def _peer_ln(yt, x1, mod, ln_g, ln_b, tiles_per_mod):
    n, d = x1.shape
    return pl.pallas_call(
        _peer_ln_kernel,
        grid=(n // TM_OUT,),
        in_specs=[pl.BlockSpec((d, TM_OUT), lambda i: (0, i)),
                  pl.BlockSpec((TM_OUT, d), lambda i: (i, 0)),
                  pl.BlockSpec((1, 6, d), lambda i: (i // tiles_per_mod, 0, 0)),
                  pl.BlockSpec((1, d), lambda i: (0, 0)),
                  pl.BlockSpec((1, d), lambda i: (0, 0))],
        out_specs=pl.BlockSpec((TM_OUT, d), lambda i: (i, 0)),
        out_shape=jax.ShapeDtypeStruct((n, d), F32),
        compiler_params=_cp(("parallel",), 40),
        name="peer_ln",
    )(yt, x1, mod, ln_g.reshape(1, d), ln_b.reshape(1, d))


def _permute_in_proj(w):
    d = w.shape[0]

    def interleave(cols):
        c = cols.reshape(d, 2, DF_HEADS, 2, DF_QK // 2)
        return c.transpose(0, 2, 3, 1, 4).reshape(d, 2 * DF_HEADS * DF_QK)

    return jnp.concatenate([w[:, :OFF_DQ], interleave(w[:, OFF_DQ:OFF_DK]),
                            interleave(w[:, OFF_DK:OFF_DV]), w[:, OFF_DV:]], axis=1).astype(BF16)


def _pair_lanes(a, b):
    half = DF_QK // 2
    return jnp.concatenate([a[..., :half], b[..., :half], a[..., half:], b[..., half:]], -1)


def _rope_tables(t):
    tok = jnp.arange(t)
    row = (tok // GRID_W).astype(F32)
    col = (tok % GRID_W).astype(F32)
    n_freq = DF_QK // 4
    inv = 1.0 / (ROPE_THETA ** (jnp.arange(n_freq, dtype=F32) / n_freq))
    ang = jnp.concatenate([row[:, None] * inv, col[:, None] * inv], -1)
    cos, sin = jnp.cos(ang), jnp.sin(ang)
    return jnp.concatenate([cos] * 4, -1), jnp.concatenate([-sin, -sin, sin, sin], -1)


def _na_bias_table(rpb, rows):
    cases = _na_group_layout(rows)
    cols = np.arange(GRID_W)
    start = np.clip(cols - NA_WIN_C // 2, 0, GRID_W - NA_WIN_C)
    inwin = (cols[None, :] >= start[:, None]) & (cols[None, :] < start[:, None] + NA_WIN_C)
    nh, ndr = rpb.shape[0], 2 * NA_WIN_R - 1
    lead = GRID_W - NA_WIN_C
    p = jnp.pad(rpb, ((0, 0), (0, 0), (lead, 2 * GRID_W - lead - (2 * NA_WIN_C - 1))))
    m = jnp.tile(p, (1, 1, GRID_W))[..., :GRID_W * (2 * GRID_W - 1)].reshape(nh, ndr, GRID_W, 2 * GRID_W - 1)
    blocks = jnp.where(inwin[None, None], m[..., GRID_W - 1:], NEG_BIG)
    blocks = jnp.concatenate([blocks, jnp.full((nh, 1, GRID_W, GRID_W), NEG_BIG, F32)], 1)
    idx = np.full((len(cases), NA_QROWS, NA_BAND), ndr, np.int32)
    for c, pat in enumerate(cases):
        for a, (off, cs) in enumerate(pat):
            for kr in range(off, off + NA_WIN_R):
                idx[c, a, kr] = kr - off - cs + NA_WIN_R - 1
    tab = jnp.take(blocks, idx.reshape(-1), axis=1)
    tab = tab.reshape(nh, len(cases), NA_QROWS, NA_BAND, GRID_W, GRID_W).transpose(0, 1, 2, 4, 3, 5)
    return tab.reshape(nh, len(cases), NA_QROWS * GRID_W, NA_BAND * GRID_W)


def _peer_block(oa, ob, oc, x, mod, tiles_out, w_out, ln_g, ln_b, wqt, keys, u_tabs, vt_tabs, layer):
    x1, h2t = _out_proj(oa, ob, oc, x, mod, w_out, ln_g[0], ln_b[0], tiles_out)
    rank2, e2, cnt, q1 = _peer_score(h2t, wqt, keys)
    yt = _peer_dense(h2t, u_tabs, vt_tabs, layer, rank2, e2, cnt, q1)
    return _peer_ln(yt, x1, mod, ln_g[1], ln_b[1], tiles_out)


def kernel(x_prompt, x_sample, cache_na_k, cache_na_v, cache_df_k1, cache_df_k2, cache_df_v, c, c_ctx,
           w_mod, b_mod, w_in, na_rpb, sg_ln_g, sg_ln_b, sg_w, sg_b, df_lambda, df_subln_g, w_out,
           pk_wq, pk_keys, pk_u, pk_v, ln_g, ln_b):
    bp, tp, d = x_prompt.shape
    bs, ts, _ = x_sample.shape
    n_p, n_s = bp * tp, bs * ts

    cond8 = jnp.zeros((8, d), F32).at[0].set(c_ctx).at[1:1 + bs].set(c)
    mods = _modulation(cond8, w_mod, b_mod).reshape(DEPTH, 8, 6, d)

    cos_t, sin_t = _rope_tables(ts)
    cache_dk = _pair_lanes(cache_df_k1, cache_df_k2)
    u_tabs = pk_u.astype(BF16)
    vt_tabs = pk_v.transpose(0, 2, 1).astype(BF16)

    xp = x_prompt.reshape(n_p, d)
    xs = x_sample.reshape(n_s, d)
    caches_out = [jnp.zeros((bp, DEPTH, NA_HEADS, tp, NA_DIM), F32), jnp.zeros((bp, DEPTH, NA_HEADS, tp, NA_DIM), F32),
                  jnp.zeros((bp, DEPTH, DF_HEADS, tp, DF_QK), F32), jnp.zeros((bp, DEPTH, DF_HEADS, tp, DF_QK), F32),
                  jnp.zeros((bp, DEPTH, DF_HEADS, tp, DF_V), F32)]
    for l in range(DEPTH):
        lam_init = 0.8 - 0.6 * math.exp(-0.3 * l)
        w_in_l = _permute_in_proj(w_in[l])
        w_out_l = w_out[l].astype(BF16)
        wqt = pk_wq[l].T.astype(BF16)
        keys = pk_keys[l].reshape(2 * PK_HEADS, PK_NKEYS, PK_QDIM // 2).astype(BF16)
        ws = sg_w[l].astype(BF16)
        bs_col = sg_b[l].reshape(SG_GROUPS, SG_CHUNK, 1)
        bias = _na_bias_table(na_rpb[l], ts // GRID_W)
        mod_p = mods[l, 0:1]
        mod_s = mods[l, 1:1 + bs]

        proj = _in_proj(xp, mod_p, w_in_l, n_p // TM_IN)
        proj3 = proj.reshape(bp, tp, IN_WIDTH)
        oa, oc = _ctx_attn(proj3, df_lambda[l], df_subln_g[l], lam_init)
        ob = _spatial_gate(proj3, sg_ln_g[l], sg_ln_b[l], ws, bs_col)
        caches_out = _cache_out(proj3, caches_out, l)
        xp = _peer_block(oa.reshape(n_p, NA_WIDTH), ob.reshape(n_p, SG_WIDTH), oc.reshape(n_p, DF_WIDTH),
                         xp, mod_p, n_p // TM_OUT, w_out_l, ln_g[l], ln_b[l], wqt, keys, u_tabs, vt_tabs, l)

        proj = _in_proj(xs, mod_s, w_in_l, ts // TM_IN)
        proj3 = proj.reshape(bs, ts, IN_WIDTH)
        oa = _lat_na(proj3, cache_na_k, cache_na_v, bias, l)
        ob = _spatial_gate(proj3, sg_ln_g[l], sg_ln_b[l], ws, bs_col)
        oc = _lat_df(proj3, cache_dk, cache_df_v, cos_t, sin_t, df_lambda[l], df_subln_g[l], lam_init, l)
        xs = _peer_block(oa.reshape(n_s, NA_WIDTH), ob.reshape(n_s, SG_WIDTH), oc.reshape(n_s, DF_WIDTH),
                         xs, mod_s, ts // TM_OUT, w_out_l, ln_g[l], ln_b[l], wqt, keys, u_tabs, vt_tabs, l)

    return (xp.reshape(bp, tp, d), xs.reshape(bs, ts, d), *caches_out)
```

```python
import functools
import math

import numpy as np
import jax
import jax.numpy as jnp
from jax import lax
from jax.experimental import pallas as pl
from jax.experimental.pallas import tpu as pltpu

F32 = jnp.float32
BF16 = jnp.bfloat16

D_MODEL = 2048
DEPTH = 4
GRID_W = 64
NA_HEADS = 8
NA_DIM = 128
NA_WIN_R = 8
NA_WIN_C = 16
SG_GROUPS = 4
SG_DIM = 128
SG_CHUNK = 128
DF_HEADS = 4
DF_QK = 64
DF_V = 128
ROPE_THETA = 10000.0
NA_WIDTH = NA_HEADS * NA_DIM
SG_WIDTH = SG_GROUPS * SG_DIM
DF_WIDTH = DF_HEADS * DF_V
MIX_WIDTH = NA_WIDTH + SG_WIDTH + DF_WIDTH
IN_WIDTH = 3 * NA_WIDTH + 2 * SG_WIDTH + 4 * DF_HEADS * DF_QK + DF_WIDTH
PK_HEADS = 8
PK_QDIM = 256
PK_NKEYS = 128
PK_TOPK = 16
PK_EXPERTS = PK_NKEYS * PK_NKEYS
DN_ALPHA = (2 * DEPTH) ** 0.25
LN_EPS = 1e-5
RMS_EPS = 1e-6

OFF_QA = 0
OFF_KA = NA_WIDTH
OFF_VA = 2 * NA_WIDTH
OFF_U = 3 * NA_WIDTH
OFF_V = OFF_U + SG_WIDTH
OFF_DQ = OFF_V + SG_WIDTH
OFF_DK = OFF_DQ + 2 * DF_HEADS * DF_QK
OFF_DV = OFF_DK + 2 * DF_HEADS * DF_QK

LANES = 128
SUBLANES = 8
MIB = 1024 * 1024

TM_IN = 512
TN_IN = IN_WIDTH // 4
TM_OUT = 512
TM_SCORE = 256
TM_PEER = 512
TE_PEER = 1024
RJ_PEER = 16
TQ_DF = 256
TN_MOD = 1024
NEG_BIG = -1e30


def _cp(sem, vmem_mib):
    return pltpu.CompilerParams(dimension_semantics=sem, vmem_limit_bytes=vmem_mib * MIB)


def _dot(a, b):
    return jnp.dot(a, b, preferred_element_type=F32)


def _dot_nt(a, b):
    return lax.dot_general(a, b, (((1,), (1,)), ((), ())), preferred_element_type=F32)


def _layernorm(z, g, b):
    mu = jnp.mean(z, -1, keepdims=True)
    d = z - mu
    var = jnp.mean(d * d, -1, keepdims=True)
    return d * lax.rsqrt(var + LN_EPS) * g + b


def _diff_probs(s1, s2, lam):
    e1 = jnp.exp(s1 - jnp.max(s1, -1, keepdims=True))
    e2 = jnp.exp(s2 - jnp.max(s2, -1, keepdims=True))
    c1 = 1.0 / jnp.sum(e1, -1, keepdims=True)
    c2 = lam / jnp.sum(e2, -1, keepdims=True)
    return e1 * c1 - e2 * c2


def _softmax_rows(s):
    m = jnp.max(s, -1, keepdims=True)
    e = jnp.exp(s - m)
    return e * (1.0 / jnp.sum(e, -1, keepdims=True))


def _diff_lambda(lam_ref, lam_init):
    lf = lam_ref[...]
    a = jnp.sum(lf[0:1, :] * lf[1:2, :], axis=1, keepdims=True)
    b = jnp.sum(lf[2:3, :] * lf[3:4, :], axis=1, keepdims=True)
    return jnp.exp(a) - jnp.exp(b) + lam_init


def _map1_mask():
    lane = lax.broadcasted_iota(jnp.int32, (1, LANES), 1)
    return (lane // (DF_QK // 2)) % 2 == 0


def _subln(o, g, lam_init):
    return o * lax.rsqrt(jnp.mean(o * o, -1, keepdims=True) + RMS_EPS) * g * (1.0 - lam_init)


def _mod_kernel(c_ref, w_ref, b_ref, o_ref):
    c = c_ref[...]
    a = (c * jax.nn.sigmoid(c)).astype(BF16)
    o_ref[0] = _dot(a, w_ref[0].astype(BF16)) + b_ref[0]


def _modulation(cond8, w_mod, b_mod):
    depth, d, n6 = w_mod.shape
    return pl.pallas_call(
        _mod_kernel,
        grid=(depth, n6 // TN_MOD),
        in_specs=[pl.BlockSpec((8, d), lambda l, j: (0, 0)),
                  pl.BlockSpec((1, d, TN_MOD), lambda l, j: (l, 0, j)),
                  pl.BlockSpec((1, 1, TN_MOD), lambda l, j: (l, 0, j))],
        out_specs=pl.BlockSpec((1, 8, TN_MOD), lambda l, j: (l, 0, j)),
        out_shape=jax.ShapeDtypeStruct((depth, 8, n6), F32),
        compiler_params=_cp(("parallel", "parallel"), 32),
        name="modulation",
    )(cond8, w_mod, b_mod.reshape(depth, 1, n6))


def _in_proj_kernel(x_ref, mod_ref, w_ref, o_ref, h_ref):
    @pl.when(pl.program_id(1) == 0)
    def _():
        sh = mod_ref[0, 0:1, :]
        sc = mod_ref[0, 1:2, :]
        h_ref[...] = (x_ref[...] * (1 + sc) + sh).astype(BF16)

    o_ref[...] = _dot(h_ref[...], w_ref[...])


def _in_proj(x, mod, w, tiles_per_mod):
    n, d = x.shape
    nw = w.shape[1]
    return pl.pallas_call(
        _in_proj_kernel,
        grid=(n // TM_IN, nw // TN_IN),
        in_specs=[pl.BlockSpec((TM_IN, d), lambda i, j: (i, 0)),
                  pl.BlockSpec((1, 6, d), lambda i, j: (i // tiles_per_mod, 0, 0)),
                  pl.BlockSpec((d, TN_IN), lambda i, j: (0, j))],
        out_specs=pl.BlockSpec((TM_IN, TN_IN), lambda i, j: (i, j)),
        out_shape=jax.ShapeDtypeStruct((n, nw), F32),
        scratch_shapes=[pltpu.VMEM((TM_IN, d), BF16)],
        compiler_params=_cp(("parallel", "arbitrary"), 40),
        name="in_proj",
    )(x, mod, w)


def _ctx_attn_kernel(p_ref, lam_ref, g_ref, oa_ref, oc_ref, *, lam_init):
    scale = NA_DIM ** -0.5
    for h in range(NA_HEADS):
        sl = slice(h * NA_DIM, (h + 1) * NA_DIM)
        q = p_ref[0, :, OFF_QA + h * NA_DIM:OFF_QA + (h + 1) * NA_DIM].astype(BF16)
        k = p_ref[0, :, OFF_KA + h * NA_DIM:OFF_KA + (h + 1) * NA_DIM].astype(BF16)
        v = p_ref[0, :, OFF_VA + h * NA_DIM:OFF_VA + (h + 1) * NA_DIM].astype(BF16)
        p = _softmax_rows(_dot_nt(q, k) * scale)
        oa_ref[0, :, sl] = _dot(p.astype(BF16), v).astype(BF16)
    lam = _diff_lambda(lam_ref, lam_init)
    m1 = _map1_mask()
    dscale = DF_QK ** -0.5
    for h in range(DF_HEADS):
        sl = slice(h * DF_V, (h + 1) * DF_V)
        q = p_ref[0, :, OFF_DQ + h * LANES:OFF_DQ + (h + 1) * LANES] * dscale
        k = p_ref[0, :, OFF_DK + h * LANES:OFF_DK + (h + 1) * LANES].astype(BF16)
        v = p_ref[0, :, OFF_DV + h * DF_V:OFF_DV + (h + 1) * DF_V].astype(BF16)
        qa = jnp.where(m1, q, 0.0).astype(BF16)
        qb = jnp.where(m1, 0.0, q).astype(BF16)
        p = _diff_probs(_dot_nt(qa, k), _dot_nt(qb, k), lam)
        o = _dot(p.astype(BF16), v)
        oc_ref[0, :, sl] = _subln(o, g_ref[...], lam_init).astype(BF16)


def _ctx_attn(proj3, lam_p, subln_g, lam_init):
    b, t, nw = proj3.shape
    return pl.pallas_call(
        functools.partial(_ctx_attn_kernel, lam_init=lam_init),
        grid=(b,),
        in_specs=[pl.BlockSpec((1, t, nw), lambda i: (i, 0, 0)),
                  pl.BlockSpec((4, DF_QK), lambda i: (0, 0)),
                  pl.BlockSpec((1, DF_V), lambda i: (0, 0))],
        out_specs=[pl.BlockSpec((1, t, NA_WIDTH), lambda i: (i, 0, 0)),
                   pl.BlockSpec((1, t, DF_WIDTH), lambda i: (i, 0, 0))],
        out_shape=[jax.ShapeDtypeStruct((b, t, NA_WIDTH), BF16),
                   jax.ShapeDtypeStruct((b, t, DF_WIDTH), BF16)],
        compiler_params=_cp(("parallel",), 32),
        name="ctx_attn",
    )(proj3, lam_p, subln_g.reshape(1, DF_V))


NA_QROWS = 4
NA_BAND = NA_WIN_R + NA_QROWS


def _na_group_layout(rows):
    groups = []
    for g in range(rows // NA_QROWS):
        bs = int(np.clip(g * NA_QROWS - NA_WIN_R // 2, 0, rows - NA_BAND))
        pat = []
        for a in range(NA_QROWS):
            r = g * NA_QROWS + a
            r0 = int(np.clip(r - NA_WIN_R // 2, 0, rows - NA_WIN_R))
            pat.append((r0 - bs, r - r0))
        groups.append((bs, tuple(pat)))
    cases = [groups[0][1], groups[1][1], groups[-1][1]]
    assert all(p == cases[1] for _, p in groups[1:-1]) and rows >= 2 * NA_BAND
    assert all(0 <= off and off + NA_WIN_R <= NA_BAND for c in cases for off, _ in c)
    return cases


def _lat_na_kernel(q_ref, k_ref, v_ref, kc_ref, vc_ref, bias_ref, o_ref, kb_ref, vb_ref, *, rows):
    kb_ref[...] = k_ref[0].astype(BF16)
    vb_ref[...] = v_ref[0].astype(BF16)
    kc = kc_ref[0, 0, 0].astype(BF16)
    vc = vc_ref[0, 0, 0].astype(BF16)
    scale = NA_DIM ** -0.5
    nq = NA_QROWS * GRID_W
    band = NA_BAND * GRID_W
    ngroups = rows // NA_QROWS

    def body(g, carry):
        bs = jnp.clip(g * NA_QROWS - NA_WIN_R // 2, 0, rows - NA_BAND)
        case = jnp.where(g == 0, 0, jnp.where(g == ngroups - 1, 2, 1))
        q0 = pl.multiple_of(g * nq, nq)
        q = q_ref[0, pl.ds(q0, nq), :].astype(BF16)
        k0 = pl.multiple_of(bs * GRID_W, GRID_W)
        kw = kb_ref[pl.ds(k0, band), :]
        vw = vb_ref[pl.ds(k0, band), :]
        s_loc = _dot_nt(q, kw) * scale + bias_ref[0, case]
        s_ctx = _dot_nt(q, kc) * scale
        m = jnp.maximum(jnp.max(s_loc, -1, keepdims=True), jnp.max(s_ctx, -1, keepdims=True))
        e_loc = jnp.exp(s_loc - m)
        e_ctx = jnp.exp(s_ctx - m)
        inv = 1.0 / (jnp.sum(e_loc, -1, keepdims=True) + jnp.sum(e_ctx, -1, keepdims=True))
        o = _dot((e_loc * inv).astype(BF16), vw) + _dot((e_ctx * inv).astype(BF16), vc)
        o_ref[0, pl.ds(q0, nq), :] = o.astype(BF16)
        return carry

    lax.fori_loop(0, ngroups, body, 0)


def _lat_na(proj3, cache_k, cache_v, bias, layer):
    b, t, _ = proj3.shape
    past = cache_k.shape[3]
    rows = t // GRID_W
    qb, kb, vb = OFF_QA // NA_DIM, OFF_KA // NA_DIM, OFF_VA // NA_DIM
    return pl.pallas_call(
        functools.partial(_lat_na_kernel, rows=rows),
        grid=(b, NA_HEADS),
        in_specs=[pl.BlockSpec((1, t, NA_DIM), lambda i, h: (i, 0, qb + h)),
                  pl.BlockSpec((1, t, NA_DIM), lambda i, h: (i, 0, kb + h)),
                  pl.BlockSpec((1, t, NA_DIM), lambda i, h: (i, 0, vb + h)),
                  pl.BlockSpec((1, 1, 1, past, NA_DIM), lambda i, h: (i, layer, h, 0, 0)),
                  pl.BlockSpec((1, 1, 1, past, NA_DIM), lambda i, h: (i, layer, h, 0, 0)),
                  pl.BlockSpec((1, 3, NA_QROWS * GRID_W, NA_BAND * GRID_W), lambda i, h: (h, 0, 0, 0))],
        out_specs=pl.BlockSpec((1, t, NA_DIM), lambda i, h: (i, 0, h)),
        out_shape=jax.ShapeDtypeStruct((b, t, NA_WIDTH), BF16),
        scratch_shapes=[pltpu.VMEM((t, NA_DIM), BF16), pltpu.VMEM((t, NA_DIM), BF16)],
        compiler_params=_cp(("parallel", "parallel"), 40),
        name="lat_na",
    )(proj3, proj3, proj3, cache_k, cache_v, bias)


def _lat_df_kernel(q_ref, k_ref, v_ref, kc_ref, vc_ref, cos_ref, sin_ref, lam_ref, g_ref, o_ref,
                   kall_ref, vall_ref, *, t, lam_init):
    qi = pl.program_id(2)

    @pl.when(qi == 0)
    def _():
        k = k_ref[0]
        kall_ref[0:t, :] = (k * cos_ref[...] + pltpu.roll(k, LANES // 2, 1) * sin_ref[...]).astype(BF16)
        kall_ref[t:, :] = kc_ref[0, 0, 0].astype(BF16)
        vall_ref[0:t, :] = v_ref[0].astype(BF16)
        vall_ref[t:, :] = vc_ref[0, 0, 0].astype(BF16)

    t0 = pl.multiple_of(qi * TQ_DF, TQ_DF)
    q = q_ref[0]
    q = q * cos_ref[pl.ds(t0, TQ_DF), :] + pltpu.roll(q, LANES // 2, 1) * sin_ref[pl.ds(t0, TQ_DF), :]
    q = q * DF_QK ** -0.5
    m1 = _map1_mask()
    qa = jnp.where(m1, q, 0.0).astype(BF16)
    qb = jnp.where(m1, 0.0, q).astype(BF16)
    lam = _diff_lambda(lam_ref, lam_init)
    kall = kall_ref[...]
    p = _diff_probs(_dot_nt(qa, kall), _dot_nt(qb, kall), lam)
    o = _dot(p.astype(BF16), vall_ref[...])
    o_ref[0] = _subln(o, g_ref[...], lam_init).astype(BF16)


def _lat_df(proj3, cache_kp, cache_v, cos_t, sin_t, lam_p, subln_g, lam_init, layer):
    b, t, _ = proj3.shape
    past = cache_v.shape[3]
    qb, kb, vb = OFF_DQ // LANES, OFF_DK // LANES, OFF_DV // LANES
    return pl.pallas_call(
        functools.partial(_lat_df_kernel, t=t, lam_init=lam_init),
        grid=(b, DF_HEADS, t // TQ_DF),
        in_specs=[pl.BlockSpec((1, TQ_DF, LANES), lambda i, h, j: (i, j, qb + h)),
                  pl.BlockSpec((1, t, LANES), lambda i, h, j: (i, 0, kb + h)),
                  pl.BlockSpec((1, t, LANES), lambda i, h, j: (i, 0, vb + h)),
                  pl.BlockSpec((1, 1, 1, past, LANES), lambda i, h, j: (i, layer, h, 0, 0)),
                  pl.BlockSpec((1, 1, 1, past, DF_V), lambda i, h, j: (i, layer, h, 0, 0)),
                  pl.BlockSpec((t, LANES), lambda i, h, j: (0, 0)),
                  pl.BlockSpec((t, LANES), lambda i, h, j: (0, 0)),
                  pl.BlockSpec((4, DF_QK), lambda i, h, j: (0, 0)),
                  pl.BlockSpec((1, DF_V), lambda i, h, j: (0, 0))],
        out_specs=pl.BlockSpec((1, TQ_DF, DF_V), lambda i, h, j: (i, j, h)),
        out_shape=jax.ShapeDtypeStruct((b, t, DF_WIDTH), BF16),
        scratch_shapes=[pltpu.VMEM((t + past, LANES), BF16), pltpu.VMEM((t + past, DF_V), BF16)],
        compiler_params=_cp(("parallel", "parallel", "arbitrary"), 48),
        name="lat_df",
    )(proj3, proj3, proj3, cache_kp, cache_v, cos_t, sin_t, lam_p, subln_g.reshape(1, DF_V))


def _sg_kernel(u_ref, v_ref, g_ref, b_ref, ws_ref, bs_ref, o_ref, *, nch):
    for c in range(nch):
        rs = slice(c * SG_CHUNK, (c + 1) * SG_CHUNK)
        vn = _layernorm(v_ref[0, rs, :], g_ref[...], b_ref[...]).astype(BF16)
        for g in range(SG_GROUPS):
            cs = slice(g * SG_DIM, (g + 1) * SG_DIM)
            mixed = _dot(ws_ref[g], vn[:, cs]) + bs_ref[g]
            o_ref[0, rs, cs] = (u_ref[0, rs, cs] * mixed).astype(BF16)


def _spatial_gate(proj3, ln_g, ln_b, ws, bs):
    b, t, _ = proj3.shape
    nch = min(4, t // SG_CHUNK)
    tt = nch * SG_CHUNK
    ub, vb = OFF_U // SG_WIDTH, OFF_V // SG_WIDTH
    return pl.pallas_call(
        functools.partial(_sg_kernel, nch=nch),
        grid=(b, t // tt),
        in_specs=[pl.BlockSpec((1, tt, SG_WIDTH), lambda i, j: (i, j, ub)),
                  pl.BlockSpec((1, tt, SG_WIDTH), lambda i, j: (i, j, vb)),
                  pl.BlockSpec((1, SG_WIDTH), lambda i, j: (0, 0)),
                  pl.BlockSpec((1, SG_WIDTH), lambda i, j: (0, 0)),
                  pl.BlockSpec((SG_GROUPS, SG_CHUNK, SG_CHUNK), lambda i, j: (0, 0, 0)),
                  pl.BlockSpec((SG_GROUPS, SG_CHUNK, 1), lambda i, j: (0, 0, 0))],
        out_specs=pl.BlockSpec((1, tt, SG_WIDTH), lambda i, j: (i, j, 0)),
        out_shape=jax.ShapeDtypeStruct((b, t, SG_WIDTH), BF16),
        compiler_params=_cp(("parallel", "parallel"), 16),
        name="spatial_gate",
    )(proj3, proj3, ln_g.reshape(1, SG_WIDTH), ln_b.reshape(1, SG_WIDTH), ws, bs)


def _cache_out_kernel(ka_ref, va_ref, dk_ref, dv_ref, *refs):
    ok, ov, ok1, ok2, odv = refs[5:]
    half = DF_QK // 2
    for h in range(NA_HEADS):
        ok[0, 0, h] = ka_ref[0, :, h * NA_DIM:(h + 1) * NA_DIM]
        ov[0, 0, h] = va_ref[0, :, h * NA_DIM:(h + 1) * NA_DIM]
    for h in range(DF_HEADS):
        kk = dk_ref[0, :, h * LANES:(h + 1) * LANES]
        ok1[0, 0, h] = jnp.concatenate([kk[:, 0:half], kk[:, 2 * half:3 * half]], -1)
        ok2[0, 0, h] = jnp.concatenate([kk[:, half:2 * half], kk[:, 3 * half:]], -1)
        odv[0, 0, h] = dv_ref[0, :, h * DF_V:(h + 1) * DF_V]


def _cache_out(proj3, caches, layer):
    b, t, _ = proj3.shape
    any_spec = pl.BlockSpec(memory_space=pl.ANY)

    def head_spec(c):
        return pl.BlockSpec((1, 1) + c.shape[2:], lambda i: (i, layer, 0, 0, 0))

    return pl.pallas_call(
        _cache_out_kernel,
        grid=(b,),
        in_specs=[pl.BlockSpec((1, t, NA_WIDTH), lambda i: (i, 0, OFF_KA // NA_WIDTH)),
                  pl.BlockSpec((1, t, NA_WIDTH), lambda i: (i, 0, OFF_VA // NA_WIDTH)),
                  pl.BlockSpec((1, t, DF_WIDTH), lambda i: (i, 0, OFF_DK // DF_WIDTH)),
                  pl.BlockSpec((1, t, DF_WIDTH), lambda i: (i, 0, OFF_DV // DF_WIDTH))] + [any_spec] * 5,
        out_specs=[head_spec(c) for c in caches],
        out_shape=[jax.ShapeDtypeStruct(c.shape, c.dtype) for c in caches],
        input_output_aliases={4 + k: k for k in range(5)},
        compiler_params=_cp(("parallel",), 32),
        name="cache_out",
    )(proj3, proj3, proj3, proj3, *caches)


def _out_proj_kernel(oa_ref, ob_ref, oc_ref, x_ref, mod_ref, w_ref, lg_ref, lb_ref, x1_ref, h2t_ref):
    y = (_dot(oa_ref[...], w_ref[0:NA_WIDTH, :])
         + _dot(ob_ref[...], w_ref[NA_WIDTH:NA_WIDTH + SG_WIDTH, :])
         + _dot(oc_ref[...], w_ref[NA_WIDTH + SG_WIDTH:, :]))
    g1 = mod_ref[0, 2:3, :]
    sh2 = mod_ref[0, 3:4, :]
    sc2 = mod_ref[0, 4:5, :]
    x1 = _layernorm(DN_ALPHA * x_ref[...] + g1 * y, lg_ref[...], lb_ref[...])
    x1_ref[...] = x1
    h2t_ref[...] = (x1 * (1 + sc2) + sh2).T.astype(BF16)


def _out_proj(oa, ob, oc, x, mod, w, ln_g, ln_b, tiles_per_mod):
    n, d = x.shape
    return pl.pallas_call(
        _out_proj_kernel,
        grid=(n // TM_OUT,),
        in_specs=[pl.BlockSpec((TM_OUT, NA_WIDTH), lambda i: (i, 0)),
                  pl.BlockSpec((TM_OUT, SG_WIDTH), lambda i: (i, 0)),
                  pl.BlockSpec((TM_OUT, DF_WIDTH), lambda i: (i, 0)),
                  pl.BlockSpec((TM_OUT, d), lambda i: (i, 0)),
                  pl.BlockSpec((1, 6, d), lambda i: (i // tiles_per_mod, 0, 0)),
                  pl.BlockSpec((MIX_WIDTH, d), lambda i: (0, 0)),
                  pl.BlockSpec((1, d), lambda i: (0, 0)),
                  pl.BlockSpec((1, d), lambda i: (0, 0))],
        out_specs=[pl.BlockSpec((TM_OUT, d), lambda i: (i, 0)),
                   pl.BlockSpec((d, TM_OUT), lambda i: (0, i))],
        out_shape=[jax.ShapeDtypeStruct((n, d), F32),
                   jax.ShapeDtypeStruct((d, n), BF16)],
        compiler_params=_cp(("parallel",), 48),
        name="out_proj",
    )(oa, ob, oc, x, mod, w, ln_g.reshape(1, d), ln_b.reshape(1, d))


_PK_PAIRS = [(a, b) for a in range(PK_TOPK) for b in range(PK_TOPK) if (a + 1) * (b + 1) <= PK_TOPK]


def _merge_exchange_network(n):
    t = int(math.ceil(math.log2(n)))
    p = 2 ** (t - 1)
    pairs = []
    while p > 0:
        q, r, d = 2 ** (t - 1), 0, p
        while d > 0:
            pairs.extend((i, i + d) for i in range(n - d) if i & p == r)
            d, q, r = q - p, q // 2, p
        p //= 2
    return pairs


def _bitonic_merge_network(n):
    pairs, d = [], n // 2
    while d >= 1:
        pairs.extend((i, i + d) for i in range(n) if i & d == 0)
        d //= 2
    return pairs


_SORT16 = _merge_exchange_network(PK_TOPK)
_MERGE16 = _bitonic_merge_network(PK_TOPK)


def _compare_exchange(xs, network):
    xs = list(xs)
    for i, j in network:
        xs[i], xs[j] = jnp.maximum(xs[i], xs[j]), jnp.minimum(xs[i], xs[j])
    return xs


def _peer_score_kernel(h2t_ref, wqt_ref, keys_ref, rank2_out, e2_out, cnt_out, q1_out,
                       qt_ref, s_ref, tt_ref, thr_ref, z_ref):
    tm = h2t_ref.shape[1]
    qt_ref[...] = _dot(wqt_ref[...], h2t_ref[...]).astype(BF16)
    neg_inf = -jnp.inf
    sub = SUBLANES

    for c in range(2 * PK_HEADS):
        s = _dot(keys_ref[c], qt_ref[c * PK_NKEYS:(c + 1) * PK_NKEYS, :])
        s_ref[c] = s
        head, half = c // 2, c % 2
        xs = _compare_exchange([s[r * sub:(r + 1) * sub, :] for r in range(PK_TOPK)], _SORT16)
        shift = 1
        while shift < sub:
            ys = [pltpu.roll(xs[PK_TOPK - 1 - r], shift, 0) for r in range(PK_TOPK)]
            xs = _compare_exchange([jnp.maximum(x, y) for x, y in zip(xs, ys)], _MERGE16)
            shift *= 2
        for r in range(PK_TOPK):
            tt_ref[half, r, head:head + 1, :] = xs[r][0:1, :]

    def per_lane_chunk(ch, carry):
        l0 = ch * LANES
        t1 = [tt_ref[0, a, :, pl.ds(l0, LANES)] for a in range(PK_TOPK)]
        t2 = [tt_ref[1, b, :, pl.ds(l0, LANES)] for b in range(PK_TOPK)]
        cands = [t1[a] + t2[b] for a, b in _PK_PAIRS]
        cur = list(cands)
        for _ in range(PK_TOPK - 1):
            m = functools.reduce(jnp.maximum, cur)
            found = jnp.zeros(m.shape, jnp.bool_)
            nxt = []
            for cnd in cur:
                is_m = cnd == m
                nxt.append(jnp.where(jnp.logical_and(is_m, jnp.logical_not(found)), neg_inf, cnd))
                found = jnp.logical_or(found, is_m)
            cur = nxt
        thr = functools.reduce(jnp.maximum, cur)
        e1 = [jnp.exp(t1[a] - t1[0]) for a in range(PK_TOPK)]
        e2 = [jnp.exp(t2[b] - t2[0]) for b in range(PK_TOPK)]
        z = jnp.zeros(thr.shape, F32)
        for (a, b), cnd in zip(_PK_PAIRS, cands):
            z = z + jnp.where(cnd >= thr, e1[a] * e2[b], 0.0)
        thr_ref[:, pl.ds(l0, LANES)] = thr
        z_ref[:, pl.ds(l0, LANES)] = z
        return carry

    for ch in range(tm // LANES):
        per_lane_chunk(ch, 0)

    for h in range(PK_HEADS):
        s1 = s_ref[2 * h]
        s2 = s_ref[2 * h + 1]
        thr = thr_ref[h:h + 1, :]
        cnt = jnp.zeros(s1.shape, F32)
        rank2 = jnp.zeros(s2.shape, F32)
        for b in range(PK_TOPK):
            t2b = tt_ref[1, b, h:h + 1, :]
            cnt = jnp.where(s1 + t2b >= thr, float(b + 1), cnt)
            rank2 = jnp.where(t2b > s2, float(b + 1), rank2)
        cnt_out[h] = cnt
        rank2_out[h] = rank2.astype(BF16)
        q1_out[h] = jnp.exp(s1 - tt_ref[0, 0, h:h + 1, :]) * (0.5 / z_ref[h:h + 1, :])
        e2_out[h] = jnp.exp(s2 - tt_ref[1, 0, h:h + 1, :]).astype(BF16)


def _peer_score(h2t, wqt, keys):
    d, n = h2t.shape
    spec = pl.BlockSpec((PK_HEADS, PK_NKEYS, TM_SCORE), lambda i: (0, 0, i))
    shape = jax.ShapeDtypeStruct((PK_HEADS, PK_NKEYS, n), F32)
    shape16 = jax.ShapeDtypeStruct((PK_HEADS, PK_NKEYS, n), BF16)
    return pl.pallas_call(
        _peer_score_kernel,
        grid=(n // TM_SCORE,),
        in_specs=[pl.BlockSpec((d, TM_SCORE), lambda i: (0, i)),
                  pl.BlockSpec(wqt.shape, lambda i: (0, 0)),
                  pl.BlockSpec(keys.shape, lambda i: (0, 0, 0))],
        out_specs=[spec, spec, spec, spec],
        out_shape=[shape16, shape16, shape, shape],
        scratch_shapes=[pltpu.VMEM((PK_HEADS * PK_QDIM, TM_SCORE), BF16),
                        pltpu.VMEM((2 * PK_HEADS, PK_NKEYS, TM_SCORE), F32),
                        pltpu.VMEM((2, PK_TOPK, PK_HEADS, TM_SCORE), F32),
                        pltpu.VMEM((PK_HEADS, TM_SCORE), F32),
                        pltpu.VMEM((PK_HEADS, TM_SCORE), F32)],
        compiler_params=_cp(("parallel",), 48),
        name="peer_score",
    )(h2t, wqt, keys)


def _peer_dense_kernel(h2t_ref, u_ref, vt_ref, rank2_ref, e2_ref, cnt_ref, q1_ref, o_ref,
                       acc_ref, a_ref, w_ref, row_ref):
    sqrt_half = math.sqrt(0.5)

    @pl.when(pl.program_id(1) == 0)
    def _():
        acc_ref[...] = jnp.zeros_like(acc_ref)

    tm = h2t_ref.shape[1]
    zero = jnp.zeros((RJ_PEER, tm), BF16)
    a_ref[...] = _dot(u_ref[...], h2t_ref[...])
    for ii in range(TE_PEER // PK_NKEYS):
        for h in range(PK_HEADS):
            row_ref[0, h] = jnp.broadcast_to(cnt_ref[h, ii:ii + 1, :], (RJ_PEER, tm)).astype(BF16)
            row_ref[1, h] = jnp.broadcast_to(q1_ref[h, ii:ii + 1, :], (RJ_PEER, tm)).astype(BF16)
        for jc in range(PK_NKEYS // RJ_PEER):
            js = slice(jc * RJ_PEER, (jc + 1) * RJ_PEER)
            rs = slice(ii * PK_NKEYS + jc * RJ_PEER, ii * PK_NKEYS + (jc + 1) * RJ_PEER)
            g = None
            for h in range(PK_HEADS):
                term = jnp.where(rank2_ref[h, js, :] < row_ref[0, h], e2_ref[h, js, :], zero) * row_ref[1, h]
                g = term if g is None else g + term
            a = a_ref[rs, :]
            w_ref[rs, :] = g * (a * (1.0 + lax.erf(a * sqrt_half))).astype(BF16)
    acc_ref[...] += _dot(vt_ref[...], w_ref[...])

    @pl.when(pl.program_id(1) == pl.num_programs(1) - 1)
    def _():
        o_ref[...] = acc_ref[...]


def _peer_dense(h2t, u_tabs, vt_tabs, layer, rank2, e2, cnt, q1):
    d, n = h2t.shape
    ne = u_tabs.shape[1]
    ri = TE_PEER // PK_NKEYS
    gspec = pl.BlockSpec((PK_HEADS, PK_NKEYS, TM_PEER), lambda i, e: (0, 0, i))
    rspec = pl.BlockSpec((PK_HEADS, ri, TM_PEER), lambda i, e: (0, e, i))
    return pl.pallas_call(
        _peer_dense_kernel,
        grid=(n // TM_PEER, ne // TE_PEER),
        in_specs=[pl.BlockSpec((d, TM_PEER), lambda i, e: (0, i)),
                  pl.BlockSpec((None, TE_PEER, d), lambda i, e: (layer, e, 0)),
                  pl.BlockSpec((None, d, TE_PEER), lambda i, e: (layer, 0, e)),
                  gspec, gspec, rspec, rspec],
        out_specs=pl.BlockSpec((d, TM_PEER), lambda i, e: (0, i)),
        out_shape=jax.ShapeDtypeStruct((d, n), F32),
        scratch_shapes=[pltpu.VMEM((d, TM_PEER), F32),
                        pltpu.VMEM((TE_PEER, TM_PEER), F32),
                        pltpu.VMEM((TE_PEER, TM_PEER), BF16),
                        pltpu.VMEM((2, PK_HEADS, RJ_PEER, TM_PEER), BF16)],
        compiler_params=_cp(("parallel", "arbitrary"), 56),
        name="peer_dense",
    )(h2t, u_tabs, vt_tabs, rank2, e2, cnt, q1)


def _peer_ln_kernel(yt_ref, x1_ref, mod_ref, lg_ref, lb_ref, o_ref):
    g2 = mod_ref[0, 5:6, :]
    o_ref[...] = _layernorm(DN_ALPHA * x1_ref[...] + g2 * yt_ref[...].T, lg_ref[...], lb_ref[...])


def _peer_ln(yt, x1, mod, ln_g, ln_b, tiles_per_mod):
    n, d = x1.shape
    return pl.pallas_call(
        _peer_ln_kernel,
        grid=(n // TM_OUT,),
        in_specs=[pl.BlockSpec((d, TM_OUT), lambda i: (0, i)),
                  pl.BlockSpec((TM_OUT, d), lambda i: (i, 0)),
                  pl.BlockSpec((1, 6, d), lambda i: (i // tiles_per_mod, 0, 0)),
                  pl.BlockSpec((1, d), lambda i: (0, 0)),
                  pl.BlockSpec((1, d), lambda i: (0, 0))],
        out_specs=pl.BlockSpec((TM_OUT, d), lambda i: (i, 0)),
        out_shape=jax.ShapeDtypeStruct((n, d), F32),
        compiler_params=_cp(("parallel",), 40),
        name="peer_ln",
    )(yt, x1, mod, ln_g.reshape(1, d), ln_b.reshape(1, d))


def _permute_in_proj(w):
    d = w.shape[0]

    def interleave(cols):
        c = cols.reshape(d, 2, DF_HEADS, 2, DF_QK // 2)
        return c.transpose(0, 2, 3, 1, 4).reshape(d, 2 * DF_HEADS * DF_QK)

    return jnp.concatenate([w[:, :OFF_DQ], interleave(w[:, OFF_DQ:OFF_DK]),
                            interleave(w[:, OFF_DK:OFF_DV]), w[:, OFF_DV:]], axis=1).astype(BF16)


def _pair_lanes(a, b):
    half = DF_QK // 2
    return jnp.concatenate([a[..., :half], b[..., :half], a[..., half:], b[..., half:]], -1)


def _rope_tables(t):
    tok = jnp.arange(t)
    row = (tok // GRID_W).astype(F32)
    col = (tok % GRID_W).astype(F32)
    n_freq = DF_QK // 4
    inv = 1.0 / (ROPE_THETA ** (jnp.arange(n_freq, dtype=F32) / n_freq))
    ang = jnp.concatenate([row[:, None] * inv, col[:, None] * inv], -1)
    cos, sin = jnp.cos(ang), jnp.sin(ang)
    return jnp.concatenate([cos] * 4, -1), jnp.concatenate([-sin, -sin, sin, sin], -1)


def _na_bias_table(rpb, rows):
    cases = _na_group_layout(rows)
    cols = np.arange(GRID_W)
    start = np.clip(cols - NA_WIN_C // 2, 0, GRID_W - NA_WIN_C)
    inwin = (cols[None, :] >= start[:, None]) & (cols[None, :] < start[:, None] + NA_WIN_C)
    nh, ndr = rpb.shape[0], 2 * NA_WIN_R - 1
    lead = GRID_W - NA_WIN_C
    p = jnp.pad(rpb, ((0, 0), (0, 0), (lead, 2 * GRID_W - lead - (2 * NA_WIN_C - 1))))
    m = jnp.tile(p, (1, 1, GRID_W))[..., :GRID_W * (2 * GRID_W - 1)].reshape(nh, ndr, GRID_W, 2 * GRID_W - 1)
    blocks = jnp.where(inwin[None, None], m[..., GRID_W - 1:], NEG_BIG)
    blocks = jnp.concatenate([blocks, jnp.full((nh, 1, GRID_W, GRID_W), NEG_BIG, F32)], 1)
    idx = np.full((len(cases), NA_QROWS, NA_BAND), ndr, np.int32)
    for c, pat in enumerate(cases):
        for a, (off, cs) in enumerate(pat):
            for kr in range(off, off + NA_WIN_R):
                idx[c, a, kr] = kr - off - cs + NA_WIN_R - 1
    tab = jnp.take(blocks, idx.reshape(-1), axis=1)
    tab = tab.reshape(nh, len(cases), NA_QROWS, NA_BAND, GRID_W, GRID_W).transpose(0, 1, 2, 4, 3, 5)
    return tab.reshape(nh, len(cases), NA_QROWS * GRID_W, NA_BAND * GRID_W)


def _peer_block(oa, ob, oc, x, mod, tiles_out, w_out, ln_g, ln_b, wqt, keys, u_tabs, vt_tabs, layer):
    x1, h2t = _out_proj(oa, ob, oc, x, mod, w_out, ln_g[0], ln_b[0], tiles_out)
    rank2, e2, cnt, q1 = _peer_score(h2t, wqt, keys)
    yt = _peer_dense(h2t, u_tabs, vt_tabs, layer, rank2, e2, cnt, q1)
    return _peer_ln(yt, x1, mod, ln_g[1], ln_b[1], tiles_out)


def kernel(x_prompt, x_sample, cache_na_k, cache_na_v, cache_df_k1, cache_df_k2, cache_df_v, c, c_ctx,
           w_mod, b_mod, w_in, na_rpb, sg_ln_g, sg_ln_b, sg_w, sg_b, df_lambda, df_subln_g, w_out,
           pk_wq, pk_keys, pk_u, pk_v, ln_g, ln_b):
    bp, tp, d = x_prompt.shape
    bs, ts, _ = x_sample.shape
    n_p, n_s = bp * tp, bs * ts

    cond8 = jnp.zeros((8, d), F32).at[0].set(c_ctx).at[1:1 + bs].set(c)
    mods = _modulation(cond8, w_mod, b_mod).reshape(DEPTH, 8, 6, d)

    cos_t, sin_t = _rope_tables(ts)
    cache_dk = _pair_lanes(cache_df_k1, cache_df_k2)
    u_tabs = pk_u.astype(BF16)
    vt_tabs = pk_v.transpose(0, 2, 1).astype(BF16)

    xp = x_prompt.reshape(n_p, d)
    xs = x_sample.reshape(n_s, d)
    caches_out = [jnp.zeros((bp, DEPTH, NA_HEADS, tp, NA_DIM), F32), jnp.zeros((bp, DEPTH, NA_HEADS, tp, NA_DIM), F32),
                  jnp.zeros((bp, DEPTH, DF_HEADS, tp, DF_QK), F32), jnp.zeros((bp, DEPTH, DF_HEADS, tp, DF_QK), F32),
                  jnp.zeros((bp, DEPTH, DF_HEADS, tp, DF_V), F32)]
    for l in range(DEPTH):
        lam_init = 0.8 - 0.6 * math.exp(-0.3 * l)
        w_in_l = _permute_in_proj(w_in[l])
        w_out_l = w_out[l].astype(BF16)
        wqt = pk_wq[l].T.astype(BF16)
        keys = pk_keys[l].reshape(2 * PK_HEADS, PK_NKEYS, PK_QDIM // 2).astype(BF16)
        ws = sg_w[l].astype(BF16)
        bs_col = sg_b[l].reshape(SG_GROUPS, SG_CHUNK, 1)
        bias = _na_bias_table(na_rpb[l], ts // GRID_W)
        mod_p = mods[l, 0:1]
        mod_s = mods[l, 1:1 + bs]

        proj = _in_proj(xp, mod_p, w_in_l, n_p // TM_IN)
        proj3 = proj.reshape(bp, tp, IN_WIDTH)
        oa, oc = _ctx_attn(proj3, df_lambda[l], df_subln_g[l], lam_init)
        ob = _spatial_gate(proj3, sg_ln_g[l], sg_ln_b[l], ws, bs_col)
        caches_out = _cache_out(proj3, caches_out, l)
        xp = _peer_block(oa.reshape(n_p, NA_WIDTH), ob.reshape(n_p, SG_WIDTH), oc.reshape(n_p, DF_WIDTH),
                         xp, mod_p, n_p // TM_OUT, w_out_l, ln_g[l], ln_b[l], wqt, keys, u_tabs, vt_tabs, l)

        proj = _in_proj(xs, mod_s, w_in_l, ts // TM_IN)
        proj3 = proj.reshape(bs, ts, IN_WIDTH)
        oa = _lat_na(proj3, cache_na_k, cache_na_v, bias, l)
        ob = _spatial_gate(proj3, sg_ln_g[l], sg_ln_b[l], ws, bs_col)
        oc = _lat_df(proj3, cache_dk, cache_df_v, cos_t, sin_t, df_lambda[l], df_subln_g[l], lam_init, l)
        xs = _peer_block(oa.reshape(n_s, NA_WIDTH), ob.reshape(n_s, SG_WIDTH), oc.reshape(n_s, DF_WIDTH),
                         xs, mod_s, ts // TM_OUT, w_out_l, ln_g[l], ln_b[l], wqt, keys, u_tabs, vt_tabs, l)

    return (xp.reshape(bp, tp, d), xs.reshape(bs, ts, d), *caches_out)
```

```python
import functools
import math

import numpy as np
import jax
import jax.numpy as jnp
from jax import lax
from jax.experimental import pallas as pl
from jax.experimental.pallas import tpu as pltpu

F32 = jnp.float32
BF16 = jnp.bfloat16

D_MODEL = 2048
DEPTH = 4
GRID_W = 64
NA_HEADS = 8
NA_DIM = 128
NA_WIN_R = 8
NA_WIN_C = 16
SG_GROUPS = 4
SG_DIM = 128
SG_CHUNK = 128
DF_HEADS = 4
DF_QK = 64
DF_V = 128
ROPE_THETA = 10000.0
NA_WIDTH = NA_HEADS * NA_DIM
SG_WIDTH = SG_GROUPS * SG_DIM
DF_WIDTH = DF_HEADS * DF_V
MIX_WIDTH = NA_WIDTH + SG_WIDTH + DF_WIDTH
IN_WIDTH = 3 * NA_WIDTH + 2 * SG_WIDTH + 4 * DF_HEADS * DF_QK + DF_WIDTH
PK_HEADS = 8
PK_QDIM = 256
PK_NKEYS = 128
PK_TOPK = 16
PK_EXPERTS = PK_NKEYS * PK_NKEYS
DN_ALPHA = (2 * DEPTH) ** 0.25
LN_EPS = 1e-5
RMS_EPS = 1e-6

OFF_QA = 0
OFF_KA = NA_WIDTH
OFF_VA = 2 * NA_WIDTH
OFF_U = 3 * NA_WIDTH
OFF_V = OFF_U + SG_WIDTH
OFF_DQ = OFF_V + SG_WIDTH
OFF_DK = OFF_DQ + 2 * DF_HEADS * DF_QK
OFF_DV = OFF_DK + 2 * DF_HEADS * DF_QK

LANES = 128
SUBLANES = 8
MIB = 1024 * 1024

TM_IN = 512
TN_IN = IN_WIDTH // 4
TM_OUT = 512
TM_SCORE = 256
TM_PEER = 512
TE_PEER = 1024
RJ_PEER = 16
TQ_DF = 256
TN_MOD = 1024
NEG_BIG = -1e30


def _cp(sem, vmem_mib):
    return pltpu.CompilerParams(dimension_semantics=sem, vmem_limit_bytes=vmem_mib * MIB)


def _dot(a, b):
    return jnp.dot(a, b, preferred_element_type=F32)


def _dot_nt(a, b):
    return lax.dot_general(a, b, (((1,), (1,)), ((), ())), preferred_element_type=F32)


def _layernorm(z, g, b):
    mu = jnp.mean(z, -1, keepdims=True)
    d = z - mu
    var = jnp.mean(d * d, -1, keepdims=True)
    return d * lax.rsqrt(var + LN_EPS) * g + b


def _diff_probs(s1, s2, lam):
    e1 = jnp.exp(s1 - jnp.max(s1, -1, keepdims=True))
    e2 = jnp.exp(s2 - jnp.max(s2, -1, keepdims=True))
    c1 = 1.0 / jnp.sum(e1, -1, keepdims=True)
    c2 = lam / jnp.sum(e2, -1, keepdims=True)
    return e1 * c1 - e2 * c2


def _softmax_rows(s):
    m = jnp.max(s, -1, keepdims=True)
    e = jnp.exp(s - m)
    return e * (1.0 / jnp.sum(e, -1, keepdims=True))


def _diff_lambda(lam_ref, lam_init):
    lf = lam_ref[...]
    a = jnp.sum(lf[0:1, :] * lf[1:2, :], axis=1, keepdims=True)
    b = jnp.sum(lf[2:3, :] * lf[3:4, :], axis=1, keepdims=True)
    return jnp.exp(a) - jnp.exp(b) + lam_init


def _map1_mask():
    lane = lax.broadcasted_iota(jnp.int32, (1, LANES), 1)
    return (lane // (DF_QK // 2)) % 2 == 0


def _subln(o, g, lam_init):
    return o * lax.rsqrt(jnp.mean(o * o, -1, keepdims=True) + RMS_EPS) * g * (1.0 - lam_init)


def _mod_kernel(c_ref, w_ref, b_ref, o_ref):
    c = c_ref[...]
    a = (c * jax.nn.sigmoid(c)).astype(BF16)
    o_ref[0] = _dot(a, w_ref[0].astype(BF16)) + b_ref[0]


def _modulation(cond8, w_mod, b_mod):
    depth, d, n6 = w_mod.shape
    return pl.pallas_call(
        _mod_kernel,
        grid=(depth, n6 // TN_MOD),
        in_specs=[pl.BlockSpec((8, d), lambda l, j: (0, 0)),
                  pl.BlockSpec((1, d, TN_MOD), lambda l, j: (l, 0, j)),
                  pl.BlockSpec((1, 1, TN_MOD), lambda l, j: (l, 0, j))],
        out_specs=pl.BlockSpec((1, 8, TN_MOD), lambda l, j: (l, 0, j)),
        out_shape=jax.ShapeDtypeStruct((depth, 8, n6), F32),
        compiler_params=_cp(("parallel", "parallel"), 32),
        name="modulation",
    )(cond8, w_mod, b_mod.reshape(depth, 1, n6))


def _in_proj_kernel(x_ref, mod_ref, w_ref, o_ref, h_ref):
    @pl.when(pl.program_id(1) == 0)
    def _():
        sh = mod_ref[0, 0:1, :]
        sc = mod_ref[0, 1:2, :]
        h_ref[...] = (x_ref[...] * (1 + sc) + sh).astype(BF16)

    o_ref[...] = _dot(h_ref[...], w_ref[...])


def _in_proj(x, mod, w, tiles_per_mod):
    n, d = x.shape
    nw = w.shape[1]
    return pl.pallas_call(
        _in_proj_kernel,
        grid=(n // TM_IN, nw // TN_IN),
        in_specs=[pl.BlockSpec((TM_IN, d), lambda i, j: (i, 0)),
                  pl.BlockSpec((1, 6, d), lambda i, j: (i // tiles_per_mod, 0, 0)),
                  pl.BlockSpec((d, TN_IN), lambda i, j: (0, j))],
        out_specs=pl.BlockSpec((TM_IN, TN_IN), lambda i, j: (i, j)),
        out_shape=jax.ShapeDtypeStruct((n, nw), F32),
        scratch_shapes=[pltpu.VMEM((TM_IN, d), BF16)],
        compiler_params=_cp(("parallel", "arbitrary"), 40),
        name="in_proj",
    )(x, mod, w)


def _ctx_attn_kernel(p_ref, lam_ref, g_ref, oa_ref, oc_ref, *, lam_init):
    scale = NA_DIM ** -0.5
    for h in range(NA_HEADS):
        sl = slice(h * NA_DIM, (h + 1) * NA_DIM)
        q = p_ref[0, :, OFF_QA + h * NA_DIM:OFF_QA + (h + 1) * NA_DIM].astype(BF16)
        k = p_ref[0, :, OFF_KA + h * NA_DIM:OFF_KA + (h + 1) * NA_DIM].astype(BF16)
        v = p_ref[0, :, OFF_VA + h * NA_DIM:OFF_VA + (h + 1) * NA_DIM].astype(BF16)
        p = _softmax_rows(_dot_nt(q, k) * scale)
        oa_ref[0, :, sl] = _dot(p.astype(BF16), v).astype(BF16)
    lam = _diff_lambda(lam_ref, lam_init)
    m1 = _map1_mask()
    dscale = DF_QK ** -0.5
    for h in range(DF_HEADS):
        sl = slice(h * DF_V, (h + 1) * DF_V)
        q = p_ref[0, :, OFF_DQ + h * LANES:OFF_DQ + (h + 1) * LANES] * dscale
        k = p_ref[0, :, OFF_DK + h * LANES:OFF_DK + (h + 1) * LANES].astype(BF16)
        v = p_ref[0, :, OFF_DV + h * DF_V:OFF_DV + (h + 1) * DF_V].astype(BF16)
        qa = jnp.where(m1, q, 0.0).astype(BF16)
        qb = jnp.where(m1, 0.0, q).astype(BF16)
        p = _diff_probs(_dot_nt(qa, k), _dot_nt(qb, k), lam)
        o = _dot(p.astype(BF16), v)
        oc_ref[0, :, sl] = _subln(o, g_ref[...], lam_init).astype(BF16)


def _ctx_attn(proj3, lam_p, subln_g, lam_init):
    b, t, nw = proj3.shape
    return pl.pallas_call(
        functools.partial(_ctx_attn_kernel, lam_init=lam_init),
        grid=(b,),
        in_specs=[pl.BlockSpec((1, t, nw), lambda i: (i, 0, 0)),
                  pl.BlockSpec((4, DF_QK), lambda i: (0, 0)),
                  pl.BlockSpec((1, DF_V), lambda i: (0, 0))],
        out_specs=[pl.BlockSpec((1, t, NA_WIDTH), lambda i: (i, 0, 0)),
                   pl.BlockSpec((1, t, DF_WIDTH), lambda i: (i, 0, 0))],
        out_shape=[jax.ShapeDtypeStruct((b, t, NA_WIDTH), BF16),
                   jax.ShapeDtypeStruct((b, t, DF_WIDTH), BF16)],
        compiler_params=_cp(("parallel",), 32),
        name="ctx_attn",
    )(proj3, lam_p, subln_g.reshape(1, DF_V))


NA_QROWS = 4
NA_BAND = NA_WIN_R + NA_QROWS


def _na_group_layout(rows):
    groups = []
    for g in range(rows // NA_QROWS):
        bs = int(np.clip(g * NA_QROWS - NA_WIN_R // 2, 0, rows - NA_BAND))
        pat = []
        for a in range(NA_QROWS):
            r = g * NA_QROWS + a
            r0 = int(np.clip(r - NA_WIN_R // 2, 0, rows - NA_WIN_R))
            pat.append((r0 - bs, r - r0))
        groups.append((bs, tuple(pat)))
    cases = [groups[0][1], groups[1][1], groups[-1][1]]
    assert all(p == cases[1] for _, p in groups[1:-1]) and rows >= 2 * NA_BAND
    assert all(0 <= off and off + NA_WIN_R <= NA_BAND for c in cases for off, _ in c)
    return cases


def _lat_na_kernel(q_ref, k_ref, v_ref, kc_ref, vc_ref, bias_ref, o_ref, kb_ref, vb_ref, *, rows):
    kb_ref[...] = k_ref[0].astype(BF16)
    vb_ref[...] = v_ref[0].astype(BF16)
    kc = kc_ref[0, 0, 0].astype(BF16)
    vc = vc_ref[0, 0, 0].astype(BF16)
    scale = NA_DIM ** -0.5
    nq = NA_QROWS * GRID_W
    band = NA_BAND * GRID_W
    ngroups = rows // NA_QROWS

    def body(g, carry):
        bs = jnp.clip(g * NA_QROWS - NA_WIN_R // 2, 0, rows - NA_BAND)
        case = jnp.where(g == 0, 0, jnp.where(g == ngroups - 1, 2, 1))
        q0 = pl.multiple_of(g * nq, nq)
        q = q_ref[0, pl.ds(q0, nq), :].astype(BF16)
        k0 = pl.multiple_of(bs * GRID_W, GRID_W)
        kw = kb_ref[pl.ds(k0, band), :]
        vw = vb_ref[pl.ds(k0, band), :]
        s_loc = _dot_nt(q, kw) * scale + bias_ref[0, case]
        s_ctx = _dot_nt(q, kc) * scale
        m = jnp.maximum(jnp.max(s_loc, -1, keepdims=True), jnp.max(s_ctx, -1, keepdims=True))
        e_loc = jnp.exp(s_loc - m)
        e_ctx = jnp.exp(s_ctx - m)
        inv = 1.0 / (jnp.sum(e_loc, -1, keepdims=True) + jnp.sum(e_ctx, -1, keepdims=True))
        o = _dot((e_loc * inv).astype(BF16), vw) + _dot((e_ctx * inv).astype(BF16), vc)
        o_ref[0, pl.ds(q0, nq), :] = o.astype(BF16)
        return carry

    lax.fori_loop(0, ngroups, body, 0)


def _lat_na(proj3, cache_k, cache_v, bias, layer):
    b, t, _ = proj3.shape
    past = cache_k.shape[3]
    rows = t // GRID_W
    qb, kb, vb = OFF_QA // NA_DIM, OFF_KA // NA_DIM, OFF_VA // NA_DIM
    return pl.pallas_call(
        functools.partial(_lat_na_kernel, rows=rows),
        grid=(b, NA_HEADS),
        in_specs=[pl.BlockSpec((1, t, NA_DIM), lambda i, h: (i, 0, qb + h)),
                  pl.BlockSpec((1, t, NA_DIM), lambda i, h: (i, 0, kb + h)),
                  pl.BlockSpec((1, t, NA_DIM), lambda i, h: (i, 0, vb + h)),
                  pl.BlockSpec((1, 1, 1, past, NA_DIM), lambda i, h: (i, layer, h, 0, 0)),
                  pl.BlockSpec((1, 1, 1, past, NA_DIM), lambda i, h: (i, layer, h, 0, 0)),
                  pl.BlockSpec((1, 3, NA_QROWS * GRID_W, NA_BAND * GRID_W), lambda i, h: (h, 0, 0, 0))],
        out_specs=pl.BlockSpec((1, t, NA_DIM), lambda i, h: (i, 0, h)),
        out_shape=jax.ShapeDtypeStruct((b, t, NA_WIDTH), BF16),
        scratch_shapes=[pltpu.VMEM((t, NA_DIM), BF16), pltpu.VMEM((t, NA_DIM), BF16)],
        compiler_params=_cp(("parallel", "parallel"), 40),
        name="lat_na",
    )(proj3, proj3, proj3, cache_k, cache_v, bias)


def _lat_df_kernel(q_ref, k_ref, v_ref, kc_ref, vc_ref, cos_ref, sin_ref, lam_ref, g_ref, o_ref,
                   kall_ref, vall_ref, *, t, lam_init):
    qi = pl.program_id(2)

    @pl.when(qi == 0)
    def _():
        k = k_ref[0]
        kall_ref[0:t, :] = (k * cos_ref[...] + pltpu.roll(k, LANES // 2, 1) * sin_ref[...]).astype(BF16)
        kall_ref[t:, :] = kc_ref[0, 0, 0].astype(BF16)
        vall_ref[0:t, :] = v_ref[0].astype(BF16)
        vall_ref[t:, :] = vc_ref[0, 0, 0].astype(BF16)

    t0 = pl.multiple_of(qi * TQ_DF, TQ_DF)
    q = q_ref[0]
    q = q * cos_ref[pl.ds(t0, TQ_DF), :] + pltpu.roll(q, LANES // 2, 1) * sin_ref[pl.ds(t0, TQ_DF), :]
    q = q * DF_QK ** -0.5
    m1 = _map1_mask()
    qa = jnp.where(m1, q, 0.0).astype(BF16)
    qb = jnp.where(m1, 0.0, q).astype(BF16)
    lam = _diff_lambda(lam_ref, lam_init)
    kall = kall_ref[...]
    s1 = _dot_nt(qa, kall)
    s2 = _dot_nt(qb, kall)
    e1 = jnp.exp(s1 - jnp.max(s1, -1, keepdims=True))
    e2 = jnp.exp(s2 - jnp.max(s2, -1, keepdims=True))
    c1 = 1.0 / jnp.sum(e1, -1, keepdims=True)
    c2 = lam / jnp.sum(e2, -1, keepdims=True)
    vall = vall_ref[...]
    o = _dot(e1.astype(BF16), vall) * c1 - _dot(e2.astype(BF16), vall) * c2
    o_ref[0] = _subln(o, g_ref[...], lam_init).astype(BF16)


def _lat_df(proj3, cache_kp, cache_v, cos_t, sin_t, lam_p, subln_g, lam_init, layer):
    b, t, _ = proj3.shape
    past = cache_v.shape[3]
    qb, kb, vb = OFF_DQ // LANES, OFF_DK // LANES, OFF_DV // LANES
    return pl.pallas_call(
        functools.partial(_lat_df_kernel, t=t, lam_init=lam_init),
        grid=(b, DF_HEADS, t // TQ_DF),
        in_specs=[pl.BlockSpec((1, TQ_DF, LANES), lambda i, h, j: (i, j, qb + h)),
                  pl.BlockSpec((1, t, LANES), lambda i, h, j: (i, 0, kb + h)),
                  pl.BlockSpec((1, t, LANES), lambda i, h, j: (i, 0, vb + h)),
                  pl.BlockSpec((1, 1, 1, past, LANES), lambda i, h, j: (i, layer, h, 0, 0)),
                  pl.BlockSpec((1, 1, 1, past, DF_V), lambda i, h, j: (i, layer, h, 0, 0)),
                  pl.BlockSpec((t, LANES), lambda i, h, j: (0, 0)),
                  pl.BlockSpec((t, LANES), lambda i, h, j: (0, 0)),
                  pl.BlockSpec((4, DF_QK), lambda i, h, j: (0, 0)),
                  pl.BlockSpec((1, DF_V), lambda i, h, j: (0, 0))],
        out_specs=pl.BlockSpec((1, TQ_DF, DF_V), lambda i, h, j: (i, j, h)),
        out_shape=jax.ShapeDtypeStruct((b, t, DF_WIDTH), BF16),
        scratch_shapes=[pltpu.VMEM((t + past, LANES), BF16), pltpu.VMEM((t + past, DF_V), BF16)],
        compiler_params=_cp(("parallel", "parallel", "arbitrary"), 48),
        name="lat_df",
    )(proj3, proj3, proj3, cache_kp, cache_v, cos_t, sin_t, lam_p, subln_g.reshape(1, DF_V))


def _sg_kernel(u_ref, v_ref, g_ref, b_ref, ws_ref, bs_ref, o_ref, *, nch):
    for c in range(nch):
        rs = slice(c * SG_CHUNK, (c + 1) * SG_CHUNK)
        vn = _layernorm(v_ref[0, rs, :], g_ref[...], b_ref[...]).astype(BF16)
        for g in range(SG_GROUPS):
            cs = slice(g * SG_DIM, (g + 1) * SG_DIM)
            mixed = _dot(ws_ref[g], vn[:, cs]) + bs_ref[g]
            o_ref[0, rs, cs] = (u_ref[0, rs, cs] * mixed).astype(BF16)


def _spatial_gate(proj3, ln_g, ln_b, ws, bs):
    b, t, _ = proj3.shape
    nch = min(4, t // SG_CHUNK)
    tt = nch * SG_CHUNK
    ub, vb = OFF_U // SG_WIDTH, OFF_V // SG_WIDTH
    return pl.pallas_call(
        functools.partial(_sg_kernel, nch=nch),
        grid=(b, t // tt),
        in_specs=[pl.BlockSpec((1, tt, SG_WIDTH), lambda i, j: (i, j, ub)),
                  pl.BlockSpec((1, tt, SG_WIDTH), lambda i, j: (i, j, vb)),
                  pl.BlockSpec((1, SG_WIDTH), lambda i, j: (0, 0)),
                  pl.BlockSpec((1, SG_WIDTH), lambda i, j: (0, 0)),
                  pl.BlockSpec((SG_GROUPS, SG_CHUNK, SG_CHUNK), lambda i, j: (0, 0, 0)),
                  pl.BlockSpec((SG_GROUPS, SG_CHUNK, 1), lambda i, j: (0, 0, 0))],
        out_specs=pl.BlockSpec((1, tt, SG_WIDTH), lambda i, j: (i, j, 0)),
        out_shape=jax.ShapeDtypeStruct((b, t, SG_WIDTH), BF16),
        compiler_params=_cp(("parallel", "parallel"), 16),
        name="spatial_gate",
    )(proj3, proj3, ln_g.reshape(1, SG_WIDTH), ln_b.reshape(1, SG_WIDTH), ws, bs)


def _cache_out_kernel(ka_ref, va_ref, dk_ref, dv_ref, *refs):
    ok, ov, ok1, ok2, odv = refs[5:]
    half = DF_QK // 2
    for h in range(NA_HEADS):
        ok[0, 0, h] = ka_ref[0, :, h * NA_DIM:(h + 1) * NA_DIM]
        ov[0, 0, h] = va_ref[0, :, h * NA_DIM:(h + 1) * NA_DIM]
    for h in range(DF_HEADS):
        kk = dk_ref[0, :, h * LANES:(h + 1) * LANES]
        ok1[0, 0, h] = jnp.concatenate([kk[:, 0:half], kk[:, 2 * half:3 * half]], -1)
        ok2[0, 0, h] = jnp.concatenate([kk[:, half:2 * half], kk[:, 3 * half:]], -1)
        odv[0, 0, h] = dv_ref[0, :, h * DF_V:(h + 1) * DF_V]


def _cache_out(proj3, caches, layer):
    b, t, _ = proj3.shape
    any_spec = pl.BlockSpec(memory_space=pl.ANY)

    def head_spec(c):
        return pl.BlockSpec((1, 1) + c.shape[2:], lambda i: (i, layer, 0, 0, 0))

    return pl.pallas_call(
        _cache_out_kernel,
        grid=(b,),
        in_specs=[pl.BlockSpec((1, t, NA_WIDTH), lambda i: (i, 0, OFF_KA // NA_WIDTH)),
                  pl.BlockSpec((1, t, NA_WIDTH), lambda i: (i, 0, OFF_VA // NA_WIDTH)),
                  pl.BlockSpec((1, t, DF_WIDTH), lambda i: (i, 0, OFF_DK // DF_WIDTH)),
                  pl.BlockSpec((1, t, DF_WIDTH), lambda i: (i, 0, OFF_DV // DF_WIDTH))] + [any_spec] * 5,
        out_specs=[head_spec(c) for c in caches],
        out_shape=[jax.ShapeDtypeStruct(c.shape, c.dtype) for c in caches],
        input_output_aliases={4 + k: k for k in range(5)},
        compiler_params=_cp(("parallel",), 32),
        name="cache_out",
    )(proj3, proj3, proj3, proj3, *caches)


def _out_proj_kernel(oa_ref, ob_ref, oc_ref, x_ref, mod_ref, w_ref, lg_ref, lb_ref, x1_ref, h2t_ref):
    y = (_dot(oa_ref[...], w_ref[0:NA_WIDTH, :])
         + _dot(ob_ref[...], w_ref[NA_WIDTH:NA_WIDTH + SG_WIDTH, :])
         + _dot(oc_ref[...], w_ref[NA_WIDTH + SG_WIDTH:, :]))
    g1 = mod_ref[0, 2:3, :]
    sh2 = mod_ref[0, 3:4, :]
    sc2 = mod_ref[0, 4:5, :]
    x1 = _layernorm(DN_ALPHA * x_ref[...] + g1 * y, lg_ref[...], lb_ref[...])
    x1_ref[...] = x1
    h2t_ref[...] = (x1 * (1 + sc2) + sh2).T.astype(BF16)


def _out_proj(oa, ob, oc, x, mod, w, ln_g, ln_b, tiles_per_mod):
    n, d = x.shape
    return pl.pallas_call(
        _out_proj_kernel,
        grid=(n // TM_OUT,),
        in_specs=[pl.BlockSpec((TM_OUT, NA_WIDTH), lambda i: (i, 0)),
                  pl.BlockSpec((TM_OUT, SG_WIDTH), lambda i: (i, 0)),
                  pl.BlockSpec((TM_OUT, DF_WIDTH), lambda i: (i, 0)),
                  pl.BlockSpec((TM_OUT, d), lambda i: (i, 0)),
                  pl.BlockSpec((1, 6, d), lambda i: (i // tiles_per_mod, 0, 0)),
                  pl.BlockSpec((MIX_WIDTH, d), lambda i: (0, 0)),
                  pl.BlockSpec((1, d), lambda i: (0, 0)),
                  pl.BlockSpec((1, d), lambda i: (0, 0))],
        out_specs=[pl.BlockSpec((TM_OUT, d), lambda i: (i, 0)),
                   pl.BlockSpec((d, TM_OUT), lambda i: (0, i))],
        out_shape=[jax.ShapeDtypeStruct((n, d), F32),
                   jax.ShapeDtypeStruct((d, n), BF16)],
        compiler_params=_cp(("parallel",), 48),
        name="out_proj",
    )(oa, ob, oc, x, mod, w, ln_g.reshape(1, d), ln_b.reshape(1, d))


_PK_PAIRS = [(a, b) for a in range(PK_TOPK) for b in range(PK_TOPK) if (a + 1) * (b + 1) <= PK_TOPK]


def _merge_exchange_network(n):
    t = int(math.ceil(math.log2(n)))
    p = 2 ** (t - 1)
    pairs = []
    while p > 0:
        q, r, d = 2 ** (t - 1), 0, p
        while d > 0:
            pairs.extend((i, i + d) for i in range(n - d) if i & p == r)
            d, q, r = q - p, q // 2, p
        p //= 2
    return pairs


def _bitonic_merge_network(n):
    pairs, d = [], n // 2
    while d >= 1:
        pairs.extend((i, i + d) for i in range(n) if i & d == 0)
        d //= 2
    return pairs


_SORT16 = _merge_exchange_network(PK_TOPK)
_MERGE16 = _bitonic_merge_network(PK_TOPK)


def _compare_exchange(xs, network):
    xs = list(xs)
    for i, j in network:
        xs[i], xs[j] = jnp.maximum(xs[i], xs[j]), jnp.minimum(xs[i], xs[j])
    return xs


def _peer_score_kernel(h2t_ref, wqt_ref, keys_ref, rank2_out, e2_out, cnt_out, q1_out,
                       qt_ref, s_ref, tt_ref, thr_ref, z_ref):
    tm = h2t_ref.shape[1]
    qt_ref[...] = _dot(wqt_ref[...], h2t_ref[...]).astype(BF16)
    neg_inf = -jnp.inf
    sub = SUBLANES

    for c in range(2 * PK_HEADS):
        s = _dot(keys_ref[c], qt_ref[c * PK_NKEYS:(c + 1) * PK_NKEYS, :])
        s_ref[c] = s
        head, half = c // 2, c % 2
        xs = _compare_exchange([s[r * sub:(r + 1) * sub, :] for r in range(PK_TOPK)], _SORT16)
        shift = 1
        while shift < sub:
            ys = [pltpu.roll(xs[PK_TOPK - 1 - r], shift, 0) for r in range(PK_TOPK)]
            xs = _compare_exchange([jnp.maximum(x, y) for x, y in zip(xs, ys)], _MERGE16)
            shift *= 2
        for r in range(PK_TOPK):
            tt_ref[half, r, head:head + 1, :] = xs[r][0:1, :]

    def per_lane_chunk(ch, carry):
        l0 = ch * LANES
        t1 = [tt_ref[0, a, :, pl.ds(l0, LANES)] for a in range(PK_TOPK)]
        t2 = [tt_ref[1, b, :, pl.ds(l0, LANES)] for b in range(PK_TOPK)]
        cands = [t1[a] + t2[b] for a, b in _PK_PAIRS]
        cur = list(cands)
        for _ in range(PK_TOPK - 1):
            m = functools.reduce(jnp.maximum, cur)
            found = jnp.zeros(m.shape, jnp.bool_)
            nxt = []
            for cnd in cur:
                is_m = cnd == m
                nxt.append(jnp.where(jnp.logical_and(is_m, jnp.logical_not(found)), neg_inf, cnd))
                found = jnp.logical_or(found, is_m)
            cur = nxt
        thr = functools.reduce(jnp.maximum, cur)
        e1 = [jnp.exp(t1[a] - t1[0]) for a in range(PK_TOPK)]
        e2 = [jnp.exp(t2[b] - t2[0]) for b in range(PK_TOPK)]
        z = jnp.zeros(thr.shape, F32)
        for (a, b), cnd in zip(_PK_PAIRS, cands):
            z = z + jnp.where(cnd >= thr, e1[a] * e2[b], 0.0)
        thr_ref[:, pl.ds(l0, LANES)] = thr
        z_ref[:, pl.ds(l0, LANES)] = z
        return carry

    for ch in range(tm // LANES):
        per_lane_chunk(ch, 0)

    for h in range(PK_HEADS):
        s1 = s_ref[2 * h]
        s2 = s_ref[2 * h + 1]
        thr = thr_ref[h:h + 1, :]
        cnt = jnp.zeros(s1.shape, F32)
        rank2 = jnp.zeros(s2.shape, F32)
        for b in range(PK_TOPK):
            t2b = tt_ref[1, b, h:h + 1, :]
            cnt = jnp.where(s1 + t2b >= thr, float(b + 1), cnt)
            rank2 = jnp.where(t2b > s2, float(b + 1), rank2)
        cnt_out[h] = cnt
        rank2_out[h] = rank2.astype(BF16)
        q1_out[h] = jnp.exp(s1 - tt_ref[0, 0, h:h + 1, :]) * (0.5 / z_ref[h:h + 1, :])
        e2_out[h] = jnp.exp(s2 - tt_ref[1, 0, h:h + 1, :]).astype(BF16)


def _peer_score(h2t, wqt, keys):
    d, n = h2t.shape
    spec = pl.BlockSpec((PK_HEADS, PK_NKEYS, TM_SCORE), lambda i: (0, 0, i))
    shape = jax.ShapeDtypeStruct((PK_HEADS, PK_NKEYS, n), F32)
    shape16 = jax.ShapeDtypeStruct((PK_HEADS, PK_NKEYS, n), BF16)
    return pl.pallas_call(
        _peer_score_kernel,
        grid=(n // TM_SCORE,),
        in_specs=[pl.BlockSpec((d, TM_SCORE), lambda i: (0, i)),
                  pl.BlockSpec(wqt.shape, lambda i: (0, 0)),
                  pl.BlockSpec(keys.shape, lambda i: (0, 0, 0))],
        out_specs=[spec, spec, spec, spec],
        out_shape=[shape16, shape16, shape, shape],
        scratch_shapes=[pltpu.VMEM((PK_HEADS * PK_QDIM, TM_SCORE), BF16),
                        pltpu.VMEM((2 * PK_HEADS, PK_NKEYS, TM_SCORE), F32),
                        pltpu.VMEM((2, PK_TOPK, PK_HEADS, TM_SCORE), F32),
                        pltpu.VMEM((PK_HEADS, TM_SCORE), F32),
                        pltpu.VMEM((PK_HEADS, TM_SCORE), F32)],
        compiler_params=_cp(("parallel",), 48),
        name="peer_score",
    )(h2t, wqt, keys)


def _peer_dense_kernel(h2t_ref, u_ref, vt_ref, rank2_ref, e2_ref, cnt_ref, q1_ref, o_ref,
                       acc_ref, a_ref, w_ref, row_ref):
    sqrt_half = math.sqrt(0.5)

    @pl.when(pl.program_id(1) == 0)
    def _():
        acc_ref[...] = jnp.zeros_like(acc_ref)

    tm = h2t_ref.shape[1]
    zero = jnp.zeros((RJ_PEER, tm), BF16)
    a_ref[...] = _dot(u_ref[...], h2t_ref[...])
    for ii in range(TE_PEER // PK_NKEYS):
        for h in range(PK_HEADS):
            row_ref[0, h] = jnp.broadcast_to(cnt_ref[h, ii:ii + 1, :], (RJ_PEER, tm)).astype(BF16)
            row_ref[1, h] = jnp.broadcast_to(q1_ref[h, ii:ii + 1, :], (RJ_PEER, tm)).astype(BF16)
        for jc in range(PK_NKEYS // RJ_PEER):
            js = slice(jc * RJ_PEER, (jc + 1) * RJ_PEER)
            rs = slice(ii * PK_NKEYS + jc * RJ_PEER, ii * PK_NKEYS + (jc + 1) * RJ_PEER)
            g = None
            for h in range(PK_HEADS):
                term = jnp.where(rank2_ref[h, js, :] < row_ref[0, h], e2_ref[h, js, :], zero) * row_ref[1, h]
                g = term if g is None else g + term
            a = a_ref[rs, :]
            w_ref[rs, :] = g * (a * (1.0 + lax.erf(a * sqrt_half))).astype(BF16)
    acc_ref[...] += _dot(vt_ref[...], w_ref[...])

    @pl.when(pl.program_id(1) == pl.num_programs(1) - 1)
    def _():
        o_ref[...] = acc_ref[...]


def _peer_dense(h2t, u_tabs, vt_tabs, layer, rank2, e2, cnt, q1):
    d, n = h2t.shape
    ne = u_tabs.shape[1]
    ri = TE_PEER // PK_NKEYS
    gspec = pl.BlockSpec((PK_HEADS, PK_NKEYS, TM_PEER), lambda i, e: (0, 0, i))
    rspec = pl.BlockSpec((PK_HEADS, ri, TM_PEER), lambda i, e: (0, e, i))
    return pl.pallas_call(
        _peer_dense_kernel,
        grid=(n // TM_PEER, ne // TE_PEER),
        in_specs=[pl.BlockSpec((d, TM_PEER), lambda i, e: (0, i)),
                  pl.BlockSpec((None, TE_PEER, d), lambda i, e: (layer, e, 0)),
                  pl.BlockSpec((None, d, TE_PEER), lambda i, e: (layer, 0, e)),
                  gspec, gspec, rspec, rspec],
        out_specs=pl.BlockSpec((d, TM_PEER), lambda i, e: (0, i)),
        out_shape=jax.ShapeDtypeStruct((d, n), F32),
        scratch_shapes=[pltpu.VMEM((d, TM_PEER), F32),
                        pltpu.VMEM((TE_PEER, TM_PEER), F32),
                        pltpu.VMEM((TE_PEER, TM_PEER), BF16),
                        pltpu.VMEM((2, PK_HEADS, RJ_PEER, TM_PEER), BF16)],
        compiler_params=_cp(("parallel", "arbitrary"), 56),
        name="peer_dense",
    )(h2t, u_tabs, vt_tabs, rank2, e2, cnt, q1)


def _peer_ln_kernel(yt_ref, x1_ref, mod_ref, lg_ref, lb_ref, o_ref):
    g2 = mod_ref[0, 5:6, :]
    o_ref[...] = _layernorm(DN_ALPHA * x1_ref[...] + g2 * yt_ref[...].T, lg_ref[...], lb_ref[...])


def _peer_ln(yt, x1, mod, ln_g, ln_b, tiles_per_mod):
    n, d = x1.shape
    return pl.pallas_call(
        _peer_ln_kernel,
        grid=(n // TM_OUT,),
        in_specs=[pl.BlockSpec((d, TM_OUT), lambda i: (0, i)),
                  pl.BlockSpec((TM_OUT, d), lambda i: (i, 0)),
                  pl.BlockSpec((1, 6, d), lambda i: (i // tiles_per_mod, 0, 0)),
                  pl.BlockSpec((1, d), lambda i: (0, 0)),
                  pl.BlockSpec((1, d), lambda i: (0, 0))],
        out_specs=pl.BlockSpec((TM_OUT, d), lambda i: (i, 0)),
        out_shape=jax.ShapeDtypeStruct((n, d), F32),
        compiler_params=_cp(("parallel",), 40),
        name="peer_ln",
    )(yt, x1, mod, ln_g.reshape(1, d), ln_b.reshape(1, d))


def _permute_in_proj(w):
    d = w.shape[0]

    def interleave(cols):
        c = cols.reshape(d, 2, DF_HEADS, 2, DF_QK // 2)
        return c.transpose(0, 2, 3, 1, 4).reshape(d, 2 * DF_HEADS * DF_QK)

    return jnp.concatenate([w[:, :OFF_DQ], interleave(w[:, OFF_DQ:OFF_DK]),
                            interleave(w[:, OFF_DK:OFF_DV]), w[:, OFF_DV:]], axis=1).astype(BF16)


def _pair_lanes(a, b):
    half = DF_QK // 2
    return jnp.concatenate([a[..., :half], b[..., :half], a[..., half:], b[..., half:]], -1)


def _rope_tables(t):
    tok = jnp.arange(t)
    row = (tok // GRID_W).astype(F32)
    col = (tok % GRID_W).astype(F32)
    n_freq = DF_QK // 4
    inv = 1.0 / (ROPE_THETA ** (jnp.arange(n_freq, dtype=F32) / n_freq))
    ang = jnp.concatenate([row[:, None] * inv, col[:, None] * inv], -1)
    cos, sin = jnp.cos(ang), jnp.sin(ang)
    return jnp.concatenate([cos] * 4, -1), jnp.concatenate([-sin, -sin, sin, sin], -1)


def _na_bias_table(rpb, rows):
    cases = _na_group_layout(rows)
    cols = np.arange(GRID_W)
    start = np.clip(cols - NA_WIN_C // 2, 0, GRID_W - NA_WIN_C)
    inwin = (cols[None, :] >= start[:, None]) & (cols[None, :] < start[:, None] + NA_WIN_C)
    nh, ndr = rpb.shape[0], 2 * NA_WIN_R - 1
    lead = GRID_W - NA_WIN_C
    p = jnp.pad(rpb, ((0, 0), (0, 0), (lead, 2 * GRID_W - lead - (2 * NA_WIN_C - 1))))
    m = jnp.tile(p, (1, 1, GRID_W))[..., :GRID_W * (2 * GRID_W - 1)].reshape(nh, ndr, GRID_W, 2 * GRID_W - 1)
    blocks = jnp.where(inwin[None, None], m[..., GRID_W - 1:], NEG_BIG)
    blocks = jnp.concatenate([blocks, jnp.full((nh, 1, GRID_W, GRID_W), NEG_BIG, F32)], 1)
    idx = np.full((len(cases), NA_QROWS, NA_BAND), ndr, np.int32)
    for c, pat in enumerate(cases):
        for a, (off, cs) in enumerate(pat):
            for kr in range(off, off + NA_WIN_R):
                idx[c, a, kr] = kr - off - cs + NA_WIN_R - 1
    tab = jnp.take(blocks, idx.reshape(-1), axis=1)
    tab = tab.reshape(nh, len(cases), NA_QROWS, NA_BAND, GRID_W, GRID_W).transpose(0, 1, 2, 4, 3, 5)
    return tab.reshape(nh, len(cases), NA_QROWS * GRID_W, NA_BAND * GRID_W)


def _peer_block(oa, ob, oc, x, mod, tiles_out, w_out, ln_g, ln_b, wqt, keys, u_tabs, vt_tabs, layer):
    x1, h2t = _out_proj(oa, ob, oc, x, mod, w_out, ln_g[0], ln_b[0], tiles_out)
    rank2, e2, cnt, q1 = _peer_score(h2t, wqt, keys)
    yt = _peer_dense(h2t, u_tabs, vt_tabs, layer, rank2, e2, cnt, q1)
    return _peer_ln(yt, x1, mod, ln_g[1], ln_b[1], tiles_out)


def kernel(x_prompt, x_sample, cache_na_k, cache_na_v, cache_df_k1, cache_df_k2, cache_df_v, c, c_ctx,
           w_mod, b_mod, w_in, na_rpb, sg_ln_g, sg_ln_b, sg_w, sg_b, df_lambda, df_subln_g, w_out,
           pk_wq, pk_keys, pk_u, pk_v, ln_g, ln_b):
    bp, tp, d = x_prompt.shape
    bs, ts, _ = x_sample.shape
    n_p, n_s = bp * tp, bs * ts

    cond8 = jnp.zeros((8, d), F32).at[0].set(c_ctx).at[1:1 + bs].set(c)
    mods = _modulation(cond8, w_mod, b_mod).reshape(DEPTH, 8, 6, d)

    cos_t, sin_t = _rope_tables(ts)
    cache_dk = _pair_lanes(cache_df_k1, cache_df_k2)
    u_tabs = pk_u.astype(BF16)
    vt_tabs = pk_v.transpose(0, 2, 1).astype(BF16)

    xp = x_prompt.reshape(n_p, d)
    xs = x_sample.reshape(n_s, d)
    caches_out = [jnp.zeros((bp, DEPTH, NA_HEADS, tp, NA_DIM), F32), jnp.zeros((bp, DEPTH, NA_HEADS, tp, NA_DIM), F32),
                  jnp.zeros((bp, DEPTH, DF_HEADS, tp, DF_QK), F32), jnp.zeros((bp, DEPTH, DF_HEADS, tp, DF_QK), F32),
                  jnp.zeros((bp, DEPTH, DF_HEADS, tp, DF_V), F32)]
    for l in range(DEPTH):
        lam_init = 0.8 - 0.6 * math.exp(-0.3 * l)
        w_in_l = _permute_in_proj(w_in[l])
        w_out_l = w_out[l].astype(BF16)
        wqt = pk_wq[l].T.astype(BF16)
        keys = pk_keys[l].reshape(2 * PK_HEADS, PK_NKEYS, PK_QDIM // 2).astype(BF16)
        ws = sg_w[l].astype(BF16)
        bs_col = sg_b[l].reshape(SG_GROUPS, SG_CHUNK, 1)
        bias = _na_bias_table(na_rpb[l], ts // GRID_W)
        mod_p = mods[l, 0:1]
        mod_s = mods[l, 1:1 + bs]

        proj = _in_proj(xp, mod_p, w_in_l, n_p // TM_IN)
        proj3 = proj.reshape(bp, tp, IN_WIDTH)
        oa, oc = _ctx_attn(proj3, df_lambda[l], df_subln_g[l], lam_init)
        ob = _spatial_gate(proj3, sg_ln_g[l], sg_ln_b[l], ws, bs_col)
        caches_out = _cache_out(proj3, caches_out, l)
        xp = _peer_block(oa.reshape(n_p, NA_WIDTH), ob.reshape(n_p, SG_WIDTH), oc.reshape(n_p, DF_WIDTH),
                         xp, mod_p, n_p // TM_OUT, w_out_l, ln_g[l], ln_b[l], wqt, keys, u_tabs, vt_tabs, l)

        proj = _in_proj(xs, mod_s, w_in_l, ts // TM_IN)
        proj3 = proj.reshape(bs, ts, IN_WIDTH)
        oa = _lat_na(proj3, cache_na_k, cache_na_v, bias, l)
        ob = _spatial_gate(proj3, sg_ln_g[l], sg_ln_b[l], ws, bs_col)
        oc = _lat_df(proj3, cache_dk, cache_df_v, cos_t, sin_t, df_lambda[l], df_subln_g[l], lam_init, l)
        xs = _peer_block(oa.reshape(n_s, NA_WIDTH), ob.reshape(n_s, SG_WIDTH), oc.reshape(n_s, DF_WIDTH),
                         xs, mod_s, ts // TM_OUT, w_out_l, ln_g[l], ln_b[l], wqt, keys, u_tabs, vt_tabs, l)

    return (xp.reshape(bp, tp, d), xs.reshape(bs, ts, d), *caches_out)
```
